```python
import math
import jax, jax.numpy as jnp
from jax import lax
import numpy as np

D_MODEL = 4096
BATCH = 4
SEQ = 4096
DEPTH = 1

GRID_W = 64
CTX_LEN = 256
W_HYENA = D_MODEL // 2
W_LRU = D_MODEL - W_HYENA
HY_SHORT = 3
HY_SHORT_LEFT = 1
FILT_EMB = 33
FILT_BANDS = (FILT_EMB - 1) // 2
FILT_HID = 64
DECAY_TARGET = 1e-2
FAST_DECAY_PCT = 0.3
SLOW_DECAY_PCT = 1.5
MIN_DECAY = math.log(DECAY_TARGET) / SLOW_DECAY_PCT
MAX_DECAY = math.log(DECAY_TARGET) / FAST_DECAY_PCT
LRU_HEADS = 16
LRU_HEAD_DIM = W_LRU // LRU_HEADS
LRU_CONV = 4
LRU_CONV_LEFT = 2
LRU_C = 8.0
N_EXPERTS = 16
EXPERT_FF = D_MODEL // 2
CAPACITY_FACTOR = 2
N_MOD = 6
EPS = 1e-6
D_IN = 3 * W_HYENA + 2 * W_LRU
IN_LRU_GATE = 3 * W_HYENA
IN_LRU_X = 3 * W_HYENA + W_LRU

kernel_name = "hymba_hyena_rglru_ecmoe_dit"


def rmsnorm(x, g):
    x32 = x.astype(jnp.float32)
    y = x32 * lax.rsqrt(jnp.mean(x32 * x32, axis=-1, keepdims=True) + EPS)
    return (y * g.astype(jnp.float32)).astype(x.dtype)


def modulate(h, shift, scale):
    return h * (1 + scale) + shift


def dwconv(x, w, b, left):
    width = w.shape[0]
    n = x.shape[1]
    xp = jnp.pad(x, ((0, 0), (left, width - 1 - left), (0, 0)))
    y = b
    for k in range(width):
        y = y + w[k] * xp[:, k:k + n]
    return y


def grid_dwconv(x, w, b, left, rows):
    bsz, n, ch = x.shape
    y = dwconv(x.reshape(bsz * rows, GRID_W, ch), w, b, left)
    return y.reshape(bsz, n, ch)


def hyena_filter(n, w1, b1, w2, b2, w3, b3, wout, freq):
    f32 = jnp.float32
    pos = jnp.arange(n, dtype=f32)
    t = jnp.linspace(0.0, 1.0, n, dtype=f32)[:, None]
    bands = jnp.linspace(1e-4, FILT_BANDS - 1, FILT_BANDS, dtype=f32)
    ang = (2.0 * math.pi * pos / n)[:, None] * bands[None, :]
    z = jnp.concatenate([t, jnp.cos(ang), -jnp.sin(ang)], axis=-1)
    fr = freq.astype(f32)
    h = jnp.sin(fr * (z @ w1.astype(f32) + b1.astype(f32)))
    h = jnp.sin(fr * (h @ w2.astype(f32) + b2.astype(f32)))
    h = jnp.sin(fr * (h @ w3.astype(f32) + b3.astype(f32)))
    h = (h @ wout.astype(f32)).reshape(n, 2, W_HYENA)
    deltas = jnp.abs(jnp.linspace(MIN_DECAY, MAX_DECAY, W_HYENA, dtype=f32))
    decay = jnp.exp(-t * deltas[None, :])
    h = h * decay[:, None, :]
    h = h / jnp.sum(jnp.abs(h), axis=(0, 1), keepdims=True)
    return jnp.concatenate([h[:, 0], jnp.zeros((1, W_HYENA), f32), h[:0:-1, 1]], axis=0)


def fftconv(u, kern, bias):
    n = u.shape[1]
    u32 = u.astype(jnp.float32)
    uf = jnp.fft.rfft(u32, n=2 * n, axis=1)
    kf = jnp.fft.rfft(kern, n=2 * n, axis=0)
    y = jnp.fft.irfft(uf * kf[None], n=2 * n, axis=1)[:, :n]
    return y + u32 * bias.astype(jnp.float32)


def hyena_mix(u, kern, bias):
    x0 = u[..., :W_HYENA]
    x1 = u[..., W_HYENA:2 * W_HYENA]
    v = u[..., 2 * W_HYENA:]
    z = fftconv(x1 * v, kern, bias)
    return (x0.astype(jnp.float32) * z).astype(u.dtype)


def block_diag(x, w, b):
    bsz, n, _ = x.shape
    y = jnp.einsum('bnhi,hij->bnhj', x.reshape(bsz, n, LRU_HEADS, LRU_HEAD_DIM), w)
    return y.reshape(bsz, n, W_LRU) + b


def rglru_coeffs(xc, wa, ba, wx, bx, lam):
    r = jax.nn.sigmoid(block_diag(xc, wa, ba)).astype(jnp.float32)
    i = jax.nn.sigmoid(block_diag(xc, wx, bx))
    log_a = -LRU_C * r * jax.nn.softplus(-lam.astype(jnp.float32))
    a = jnp.exp(log_a)
    b = jnp.sqrt(-jnp.expm1(2.0 * log_a)) * (i * xc).astype(jnp.float32)
    return a, b


def linear_scan(a, b, h0, reverse):
    def step(h, ab):
        h = ab[0] * h + ab[1]
        return h, h
    h_last, hs = lax.scan(step, h0, (jnp.swapaxes(a, 0, 1), jnp.swapaxes(b, 0, 1)), reverse=reverse)
    return jnp.swapaxes(hs, 0, 1), h_last


def rglru_bidir(xc, wa, ba, wx, bx, lam, h0_f, h0_b):
    a_f, b_f = rglru_coeffs(xc, wa[0], ba[0], wx[0], bx[0], lam[0])
    hs_f, hT_f = linear_scan(a_f, b_f, h0_f, False)
    a_b, b_b = rglru_coeffs(xc, wa[1], ba[1], wx[1], bx[1], lam[1])
    hs_b, hT_b = linear_scan(a_b, b_b, h0_b, True)
    return hs_f, hs_b, hT_f, hT_b


def expert_choice_ffn(h, w_router, w_g, w_u, w_d):
    bsz, n, d = h.shape
    cap = CAPACITY_FACTOR * n // N_EXPERTS
    aff = jax.nn.softmax(jnp.einsum('bnd,de->bne', h, w_router).astype(jnp.float32), axis=-1)
    gates, idx = lax.top_k(jnp.swapaxes(aff, 1, 2), cap)
    xs = jax.vmap(lambda hb, ib: hb[ib])(h, idx)
    ga = jnp.einsum('becd,edf->becf', xs, w_g)
    up = jnp.einsum('becd,edf->becf', xs, w_u)
    y = jnp.einsum('becf,efd->becd', jax.nn.silu(ga) * up, w_d) * gates[..., None].astype(h.dtype)
    return jax.vmap(lambda ib, yb: jnp.zeros((n, d), h.dtype).at[ib.reshape(-1)].add(yb.reshape(-1, d)))(idx, y)


def setup_inputs(seed: int = 0) -> dict:
    key = jax.random.key(seed)
    ks = jax.random.split(key, 40)
    f32 = jnp.float32

    def nrm(k, shape, scale):
        return jax.random.normal(k, shape, f32) * scale

    L = DEPTH
    u_lam = jax.random.uniform(ks[26], (L, 2, W_LRU), f32, minval=0.9, maxval=0.999)
    a0 = u_lam ** (1.0 / LRU_C)
    lru_lambda = jnp.log(a0) - jnp.log1p(-a0)
    return {
        'x': nrm(ks[0], (BATCH, SEQ, D_MODEL), 1.0),
        'c': nrm(ks[1], (BATCH, D_MODEL), 1.0),
        'ctx': nrm(ks[2], (BATCH, CTX_LEN, D_MODEL), 1.0),
        'c_ctx': nrm(ks[3], (D_MODEL,), 1.0),
        'w_mod': nrm(ks[4], (L, D_MODEL, N_MOD * D_MODEL), 0.5 * D_MODEL ** -0.5),
        'b_mod': nrm(ks[5], (L, N_MOD * D_MODEL), 0.02),
        'g_mix': 1.0 + nrm(ks[6], (L, D_MODEL), 0.02),
        'g_ffn': 1.0 + nrm(ks[7], (L, D_MODEL), 0.02),
        'w_in': nrm(ks[8], (L, D_MODEL, D_IN), D_MODEL ** -0.5),
        'b_in': nrm(ks[9], (L, D_IN), 0.02),
        'hy_conv_w': nrm(ks[10], (L, HY_SHORT, 3 * W_HYENA), HY_SHORT ** -0.5),
        'hy_conv_b': nrm(ks[11], (L, 3 * W_HYENA), 0.02),
        'hy_f_w1': nrm(ks[12], (L, FILT_EMB, FILT_HID), FILT_EMB ** -0.5),
        'hy_f_b1': nrm(ks[13], (L, FILT_HID), 0.02),
        'hy_f_w2': nrm(ks[14], (L, FILT_HID, FILT_HID), FILT_HID ** -0.5),
        'hy_f_b2': nrm(ks[15], (L, FILT_HID), 0.02),
        'hy_f_w3': nrm(ks[16], (L, FILT_HID, FILT_HID), FILT_HID ** -0.5),
        'hy_f_b3': nrm(ks[17], (L, FILT_HID), 0.02),
        'hy_f_wout': nrm(ks[18], (L, FILT_HID, 2 * W_HYENA), FILT_HID ** -0.5),
        'hy_f_freq': 1.0 + nrm(ks[19], (L, FILT_HID), 0.1),
        'hy_bias': nrm(ks[20], (L, W_HYENA), 0.5),
        'lru_conv_w': nrm(ks[21], (L, LRU_CONV, W_LRU), 0.5),
        'lru_conv_b': nrm(ks[22], (L, W_LRU), 0.02),
        'lru_wa': nrm(ks[23], (L, 2, LRU_HEADS, LRU_HEAD_DIM, LRU_HEAD_DIM), LRU_HEAD_DIM ** -0.5),
        'lru_ba': nrm(ks[24], (L, 2, W_LRU), 0.02),
        'lru_wx': nrm(ks[25], (L, 2, LRU_HEADS, LRU_HEAD_DIM, LRU_HEAD_DIM), LRU_HEAD_DIM ** -0.5),
        'lru_bx': nrm(ks[27], (L, 2, W_LRU), 0.02),
        'lru_lambda': lru_lambda,
        'w_out': nrm(ks[28], (L, D_MODEL, D_MODEL), D_MODEL ** -0.5),
        'b_out': nrm(ks[29], (L, D_MODEL), 0.02),
        'w_router': nrm(ks[30], (L, D_MODEL, N_EXPERTS), D_MODEL ** -0.5),
        'w_exp_gate': nrm(ks[31], (L, N_EXPERTS, D_MODEL, EXPERT_FF), D_MODEL ** -0.5),
        'w_exp_up': nrm(ks[32], (L, N_EXPERTS, D_MODEL, EXPERT_FF), D_MODEL ** -0.5),
        'w_exp_down': nrm(ks[33], (L, N_EXPERTS, EXPERT_FF, D_MODEL), EXPERT_FF ** -0.5),
        'g_final': 1.0 + nrm(ks[34], (D_MODEL,), 0.02),
    }


def reference(x, c, ctx, c_ctx, w_mod, b_mod, g_mix, g_ffn, w_in, b_in, hy_conv_w, hy_conv_b,
              hy_f_w1, hy_f_b1, hy_f_w2, hy_f_b2, hy_f_w3, hy_f_b3, hy_f_wout, hy_f_freq, hy_bias,
              lru_conv_w, lru_conv_b, lru_wa, lru_ba, lru_wx, lru_bx, lru_lambda, w_out, b_out,
              w_router, w_exp_gate, w_exp_up, w_exp_down, g_final):
    bsz, n_lat, _ = x.shape
    n_ctx = ctx.shape[1]
    rows = n_lat // GRID_W
    silu_c = jax.nn.silu(c)
    silu_cc = jax.nn.silu(c_ctx)
    for l in range(DEPTH):
        ctx_needed = l < DEPTH - 1
        mod_x = (silu_c @ w_mod[l] + b_mod[l])[:, None, :]
        sh1, sc1, gt1, sh2, sc2, gt2 = jnp.split(mod_x, N_MOD, axis=-1)
        mod_c = silu_cc @ w_mod[l] + b_mod[l]
        csh1, csc1, cgt1, csh2, csc2, cgt2 = jnp.split(mod_c, N_MOD, axis=-1)
        filt = (hy_f_w1[l], hy_f_b1[l], hy_f_w2[l], hy_f_b2[l], hy_f_w3[l], hy_f_b3[l], hy_f_wout[l], hy_f_freq[l])
        lru_p = (lru_wa[l], lru_ba[l], lru_wx[l], lru_bx[l], lru_lambda[l])

        hx = modulate(rmsnorm(x, g_mix[l]), sh1, sc1)
        hc = modulate(rmsnorm(ctx, g_mix[l]), csh1, csc1)
        px = hx @ w_in[l] + b_in[l]
        if ctx_needed:
            pc = hc @ w_in[l] + b_in[l]
            pc_lx = pc[..., IN_LRU_X:]
        else:
            pc_lx = hc @ w_in[l][:, IN_LRU_X:] + b_in[l][IN_LRU_X:]

        xc_c = dwconv(pc_lx, lru_conv_w[l], lru_conv_b[l], LRU_CONV_LEFT)
        h_zero = jnp.zeros((bsz, W_LRU), jnp.float32)
        hcf, hcb, hT_f, hT_b = rglru_bidir(xc_c, *lru_p, h_zero, h_zero)

        u_x = grid_dwconv(px[..., :IN_LRU_GATE], hy_conv_w[l], hy_conv_b[l], HY_SHORT_LEFT, rows)
        y_hy = hyena_mix(u_x, hyena_filter(n_lat, *filt), hy_bias[l])
        xc_x = grid_dwconv(px[..., IN_LRU_X:], lru_conv_w[l], lru_conv_b[l], LRU_CONV_LEFT, rows)
        hxf, hxb, _, _ = rglru_bidir(xc_x, *lru_p, hT_f, hT_b)
        y_lru = jax.nn.gelu(px[..., IN_LRU_GATE:IN_LRU_X]) * (hxf + hxb).astype(px.dtype)
        y_x = jnp.concatenate([y_hy, y_lru], axis=-1) @ w_out[l] + b_out[l]
        x = x + gt1 * y_x

        if ctx_needed:
            u_c = dwconv(pc[..., :IN_LRU_GATE], hy_conv_w[l], hy_conv_b[l], HY_SHORT_LEFT)
            yc_hy = hyena_mix(u_c, hyena_filter(n_ctx, *filt), hy_bias[l])
            yc_lru = jax.nn.gelu(pc[..., IN_LRU_GATE:IN_LRU_X]) * (hcf + hcb).astype(pc.dtype)
            y_c = jnp.concatenate([yc_hy, yc_lru], axis=-1) @ w_out[l] + b_out[l]
            ctx = ctx + cgt1 * y_c
            hc2 = modulate(rmsnorm(ctx, g_ffn[l]), csh2, csc2)
            ctx = ctx + cgt2 * expert_choice_ffn(hc2, w_router[l], w_exp_gate[l], w_exp_up[l], w_exp_down[l])

        hx2 = modulate(rmsnorm(x, g_ffn[l]), sh2, sc2)
        x = x + gt2 * expert_choice_ffn(hx2, w_router[l], w_exp_gate[l], w_exp_up[l], w_exp_down[l])
    return rmsnorm(x, g_final)
```

```python
import functools
import math

import jax
import jax.numpy as jnp
from jax import lax
from jax.experimental import pallas as pl
from jax.experimental.pallas import tpu as pltpu

F32 = jnp.float32
BF16 = jnp.bfloat16

GRID_W = 64
HY_SHORT_LEFT = 1
FILT_BANDS = 16
DECAY_TARGET = 1e-2
MIN_DECAY = math.log(DECAY_TARGET) / 1.5
MAX_DECAY = math.log(DECAY_TARGET) / 0.3
LRU_HEADS = 16
LRU_CONV_LEFT = 2
LRU_C = 8.0
CAPACITY_FACTOR = 2
N_MOD = 6
EPS = 1e-6

LANES = 128
SUBLANES = 8
ROW_SLAB = 8
DFT_ROWS = 64
VMEM_LIMIT = 56 * 2 ** 20


def _params(sem):
    return pltpu.CompilerParams(dimension_semantics=sem, vmem_limit_bytes=VMEM_LIMIT)


def _tile(n, pref):
    t = min(n, pref)
    while n % t:
        t //= 2
    return t


def _split_bf16(v):
    hi = v.astype(BF16)
    lo = (v - hi.astype(F32)).astype(BF16)
    return hi, lo


def _mod_kernel(c_ref, w_ref, b_ref, o_ref):
    c = c_ref[...]
    s = c * jax.nn.sigmoid(c)
    s_hi, s_lo = _split_bf16(s)
    w_hi, w_lo = _split_bf16(w_ref[...])
    rows = c.shape[0]
    r = jnp.dot(jnp.concatenate([s_hi, s_lo], axis=0), w_hi, preferred_element_type=F32)
    acc = r[:rows] + r[rows:] + jnp.dot(s_hi, w_lo, preferred_element_type=F32)
    o_ref[...] = acc + b_ref[...]


def _modulation(c_all, w_mod, b_mod):
    rows, d = c_all.shape
    n = w_mod.shape[1]
    tn = _tile(n, 256)
    return pl.pallas_call(
        _mod_kernel,
        grid=(n // tn,),
        in_specs=[pl.BlockSpec((rows, d), lambda j: (0, 0)),
                  pl.BlockSpec((d, tn), lambda j: (0, j)),
                  pl.BlockSpec((1, tn), lambda j: (0, j))],
        out_specs=pl.BlockSpec((rows, tn), lambda j: (0, j)),
        out_shape=jax.ShapeDtypeStruct((rows, n), F32),
        compiler_params=_params(("parallel",)),
        name="modulation",
    )(c_all, w_mod, b_mod.reshape(1, n))


def _rms_mod(x, g, shift, scale):
    y = x * lax.rsqrt(jnp.mean(x * x, axis=-1, keepdims=True) + EPS) * g
    return y * (1.0 + scale) + shift


def _norm_kernel(x_ref, g_ref, sh_ref, sc_ref, o_ref):
    o_ref[0] = _rms_mod(x_ref[0], g_ref[...], sh_ref[0], sc_ref[0]).astype(o_ref.dtype)


def _norm_mod(x, g, shift, scale, per_batch):
    b, t, d = x.shape
    tt = _tile(t, 512)
    mod_map = (lambda i, j: (i, 0, 0)) if per_batch else (lambda i, j: (0, 0, 0))
    return pl.pallas_call(
        _norm_kernel,
        grid=(b, t // tt),
        in_specs=[pl.BlockSpec((1, tt, d), lambda i, j: (i, j, 0)),
                  pl.BlockSpec((1, d), lambda i, j: (0, 0)),
                  pl.BlockSpec((1, 1, d), mod_map),
                  pl.BlockSpec((1, 1, d), mod_map)],
        out_specs=pl.BlockSpec((1, tt, d), lambda i, j: (i, j, 0)),
        out_shape=jax.ShapeDtypeStruct((b, t, d), BF16),
        compiler_params=_params(("parallel", "parallel")),
        name="adaln_norm",
    )(x, g.reshape(1, d), shift, scale)


def _proj_kernel(a_ref, w_ref, b_ref, o_ref):
    acc = jnp.dot(a_ref[...], w_ref[...], preferred_element_type=F32)
    o_ref[...] = (acc + b_ref[...]).astype(o_ref.dtype)


def _projection(a, w, bias, col_start, n_cols, out_dtype):
    m, k = a.shape
    tm = _tile(m, 1024)
    tn = _tile(n_cols, 512)
    off = col_start // tn
    return pl.pallas_call(
        _proj_kernel,
        grid=(m // tm, n_cols // tn),
        in_specs=[pl.BlockSpec((tm, k), lambda i, j: (i, 0)),
                  pl.BlockSpec((k, tn), lambda i, j: (0, j + off)),
                  pl.BlockSpec((1, tn), lambda i, j: (0, j + off))],
        out_specs=pl.BlockSpec((tm, tn), lambda i, j: (i, j)),
        out_shape=jax.ShapeDtypeStruct((m, n_cols), out_dtype),
        compiler_params=_params(("parallel", "arbitrary")),
        name="projection",
    )(a, w, bias.reshape(1, -1))


def _short_conv(x, w_ref, bias, left, group):
    rows = x.shape[0]
    pos = lax.broadcasted_iota(jnp.int32, (rows, 1), 0) & (group - 1)
    y = bias + w_ref[left:left + 1, :] * x
    for k in range(w_ref.shape[0]):
        off = k - left
        if off == 0:
            continue
        shifted = pltpu.roll(x, (-off) % rows, axis=0)
        valid = (pos + off >= 0) & (pos + off < group)
        y = y + w_ref[k:k + 1, :] * jnp.where(valid, shifted, 0.0)
    return y


def _hy_pre_kernel(p0_ref, p1_ref, p2_ref, w0_ref, w1_ref, w2_ref, b0_ref, b1_ref, b2_ref,
                   x0_ref, wf_ref, wb_ref):
    x0 = _short_conv(p0_ref[0], w0_ref, b0_ref[...], HY_SHORT_LEFT, GRID_W)
    x1 = _short_conv(p1_ref[0], w1_ref, b1_ref[...], HY_SHORT_LEFT, GRID_W)
    v = _short_conv(p2_ref[0], w2_ref, b2_ref[...], HY_SHORT_LEFT, GRID_W)
    w = x1 * v
    x0_ref[0] = x0
    wf_ref[0] = w
    wb_ref[0] = w.astype(BF16)


def _hyena_pre(px, conv_w, conv_b, c):
    b, t, _ = px.shape
    tt = _tile(t, 512)
    tc = _tile(c, 512)
    nc = c // tc
    kw = conv_w.shape[0]

    def pspec(g):
        return pl.BlockSpec((1, tt, tc), lambda i, j, k: (i, j, k + g * nc))

    def wspec(g):
        return pl.BlockSpec((kw, tc), lambda i, j, k: (0, k + g * nc))

    def bspec(g):
        return pl.BlockSpec((1, tc), lambda i, j, k: (0, k + g * nc))

    ospec = pl.BlockSpec((1, tt, tc), lambda i, j, k: (i, j, k))
    cb = conv_b.reshape(1, -1)
    return pl.pallas_call(
        _hy_pre_kernel,
        grid=(b, t // tt, nc),
        in_specs=[pspec(0), pspec(1), pspec(2), wspec(0), wspec(1), wspec(2),
                  bspec(0), bspec(1), bspec(2)],
        out_specs=[ospec, ospec, ospec],
        out_shape=[jax.ShapeDtypeStruct((b, t, c), F32),
                   jax.ShapeDtypeStruct((b, t, c), F32),
                   jax.ShapeDtypeStruct((b, t, c), BF16)],
        compiler_params=_params(("parallel", "parallel", "parallel")),
        name="hyena_pre",
    )(px, px, px, conv_w, conv_w, conv_w, cb, cb, cb)


def _hp_dot(a, b):
    return jnp.dot(a, b, preferred_element_type=F32, precision=lax.Precision.HIGHEST)


def _filter_kernel(n, w1t_ref, w1c_ref, w1s_ref, b1_ref, w2_ref, b2_ref, w3_ref, b3_ref,
                   wout_ref, freq_ref, delta_ref, h_ref, norm_ref):
    i = pl.program_id(0)
    tn = h_ref.shape[0]
    pos_i = i * tn + lax.broadcasted_iota(jnp.int32, (tn, 1), 0)
    pos = pos_i.astype(F32)
    t = pos * (1.0 / (n - 1))
    band_step = (FILT_BANDS - 1 - 1e-4) / (FILT_BANDS - 1)
    bands = 1e-4 + band_step * lax.broadcasted_iota(jnp.int32, (1, FILT_BANDS), 1).astype(F32)
    ang = (2.0 * math.pi * pos / n) * bands
    fr = freq_ref[...]
    pre = t * w1t_ref[...] + _hp_dot(jnp.cos(ang), w1c_ref[...]) - _hp_dot(jnp.sin(ang), w1s_ref[...])
    h = jnp.sin(fr * (pre + b1_ref[...]))
    h = jnp.sin(fr * (_hp_dot(h, w2_ref[...]) + b2_ref[...]))
    h = jnp.sin(fr * (_hp_dot(h, w3_ref[...]) + b3_ref[...]))
    taps = _hp_dot(h, wout_ref[...]) * jnp.exp(-t * delta_ref[...])

    @pl.when(i == 0)
    def _():
        norm_ref[...] = jnp.zeros_like(norm_ref)

    norm_ref[...] += jnp.sum(jnp.abs(taps), axis=0, keepdims=True)
    c = taps.shape[1] // 2
    col = lax.broadcasted_iota(jnp.int32, (1, taps.shape[1]), 1)
    drop = (pos_i == 0) & (col >= c)
    h_ref[...] = jnp.where(drop, 0.0, taps).astype(h_ref.dtype)


def _hyena_filter(n, w1, b1, w2, b2, w3, b3, wout, freq):
    hid = w1.shape[1]
    c2 = wout.shape[1]
    c = c2 // 2
    tn = _tile(n, 512)
    deltas = jnp.abs(jnp.linspace(MIN_DECAY, MAX_DECAY, c, dtype=F32))
    deltas = jnp.concatenate([deltas, deltas]).reshape(1, c2)
    full = lambda shape: pl.BlockSpec(shape, lambda i: (0, 0))
    return pl.pallas_call(
        functools.partial(_filter_kernel, n),
        grid=(n // tn,),
        in_specs=[full((1, hid)), full((FILT_BANDS, hid)), full((FILT_BANDS, hid)), full((1, hid)),
                  full((hid, hid)), full((1, hid)), full((hid, hid)), full((1, hid)),
                  full((hid, c2)), full((1, hid)), full((1, c2))],
        out_specs=[pl.BlockSpec((tn, c2), lambda i: (i, 0)), full((1, c2))],
        out_shape=[jax.ShapeDtypeStruct((n, c2), BF16), jax.ShapeDtypeStruct((1, c2), F32)],
        compiler_params=_params(("arbitrary",)),
        name="hyena_filter",
    )(w1[0:1], w1[1:1 + FILT_BANDS], w1[1 + FILT_BANDS:], b1.reshape(1, hid), w2, b2.reshape(1, hid),
      w3, b3.reshape(1, hid), wout, freq.reshape(1, hid), deltas)


def _dft_kernel(n, fre_ref, fim_ref, fimt_ref, tre_ref, tim_ref):
    k = pl.program_id(0)
    n2 = 2 * n
    theta = 2.0 * math.pi / n2
    s = lax.broadcasted_iota(jnp.int32, (1, n), 1)
    fl = lax.broadcasted_iota(jnp.int32, (DFT_ROWS, 1), 0)

    @pl.when(k == 0)
    def _():
        ang = ((fl * s) & (n2 - 1)).astype(F32) * theta
        tre_ref[...] = jnp.cos(ang)
        tim_ref[...] = -jnp.sin(ang)

    ang = ((DFT_ROWS * k * s) & (n2 - 1)).astype(F32) * theta
    rre = jnp.cos(ang)
    rim = -jnp.sin(ang)
    tre = tre_ref[...]
    tim = tim_ref[...]
    fre = rre * tre - rim * tim
    fim = rre * tim + rim * tre
    row = DFT_ROWS * k + fl
    fre_ref[...] = fre.astype(fre_ref.dtype)
    fim_ref[...] = jnp.where(row == 0, (1 - 2 * (s & 1)).astype(F32), fim).astype(fim_ref.dtype)
    fimt_ref[...] = jnp.where(s == 0, (1 - 2 * (row & 1)).astype(F32), fim).astype(fimt_ref.dtype)


def _dft_matrices(n):
    assert n & (n - 1) == 0 and n % DFT_ROWS == 0
    spec = pl.BlockSpec((DFT_ROWS, n), lambda k: (k, 0))
    shape = jax.ShapeDtypeStruct((n, n), BF16)
    return pl.pallas_call(
        functools.partial(_dft_kernel, n),
        grid=(n // DFT_ROWS,),
        out_specs=[spec, spec, spec],
        out_shape=[shape, shape, shape],
        scratch_shapes=[pltpu.VMEM((DFT_ROWS, n), F32), pltpu.VMEM((DFT_ROWS, n), F32)],
        compiler_params=_params(("arbitrary",)),
        name="dft_matrices",
    )()


def _filter_dft_kernel(n, fre_ref, fim_ref, hf_ref, hb_ref, nf_ref, nb_ref, kre_ref, kim_ref):
    i = pl.program_id(1)
    tf = fre_ref.shape[0]
    fre = fre_ref[...]
    fim = fim_ref[...]
    hf = hf_ref[...]
    hb = hb_ref[...]
    row0 = (i * tf + lax.broadcasted_iota(jnp.int32, (tf, 1), 0)) == 0
    scale = jnp.where(row0, 0.5 / n, 1.0 / n) / (nf_ref[...] + nb_ref[...])
    kre = jnp.dot(fre, hf, preferred_element_type=F32) + jnp.dot(fre, hb, preferred_element_type=F32)
    im_f = jnp.dot(fim, hf, preferred_element_type=F32)
    im_b = jnp.dot(fim, hb, preferred_element_type=F32)
    kre_ref[...] = kre * scale
    kim_ref[...] = jnp.where(row0, im_f + im_b, im_f - im_b) * scale


def _filter_spectrum(fre, fim, taps, norm, c):
    n = fre.shape[0]
    tf = _tile(n, 512)
    tc = _tile(c, 512)
    nc = c // tc
    fspec = pl.BlockSpec((tf, n), lambda j, i: (i, 0))
    ospec = pl.BlockSpec((tf, tc), lambda j, i: (i, j))
    shape = jax.ShapeDtypeStruct((n, c), F32)
    return pl.pallas_call(
        functools.partial(_filter_dft_kernel, n),
        grid=(nc, n // tf),
        in_specs=[fspec, fspec,
                  pl.BlockSpec((n, tc), lambda j, i: (0, j)),
                  pl.BlockSpec((n, tc), lambda j, i: (0, j + nc)),
                  pl.BlockSpec((1, tc), lambda j, i: (0, j)),
                  pl.BlockSpec((1, tc), lambda j, i: (0, j + nc))],
        out_specs=[ospec, ospec],
        out_shape=[shape, shape],
        compiler_params=_params(("parallel", "arbitrary")),
        name="filter_spectrum",
    )(fre, fim, taps, taps, norm, norm)


def _conv_fwd_kernel(fre_ref, fim_ref, w_ref, kre_ref, kim_ref, yre_ref, yim_ref):
    i = pl.program_id(2)
    tf = fre_ref.shape[0]
    w = w_ref[0]
    ure = jnp.dot(fre_ref[...], w, preferred_element_type=F32)
    uim = jnp.dot(fim_ref[...], w, preferred_element_type=F32)
    kre = kre_ref[...]
    kim = kim_ref[...]
    row0 = (i * tf + lax.broadcasted_iota(jnp.int32, (tf, 1), 0)) == 0
    yre_ref[0] = jnp.where(row0, ure * kre, ure * kre - uim * kim).astype(yre_ref.dtype)
    yim_ref[0] = jnp.where(row0, uim * kim, ure * kim + uim * kre).astype(yim_ref.dtype)


def _conv_forward(fre, fim, wb, kre, kim):
    b, n, c = wb.shape
    tf = _tile(n, 512)
    tc = _tile(c, 512)
    fspec = pl.BlockSpec((tf, n), lambda bi, j, i: (i, 0))
    kspec = pl.BlockSpec((tf, tc), lambda bi, j, i: (i, j))
    ospec = pl.BlockSpec((1, tf, tc), lambda bi, j, i: (bi, i, j))
    shape = jax.ShapeDtypeStruct((b, n, c), BF16)
    return pl.pallas_call(
        _conv_fwd_kernel,
        grid=(b, c // tc, n // tf),
        in_specs=[fspec, fspec, pl.BlockSpec((1, n, tc), lambda bi, j, i: (bi, 0, j)), kspec, kspec],
        out_specs=[ospec, ospec],
        out_shape=[shape, shape],
        compiler_params=_params(("parallel", "parallel", "arbitrary")),
        name="conv_forward_dft",
    )(fre, fim, wb, kre, kim)


def _conv_inv_kernel(fre_ref, fimt_ref, yre_ref, yim_ref, wf_ref, x0_ref, bias_ref, o_ref):
    z = jnp.dot(fre_ref[...], yre_ref[0], preferred_element_type=F32)
    z = z + jnp.dot(fimt_ref[...], yim_ref[0], preferred_element_type=F32)
    z = z + wf_ref[0] * bias_ref[...]
    o_ref[0] = (x0_ref[0] * z).astype(o_ref.dtype)


def _conv_inverse(fre, fimt, yre, yim, wf, x0, bias):
    b, n, c = wf.shape
    tt = _tile(n, 512)
    tc = _tile(c, 512)
    fspec = pl.BlockSpec((tt, n), lambda bi, j, i: (i, 0))
    yspec = pl.BlockSpec((1, n, tc), lambda bi, j, i: (bi, 0, j))
    espec = pl.BlockSpec((1, tt, tc), lambda bi, j, i: (bi, i, j))
    return pl.pallas_call(
        _conv_inv_kernel,
        grid=(b, c // tc, n // tt),
        in_specs=[fspec, fspec, yspec, yspec, espec, espec,
                  pl.BlockSpec((1, tc), lambda bi, j, i: (0, j))],
        out_specs=espec,
        out_shape=jax.ShapeDtypeStruct((b, n, c), BF16),
        compiler_params=_params(("parallel", "parallel", "arbitrary")),
        name="conv_inverse_dft",
    )(fre, fimt, yre, yim, wf, x0, bias.reshape(1, c))


SCAN_UNROLL = 8


def _lru_kernel(group, has_gate, *refs):
    if has_gate:
        (x_ref, gate_ref, cw_ref, cb_ref, wa_ref, ba_ref, wx_ref, bx_ref, lam_ref, h0_ref,
         y_ref, ht_ref, af_ref, bf_ref, ab_ref, bb_ref) = refs
    else:
        (x_ref, cw_ref, cb_ref, wa_ref, ba_ref, wx_ref, bx_ref, lam_ref, h0_ref,
         ht_ref, af_ref, bf_ref, ab_ref, bb_ref) = refs
    t_len = x_ref.shape[1]
    chunk = _tile(t_len, 512)
    a_refs = (af_ref, ab_ref)
    b_refs = (bf_ref, bb_ref)

    def coeffs(ci, carry):
        r0 = pl.multiple_of(ci * chunk, chunk)
        xc = _short_conv(x_ref[0, pl.ds(r0, chunk), :], cw_ref, cb_ref[...], LRU_CONV_LEFT, group)
        xcb = xc.astype(BF16)
        for d in range(2):
            r = jax.nn.sigmoid(jnp.dot(xcb, wa_ref[d, 0], preferred_element_type=F32) + ba_ref[d])
            gi = jax.nn.sigmoid(jnp.dot(xcb, wx_ref[d, 0], preferred_element_type=F32) + bx_ref[d])
            lam = lam_ref[d]
            softplus_neg = jnp.maximum(-lam, 0.0) + jnp.log(1.0 + jnp.exp(-jnp.abs(lam)))
            log_a = -LRU_C * r * softplus_neg
            a = jnp.exp(log_a)
            a_refs[d][pl.ds(r0, chunk), :] = a
            b_refs[d][pl.ds(r0, chunk), :] = jnp.sqrt(1.0 - a * a) * (gi * xc)
        return carry

    lax.fori_loop(0, t_len // chunk, coeffs, 0)

    def scan(i, carry):
        hf, hb = carry
        for r in range(SCAN_UNROLL):
            tf = i * SCAN_UNROLL + r
            hf = af_ref[pl.ds(tf, 1), :] * hf + bf_ref[pl.ds(tf, 1), :]
            bf_ref[pl.ds(tf, 1), :] = hf
            tb = t_len - 1 - tf
            hb = ab_ref[pl.ds(tb, 1), :] * hb + bb_ref[pl.ds(tb, 1), :]
            bb_ref[pl.ds(tb, 1), :] = hb
        return hf, hb

    hf, hb = lax.fori_loop(0, t_len // SCAN_UNROLL, scan, (h0_ref[0, 0:1, :], h0_ref[0, 1:2, :]))
    ht_ref[0, 0:1, :] = hf
    ht_ref[0, 1:2, :] = hb

    if has_gate:
        def emit(ci, carry):
            r0 = pl.multiple_of(ci * chunk, chunk)
            hs = bf_ref[pl.ds(r0, chunk), :] + bb_ref[pl.ds(r0, chunk), :]
            y_ref[0, pl.ds(r0, chunk), :] = (jax.nn.gelu(gate_ref[0, pl.ds(r0, chunk), :]) * hs).astype(y_ref.dtype)
            return carry

        lax.fori_loop(0, t_len // chunk, emit, 0)


def _block_diag(w, heads_per_tile):
    d2, h, hd, _ = w.shape
    w = w.reshape(d2, h // heads_per_tile, heads_per_tile, hd, hd)
    eye = jnp.eye(heads_per_tile, dtype=w.dtype)
    bd = jnp.einsum('dghij,hq->dghiqj', w, eye)
    return bd.reshape(d2, h // heads_per_tile, heads_per_tile * hd, heads_per_tile * hd).astype(BF16)


def _rglru(xsrc, x_col, gsrc, g_col, conv_w, conv_b, wa, ba, wx, bx, lam, h0, group):
    b, t, _ = xsrc.shape
    c = conv_w.shape[1]
    hd = c // LRU_HEADS
    tc = min(c, max(hd, 256))
    hp = tc // hd
    nt = c // tc
    has_gate = gsrc is not None
    wa_bd = _block_diag(wa, hp)
    wx_bd = _block_diag(wx, hp)
    xo, go = x_col // tc, (g_col // tc if has_gate else 0)
    vec = lambda: pl.BlockSpec((2, 1, tc), lambda i, j: (0, 0, j))
    mat = lambda: pl.BlockSpec((2, 1, tc, tc), lambda i, j: (0, j, 0, 0))
    in_specs = [pl.BlockSpec((1, t, tc), lambda i, j: (i, 0, j + xo))]
    args = [xsrc]
    if has_gate:
        in_specs.append(pl.BlockSpec((1, t, tc), lambda i, j: (i, 0, j + go)))
        args.append(gsrc)
    in_specs += [pl.BlockSpec((conv_w.shape[0], tc), lambda i, j: (0, j)),
                 pl.BlockSpec((1, tc), lambda i, j: (0, j)),
                 mat(), vec(), mat(), vec(), vec(),
                 pl.BlockSpec((1, 2, tc), lambda i, j: (i, 0, j))]
    args += [conv_w, conv_b.reshape(1, c), wa_bd, ba.reshape(2, 1, c), wx_bd, bx.reshape(2, 1, c),
             lam.reshape(2, 1, c), h0]
    ht_spec = pl.BlockSpec((1, 2, tc), lambda i, j: (i, 0, j))
    ht_shape = jax.ShapeDtypeStruct((b, 2, c), F32)
    if has_gate:
        out_specs = [pl.BlockSpec((1, t, tc), lambda i, j: (i, 0, j)), ht_spec]
        out_shape = [jax.ShapeDtypeStruct((b, t, c), BF16), ht_shape]
    else:
        out_specs = [ht_spec]
        out_shape = [ht_shape]
    return pl.pallas_call(
        functools.partial(_lru_kernel, group, has_gate),
        grid=(b, nt),
        in_specs=in_specs,
        out_specs=out_specs,
        out_shape=out_shape,
        scratch_shapes=[pltpu.VMEM((t, tc), F32) for _ in range(4)],
        compiler_params=_params(("parallel", "parallel")),
        name="rglru" if has_gate else "rglru_context",
    )(*args)


def _out_proj_kernel(a1_ref, a2_ref, w1_ref, w2_ref, b_ref, x_ref, g_ref, o_ref):
    acc = jnp.dot(a1_ref[...], w1_ref[...], preferred_element_type=F32)
    acc = acc + jnp.dot(a2_ref[...], w2_ref[...], preferred_element_type=F32)
    o_ref[...] = x_ref[...] + g_ref[0] * (acc + b_ref[...])


def _out_projection(a1, a2, w, bias, x, gate, t):
    m, k1 = a1.shape
    n = w.shape[1]
    tm = _tile(t, 1024)
    tn = _tile(n, 512)
    per_b = t // tm
    k1_blocks = 1
    return pl.pallas_call(
        _out_proj_kernel,
        grid=(m // tm, n // tn),
        in_specs=[pl.BlockSpec((tm, k1), lambda i, j: (i, 0)),
                  pl.BlockSpec((tm, k1), lambda i, j: (i, 0)),
                  pl.BlockSpec((k1, tn), lambda i, j: (0, j)),
                  pl.BlockSpec((k1, tn), lambda i, j: (k1_blocks, j)),
                  pl.BlockSpec((1, tn), lambda i, j: (0, j)),
                  pl.BlockSpec((tm, tn), lambda i, j: (i, j)),
                  pl.BlockSpec((1, 1, tn), lambda i, j: (i // per_b, 0, j))],
        out_specs=pl.BlockSpec((tm, tn), lambda i, j: (i, j)),
        out_shape=jax.ShapeDtypeStruct((m, n), F32),
        compiler_params=_params(("parallel", "arbitrary")),
        name="out_projection",
    )(a1, a2, w, w, bias.reshape(1, n), x, gate)


def _router_kernel(x_ref, g_ref, sh_ref, sc_ref, wr_ref, h_ref, aff_ref):
    h = _rms_mod(x_ref[0], g_ref[...], sh_ref[0], sc_ref[0])
    seg = h.shape[1] // ROW_SLAB
    for s in range(ROW_SLAB):
        h_ref[:, s, :] = h[:, s * seg:(s + 1) * seg]
    h_hi, h_lo = _split_bf16(h)
    w_hi, w_lo = _split_bf16(wr_ref[...])
    nt = (((1,), (1,)), ((), ()))
    logits = (lax.dot_general(w_hi, h_hi, nt, preferred_element_type=F32)
              + lax.dot_general(w_lo, h_hi, nt, preferred_element_type=F32)
              + lax.dot_general(w_hi, h_lo, nt, preferred_element_type=F32))
    z = jnp.exp(logits - jnp.max(logits, axis=0, keepdims=True))
    aff_ref[0] = z / jnp.sum(z, axis=0, keepdims=True)


def _router(x1, g, shift, scale, w_router):
    b, t, d = x1.shape
    e = w_router.shape[1]
    tt = _tile(t, 256)
    per_b = t // tt
    mod_map = lambda i, j: (i, 0, 0)
    return pl.pallas_call(
        _router_kernel,
        grid=(b, per_b),
        in_specs=[pl.BlockSpec((1, tt, d), lambda i, j: (i, j, 0)),
                  pl.BlockSpec((1, d), lambda i, j: (0, 0)),
                  pl.BlockSpec((1, 1, d), mod_map),
                  pl.BlockSpec((1, 1, d), mod_map),
                  pl.BlockSpec((e, d), lambda i, j: (0, 0))],
        out_specs=[pl.BlockSpec((tt, ROW_SLAB, d // ROW_SLAB), lambda i, j: (i * per_b + j, 0, 0)),
                   pl.BlockSpec((1, e, tt), lambda i, j: (i, 0, j))],
        out_shape=[jax.ShapeDtypeStruct((b * t, ROW_SLAB, d // ROW_SLAB), F32),
                   jax.ShapeDtypeStruct((b, e, t), F32)],
        compiler_params=_params(("parallel", "parallel")),
        name="router",
    )(x1, g.reshape(1, d), shift, scale, w_router.T)


def _prefix_count(mask_ref, out_ref):
    e, t = mask_ref.shape
    blk = min(t, LANES)
    tri = (lax.broadcasted_iota(jnp.int32, (blk, blk), 0)
           < lax.broadcasted_iota(jnp.int32, (blk, blk), 1)).astype(BF16)
    carry = jnp.zeros((e, 1), F32)
    for k in range(t // blk):
        m = mask_ref[:, k * blk:(k + 1) * blk]
        out_ref[:, k * blk:(k + 1) * blk] = jnp.dot(m.astype(BF16), tri, preferred_element_type=F32) + carry
        carry = carry + jnp.sum(m, axis=1, keepdims=True)
    return carry


def _topk_kernel(cap, aff_ref, idx_ref, gate_ref, mask_ref, pos_ref):
    a = aff_ref[0]
    e, t = a.shape
    bits = pltpu.bitcast(a, jnp.int32)

    def refine(i, thr):
        cand = thr | jnp.left_shift(jnp.int32(1), 30 - i)
        cnt = jnp.sum(jnp.where(bits >= cand, 1.0, 0.0), axis=1, keepdims=True)
        return jnp.where(cnt >= cap, cand, thr)

    thr = lax.fori_loop(0, 31, refine, jnp.zeros((e, 1), jnp.int32))
    above = bits > thr
    tied = bits == thr
    need = cap - jnp.sum(jnp.where(above, 1.0, 0.0), axis=1, keepdims=True)
    mask_ref[...] = jnp.where(tied, 1.0, 0.0)
    _prefix_count(mask_ref, pos_ref)
    sel = above | (tied & (pos_ref[...] < need))
    mask_ref[...] = jnp.where(sel, 1.0, 0.0)
    _prefix_count(mask_ref, pos_ref)

    tok = lax.broadcasted_iota(jnp.int32, (1, t), 1)
    tok_hi = (tok >> 6).astype(F32)
    tok_lo = (tok & 63).astype(F32)
    slot = lax.broadcasted_iota(jnp.int32, (cap, 1), 0).astype(F32)
    vrow = lax.broadcasted_iota(jnp.int32, (SUBLANES, 1), 0)

    def compact(ei, carry):
        g = aff_ref[0, pl.ds(ei, 1), :]
        g_hi = g.astype(BF16).astype(F32)
        g_mid = (g - g_hi).astype(BF16).astype(F32)
        g_lo = g - g_hi - g_mid
        vals = jnp.where(vrow == 0, tok_hi, jnp.where(vrow == 1, tok_lo, jnp.where(
            vrow == 2, g_hi, jnp.where(vrow == 3, g_mid, jnp.where(vrow == 4, g_lo, 0.0))))).astype(BF16)
        hit = (pos_ref[pl.ds(ei, 1), :] == slot) & (mask_ref[pl.ds(ei, 1), :] > 0.0)
        onehot = jnp.where(hit, 1.0, 0.0).astype(BF16)
        res = lax.dot_general(vals, onehot, (((1,), (1,)), ((), ())), preferred_element_type=F32)
        idx_ref[0, pl.ds(ei, 1), :] = (res[0:1] * 64.0 + res[1:2]).astype(jnp.int32)
        gate_ref[0, pl.ds(ei, 1), :] = res[2:3] + res[3:4] + res[4:5]
        return carry

    lax.fori_loop(0, e, compact, 0)


def _select_tokens(aff, cap):
    b, e, t = aff.shape
    assert t <= 64 * 256
    spec = pl.BlockSpec((1, e, cap), lambda i: (i, 0, 0))
    return pl.pallas_call(
        functools.partial(_topk_kernel, cap),
        grid=(b,),
        in_specs=[pl.BlockSpec((1, e, t), lambda i: (i, 0, 0))],
        out_specs=[spec, spec],
        out_shape=[jax.ShapeDtypeStruct((b, e, cap), jnp.int32), jax.ShapeDtypeStruct((b, e, cap), F32)],
        scratch_shapes=[pltpu.VMEM((e, t), F32), pltpu.VMEM((e, t), F32)],
        compiler_params=_params(("parallel",)),
        name="select_tokens",
    )(aff)


def _gather_kernel(rows_ref, h_hbm, o_ref, buf, sem):
    i = pl.program_id(0)
    r_blk = buf.shape[0]

    def issue(r, carry):
        row = rows_ref[i * r_blk + r]
        pltpu.make_async_copy(h_hbm.at[pl.ds(row, 1)], buf.at[pl.ds(r, 1)], sem).start()
        return carry

    lax.fori_loop(0, r_blk, issue, 0)
    pltpu.make_async_copy(h_hbm.at[pl.ds(0, r_blk)], buf, sem).wait()
    seg = buf.shape[2]
    for s in range(ROW_SLAB):
        o_ref[:, s * seg:(s + 1) * seg] = buf[:, s, :].astype(o_ref.dtype)


def _gather_rows(rows, h_slabs):
    n_rows = rows.shape[0]
    _, slab, seg = h_slabs.shape
    r_blk = _tile(n_rows, 256)
    grid_spec = pltpu.PrefetchScalarGridSpec(
        num_scalar_prefetch=1,
        grid=(n_rows // r_blk,),
        in_specs=[pl.BlockSpec(memory_space=pl.ANY)],
        out_specs=pl.BlockSpec((r_blk, slab * seg), lambda i, rows_ref: (i, 0)),
        scratch_shapes=[pltpu.VMEM((r_blk, slab, seg), F32), pltpu.SemaphoreType.DMA],
    )
    return pl.pallas_call(
        _gather_kernel,
        grid_spec=grid_spec,
        out_shape=jax.ShapeDtypeStruct((n_rows, slab * seg), BF16),
        compiler_params=_params(("arbitrary",)),
        name="gather_rows",
    )(rows, h_slabs)


def _expert_up_kernel(xs_ref, wg_ref, wu_ref, h_ref):
    xs = xs_ref[...]
    g = jnp.dot(xs, wg_ref[0].astype(BF16), preferred_element_type=F32)
    u = jnp.dot(xs, wu_ref[0].astype(BF16), preferred_element_type=F32)
    h_ref[...] = (g * jax.nn.sigmoid(g) * u).astype(h_ref.dtype)


def _expert_up(xs, w_gate, w_up, rows_per_expert):
    m, d = xs.shape
    f = w_gate.shape[2]
    tm = _tile(rows_per_expert, 1024)
    per_e = rows_per_expert // tm
    tf = _tile(f, 256)
    wspec = pl.BlockSpec((1, d, tf), lambda i, j: (i // per_e, 0, j))
    return pl.pallas_call(
        _expert_up_kernel,
        grid=(m // tm, f // tf),
        in_specs=[pl.BlockSpec((tm, d), lambda i, j: (i, 0)), wspec, wspec],
        out_specs=pl.BlockSpec((tm, tf), lambda i, j: (i, j)),
        out_shape=jax.ShapeDtypeStruct((m, f), BF16),
        compiler_params=_params(("parallel", "arbitrary")),
        name="expert_up",
    )(xs, w_gate, w_up)


def _expert_down_kernel(h_ref, wd_ref, g_ref, y_ref):
    y = jnp.dot(h_ref[...], wd_ref[0].astype(BF16), preferred_element_type=F32)
    y_ref[...] = (y * g_ref[...]).astype(y_ref.dtype)


def _expert_down(h, w_down, gates, rows_per_expert):
    m, f = h.shape
    d = w_down.shape[2]
    tm = _tile(rows_per_expert, 1024)
    per_e = rows_per_expert // tm
    td = _tile(d, 512)
    return pl.pallas_call(
        _expert_down_kernel,
        grid=(m // tm, d // td),
        in_specs=[pl.BlockSpec((tm, f), lambda i, j: (i, 0)),
                  pl.BlockSpec((1, f, td), lambda i, j: (i // per_e, 0, j)),
                  pl.BlockSpec((tm, 1), lambda i, j: (i, 0))],
        out_specs=pl.BlockSpec((tm, td), lambda i, j: (i, j)),
        out_shape=jax.ShapeDtypeStruct((m, d), BF16),
        compiler_params=_params(("parallel", "arbitrary")),
        name="expert_down",
    )(h, w_down, gates)


def _combine_kernel(idx_ref, y_ref, x_ref, gt_ref, g_ref, o_ref, acc_ref):
    i = pl.program_id(1)
    e = pl.program_id(2)
    tt = acc_ref.shape[0]

    @pl.when(e == 0)
    def _():
        acc_ref[...] = jnp.zeros_like(acc_ref)

    tok = i * tt + lax.broadcasted_iota(jnp.int32, (tt, 1), 0)
    onehot = jnp.where(tok == idx_ref[0, 0], 1.0, 0.0).astype(BF16)
    acc_ref[...] += jnp.dot(onehot, y_ref[0], preferred_element_type=F32)

    @pl.when(e == pl.num_programs(2) - 1)
    def _():
        x = x_ref[0] + gt_ref[0] * acc_ref[...]
        o_ref[0] = x * lax.rsqrt(jnp.mean(x * x, axis=-1, keepdims=True) + EPS) * g_ref[...]


def _combine(idx, y, x1, gate, g_final):
    b, e, cap = idx.shape
    _, t, d = x1.shape
    tt = _tile(t, 256)
    return pl.pallas_call(
        _combine_kernel,
        grid=(b, t // tt, e),
        in_specs=[pl.BlockSpec((1, 1, 1, cap), lambda bi, i, ei: (bi, ei, 0, 0)),
                  pl.BlockSpec((1, cap, d), lambda bi, i, ei: (ei, bi, 0)),
                  pl.BlockSpec((1, tt, d), lambda bi, i, ei: (bi, i, 0)),
                  pl.BlockSpec((1, 1, d), lambda bi, i, ei: (bi, 0, 0)),
                  pl.BlockSpec((1, d), lambda bi, i, ei: (0, 0))],
        out_specs=pl.BlockSpec((1, tt, d), lambda bi, i, ei: (bi, i, 0)),
        out_shape=jax.ShapeDtypeStruct((b, t, d), F32),
        scratch_shapes=[pltpu.VMEM((tt, d), F32)],
        compiler_params=_params(("parallel", "parallel", "arbitrary")),
        name="combine",
    )(idx.reshape(b, e, 1, cap), y.reshape(e, b * cap, d), x1, gate, g_final.reshape(1, d))


def kernel(x, c, ctx, c_ctx, w_mod, b_mod, g_mix, g_ffn, w_in, b_in, hy_conv_w, hy_conv_b, hy_f_w1, hy_f_b1, hy_f_w2, hy_f_b2, hy_f_w3, hy_f_b3, hy_f_wout, hy_f_freq, hy_bias, lru_conv_w, lru_conv_b, lru_wa, lru_ba, lru_wx, lru_bx, lru_lambda, w_out, b_out, w_router, w_exp_gate, w_exp_up, w_exp_down, g_final):
    bsz, n_lat, d = x.shape
    n_ctx = ctx.shape[1]
    depth = w_mod.shape[0]
    c_hy = hy_bias.shape[1]
    c_lru = lru_conv_b.shape[1]
    in_gate = 3 * c_hy
    in_x = in_gate + c_lru
    n_exp = w_router.shape[2]
    cap = CAPACITY_FACTOR * n_lat // n_exp
    assert depth == 1, "context residual updates are only needed for depth > 1"
    assert n_lat % GRID_W == 0 and n_ctx & (n_ctx - 1) == 0

    rows = -(-(bsz + 1) // SUBLANES) * SUBLANES
    c_all = jnp.zeros((rows, d), F32).at[:bsz].set(c).at[bsz].set(c_ctx)
    fre, fim, fimt = _dft_matrices(n_lat)

    l = 0
    mod = _modulation(c_all, w_mod[l], b_mod[l])
    mx = mod[:bsz].reshape(bsz, 1, N_MOD, d)
    sh1, sc1, gt1, sh2, sc2, gt2 = (mx[:, :, k] for k in range(N_MOD))
    mc = mod[bsz].reshape(1, 1, N_MOD, d)
    csh1, csc1 = mc[:, :, 0], mc[:, :, 1]

    w_in_b = w_in[l].astype(BF16)
    w_out_b = w_out[l].astype(BF16)

    hx = _norm_mod(x, g_mix[l], sh1, sc1, True)
    hc = _norm_mod(ctx, g_mix[l], csh1, csc1, False)
    px = _projection(hx.reshape(bsz * n_lat, d), w_in_b, b_in[l], 0, w_in_b.shape[1], F32)
    px = px.reshape(bsz, n_lat, -1)
    pc_lx = _projection(hc.reshape(bsz * n_ctx, d), w_in_b, b_in[l], in_x, c_lru, F32)
    pc_lx = pc_lx.reshape(bsz, n_ctx, c_lru)

    lru_args = (lru_conv_w[l], lru_conv_b[l], lru_wa[l], lru_ba[l], lru_wx[l], lru_bx[l], lru_lambda[l])
    (h_ctx,) = _rglru(pc_lx, 0, None, 0, *lru_args, jnp.zeros((bsz, 2, c_lru), F32), n_ctx)
    y_lru, _ = _rglru(px, in_x, px, in_gate, *lru_args, h_ctx, GRID_W)

    taps, norm = _hyena_filter(n_lat, hy_f_w1[l], hy_f_b1[l], hy_f_w2[l], hy_f_b2[l], hy_f_w3[l],
                               hy_f_b3[l], hy_f_wout[l], hy_f_freq[l])
    kre, kim = _filter_spectrum(fre, fim, taps, norm, c_hy)
    x0, wf, wb = _hyena_pre(px, hy_conv_w[l], hy_conv_b[l], c_hy)
    yre, yim = _conv_forward(fre, fim, wb, kre, kim)
    y_hy = _conv_inverse(fre, fimt, yre, yim, wf, x0, hy_bias[l])

    x1 = _out_projection(y_hy.reshape(bsz * n_lat, c_hy), y_lru.reshape(bsz * n_lat, c_lru), w_out_b,
                         b_out[l], x.reshape(bsz * n_lat, d), gt1, n_lat)
    x1 = x1.reshape(bsz, n_lat, d)

    h_slabs, aff = _router(x1, g_ffn[l], sh2, sc2, w_router[l])
    idx, gates = _select_tokens(aff, cap)
    rows_g = (idx + (jnp.arange(bsz, dtype=jnp.int32) * n_lat)[:, None, None])
    rows_g = jnp.swapaxes(rows_g, 0, 1).reshape(-1)
    xs = _gather_rows(rows_g, h_slabs)
    hmid = _expert_up(xs, w_exp_gate[l], w_exp_up[l], bsz * cap)
    y = _expert_down(hmid, w_exp_down[l], jnp.swapaxes(gates, 0, 1).reshape(-1, 1), bsz * cap)
    return _combine(idx, y, x1, gt2, g_final)
```

```python
import functools
import math

import jax
import jax.numpy as jnp
from jax import lax
from jax.experimental import pallas as pl
from jax.experimental.pallas import tpu as pltpu

F32 = jnp.float32
BF16 = jnp.bfloat16

GRID_W = 64
HY_SHORT_LEFT = 1
FILT_BANDS = 16
DECAY_TARGET = 1e-2
MIN_DECAY = math.log(DECAY_TARGET) / 1.5
MAX_DECAY = math.log(DECAY_TARGET) / 0.3
LRU_HEADS = 16
LRU_CONV_LEFT = 2
LRU_C = 8.0
CAPACITY_FACTOR = 2
N_MOD = 6
EPS = 1e-6

LANES = 128
SUBLANES = 8
DFT_ROWS = 64
VMEM_LIMIT = 56 * 2 ** 20


def _params(sem):
    return pltpu.CompilerParams(dimension_semantics=sem, vmem_limit_bytes=VMEM_LIMIT)


def _tile(n, pref):
    t = min(n, pref)
    while n % t:
        t //= 2
    return t


def _split_bf16(v):
    hi = v.astype(BF16)
    lo = (v - hi.astype(F32)).astype(BF16)
    return hi, lo


def _mod_kernel(c_ref, w_ref, b_ref, o_ref):
    c = c_ref[...]
    s = c * jax.nn.sigmoid(c)
    s_hi, s_lo = _split_bf16(s)
    w_hi, w_lo = _split_bf16(w_ref[...])
    rows = c.shape[0]
    r = jnp.dot(jnp.concatenate([s_hi, s_lo], axis=0), w_hi, preferred_element_type=F32)
    acc = r[:rows] + r[rows:] + jnp.dot(s_hi, w_lo, preferred_element_type=F32)
    o_ref[...] = acc + b_ref[...]


def _modulation(c_all, w_mod, b_mod):
    rows, d = c_all.shape
    n = w_mod.shape[1]
    tn = _tile(n, 256)
    return pl.pallas_call(
        _mod_kernel,
        grid=(n // tn,),
        in_specs=[pl.BlockSpec((rows, d), lambda j: (0, 0)),
                  pl.BlockSpec((d, tn), lambda j: (0, j)),
                  pl.BlockSpec((1, tn), lambda j: (0, j))],
        out_specs=pl.BlockSpec((rows, tn), lambda j: (0, j)),
        out_shape=jax.ShapeDtypeStruct((rows, n), F32),
        compiler_params=_params(("parallel",)),
        name="modulation",
    )(c_all, w_mod, b_mod.reshape(1, n))


def _rms_mod(x, g, shift, scale):
    y = x * lax.rsqrt(jnp.mean(x * x, axis=-1, keepdims=True) + EPS) * g
    return y * (1.0 + scale) + shift


def _norm_kernel(x_ref, g_ref, sh_ref, sc_ref, o_ref):
    o_ref[0] = _rms_mod(x_ref[0], g_ref[...], sh_ref[0], sc_ref[0]).astype(o_ref.dtype)


def _norm_mod(x, g, shift, scale, per_batch):
    b, t, d = x.shape
    tt = _tile(t, 512)
    mod_map = (lambda i, j: (i, 0, 0)) if per_batch else (lambda i, j: (0, 0, 0))
    return pl.pallas_call(
        _norm_kernel,
        grid=(b, t // tt),
        in_specs=[pl.BlockSpec((1, tt, d), lambda i, j: (i, j, 0)),
                  pl.BlockSpec((1, d), lambda i, j: (0, 0)),
                  pl.BlockSpec((1, 1, d), mod_map),
                  pl.BlockSpec((1, 1, d), mod_map)],
        out_specs=pl.BlockSpec((1, tt, d), lambda i, j: (i, j, 0)),
        out_shape=jax.ShapeDtypeStruct((b, t, d), BF16),
        compiler_params=_params(("parallel", "parallel")),
        name="adaln_norm",
    )(x, g.reshape(1, d), shift, scale)


def _proj_kernel(a_ref, w_ref, b_ref, o_ref):
    acc = jnp.dot(a_ref[...], w_ref[...], preferred_element_type=F32)
    o_ref[...] = (acc + b_ref[...]).astype(o_ref.dtype)


def _projection(a, w, bias, col_start, n_cols, out_dtype):
    m, k = a.shape
    tm = _tile(m, 1024)
    tn = _tile(n_cols, 512)
    off = col_start // tn
    return pl.pallas_call(
        _proj_kernel,
        grid=(m // tm, n_cols // tn),
        in_specs=[pl.BlockSpec((tm, k), lambda i, j: (i, 0)),
                  pl.BlockSpec((k, tn), lambda i, j: (0, j + off)),
                  pl.BlockSpec((1, tn), lambda i, j: (0, j + off))],
        out_specs=pl.BlockSpec((tm, tn), lambda i, j: (i, j)),
        out_shape=jax.ShapeDtypeStruct((m, n_cols), out_dtype),
        compiler_params=_params(("parallel", "arbitrary")),
        name="projection",
    )(a, w, bias.reshape(1, -1))


def _short_conv(x, w_ref, bias, left, group):
    rows = x.shape[0]
    pos = lax.broadcasted_iota(jnp.int32, (rows, 1), 0) & (group - 1)
    y = bias + w_ref[left:left + 1, :] * x
    for k in range(w_ref.shape[0]):
        off = k - left
        if off == 0:
            continue
        shifted = pltpu.roll(x, (-off) % rows, axis=0)
        valid = (pos + off >= 0) & (pos + off < group)
        y = y + w_ref[k:k + 1, :] * jnp.where(valid, shifted, 0.0)
    return y


def _hy_pre_kernel(p0_ref, p1_ref, p2_ref, w0_ref, w1_ref, w2_ref, b0_ref, b1_ref, b2_ref,
                   x0_ref, wf_ref, wb_ref):
    x0 = _short_conv(p0_ref[0], w0_ref, b0_ref[...], HY_SHORT_LEFT, GRID_W)
    x1 = _short_conv(p1_ref[0], w1_ref, b1_ref[...], HY_SHORT_LEFT, GRID_W)
    v = _short_conv(p2_ref[0], w2_ref, b2_ref[...], HY_SHORT_LEFT, GRID_W)
    w = x1 * v
    x0_ref[0] = x0
    wf_ref[0] = w
    wb_ref[0] = w.astype(BF16)


def _hyena_pre(px, conv_w, conv_b, c):
    b, t, _ = px.shape
    tt = _tile(t, 512)
    tc = _tile(c, 512)
    nc = c // tc
    kw = conv_w.shape[0]

    def pspec(g):
        return pl.BlockSpec((1, tt, tc), lambda i, j, k: (i, j, k + g * nc))

    def wspec(g):
        return pl.BlockSpec((kw, tc), lambda i, j, k: (0, k + g * nc))

    def bspec(g):
        return pl.BlockSpec((1, tc), lambda i, j, k: (0, k + g * nc))

    ospec = pl.BlockSpec((1, tt, tc), lambda i, j, k: (i, j, k))
    cb = conv_b.reshape(1, -1)
    return pl.pallas_call(
        _hy_pre_kernel,
        grid=(b, t // tt, nc),
        in_specs=[pspec(0), pspec(1), pspec(2), wspec(0), wspec(1), wspec(2),
                  bspec(0), bspec(1), bspec(2)],
        out_specs=[ospec, ospec, ospec],
        out_shape=[jax.ShapeDtypeStruct((b, t, c), F32),
                   jax.ShapeDtypeStruct((b, t, c), F32),
                   jax.ShapeDtypeStruct((b, t, c), BF16)],
        compiler_params=_params(("parallel", "parallel", "parallel")),
        name="hyena_pre",
    )(px, px, px, conv_w, conv_w, conv_w, cb, cb, cb)


def _hp_dot(a, b):
    return jnp.dot(a, b, preferred_element_type=F32, precision=lax.Precision.HIGHEST)


def _filter_kernel(n, w1t_ref, w1c_ref, w1s_ref, b1_ref, w2_ref, b2_ref, w3_ref, b3_ref,
                   wout_ref, freq_ref, delta_ref, h_ref, norm_ref):
    i = pl.program_id(0)
    tn = h_ref.shape[0]
    pos_i = i * tn + lax.broadcasted_iota(jnp.int32, (tn, 1), 0)
    pos = pos_i.astype(F32)
    t = pos * (1.0 / (n - 1))
    band_step = (FILT_BANDS - 1 - 1e-4) / (FILT_BANDS - 1)
    bands = 1e-4 + band_step * lax.broadcasted_iota(jnp.int32, (1, FILT_BANDS), 1).astype(F32)
    ang = (2.0 * math.pi * pos / n) * bands
    fr = freq_ref[...]
    pre = t * w1t_ref[...] + _hp_dot(jnp.cos(ang), w1c_ref[...]) - _hp_dot(jnp.sin(ang), w1s_ref[...])
    h = jnp.sin(fr * (pre + b1_ref[...]))
    h = jnp.sin(fr * (_hp_dot(h, w2_ref[...]) + b2_ref[...]))
    h = jnp.sin(fr * (_hp_dot(h, w3_ref[...]) + b3_ref[...]))
    taps = _hp_dot(h, wout_ref[...]) * jnp.exp(-t * delta_ref[...])

    @pl.when(i == 0)
    def _():
        norm_ref[...] = jnp.zeros_like(norm_ref)

    norm_ref[...] += jnp.sum(jnp.abs(taps), axis=0, keepdims=True)
    c = taps.shape[1] // 2
    col = lax.broadcasted_iota(jnp.int32, (1, taps.shape[1]), 1)
    drop = (pos_i == 0) & (col >= c)
    h_ref[...] = jnp.where(drop, 0.0, taps).astype(h_ref.dtype)


def _hyena_filter(n, w1, b1, w2, b2, w3, b3, wout, freq):
    hid = w1.shape[1]
    c2 = wout.shape[1]
    c = c2 // 2
    tn = _tile(n, 512)
    deltas = jnp.abs(jnp.linspace(MIN_DECAY, MAX_DECAY, c, dtype=F32))
    deltas = jnp.concatenate([deltas, deltas]).reshape(1, c2)
    full = lambda shape: pl.BlockSpec(shape, lambda i: (0, 0))
    return pl.pallas_call(
        functools.partial(_filter_kernel, n),
        grid=(n // tn,),
        in_specs=[full((1, hid)), full((FILT_BANDS, hid)), full((FILT_BANDS, hid)), full((1, hid)),
                  full((hid, hid)), full((1, hid)), full((hid, hid)), full((1, hid)),
                  full((hid, c2)), full((1, hid)), full((1, c2))],
        out_specs=[pl.BlockSpec((tn, c2), lambda i: (i, 0)), full((1, c2))],
        out_shape=[jax.ShapeDtypeStruct((n, c2), BF16), jax.ShapeDtypeStruct((1, c2), F32)],
        compiler_params=_params(("arbitrary",)),
        name="hyena_filter",
    )(w1[0:1], w1[1:1 + FILT_BANDS], w1[1 + FILT_BANDS:], b1.reshape(1, hid), w2, b2.reshape(1, hid),
      w3, b3.reshape(1, hid), wout, freq.reshape(1, hid), deltas)


def _dft_kernel(n, fre_ref, fim_ref, fimt_ref, tre_ref, tim_ref):
    k = pl.program_id(0)
    n2 = 2 * n
    theta = 2.0 * math.pi / n2
    s = lax.broadcasted_iota(jnp.int32, (1, n), 1)
    fl = lax.broadcasted_iota(jnp.int32, (DFT_ROWS, 1), 0)

    @pl.when(k == 0)
    def _():
        ang = ((fl * s) & (n2 - 1)).astype(F32) * theta
        tre_ref[...] = jnp.cos(ang)
        tim_ref[...] = -jnp.sin(ang)

    ang = ((DFT_ROWS * k * s) & (n2 - 1)).astype(F32) * theta
    rre = jnp.cos(ang)
    rim = -jnp.sin(ang)
    tre = tre_ref[...]
    tim = tim_ref[...]
    fre = rre * tre - rim * tim
    fim = rre * tim + rim * tre
    row = DFT_ROWS * k + fl
    fre_ref[...] = fre.astype(fre_ref.dtype)
    fim_ref[...] = jnp.where(row == 0, (1 - 2 * (s & 1)).astype(F32), fim).astype(fim_ref.dtype)
    fimt_ref[...] = jnp.where(s == 0, (1 - 2 * (row & 1)).astype(F32), fim).astype(fimt_ref.dtype)


def _dft_matrices(n):
    assert n & (n - 1) == 0 and n % DFT_ROWS == 0
    spec = pl.BlockSpec((DFT_ROWS, n), lambda k: (k, 0))
    shape = jax.ShapeDtypeStruct((n, n), BF16)
    return pl.pallas_call(
        functools.partial(_dft_kernel, n),
        grid=(n // DFT_ROWS,),
        out_specs=[spec, spec, spec],
        out_shape=[shape, shape, shape],
        scratch_shapes=[pltpu.VMEM((DFT_ROWS, n), F32), pltpu.VMEM((DFT_ROWS, n), F32)],
        compiler_params=_params(("arbitrary",)),
        name="dft_matrices",
    )()


def _filter_dft_kernel(n, fre_ref, fim_ref, hf_ref, hb_ref, nf_ref, nb_ref, kre_ref, kim_ref):
    i = pl.program_id(1)
    tf = fre_ref.shape[0]
    fre = fre_ref[...]
    fim = fim_ref[...]
    hf = hf_ref[...]
    hb = hb_ref[...]
    row0 = (i * tf + lax.broadcasted_iota(jnp.int32, (tf, 1), 0)) == 0
    scale = jnp.where(row0, 0.5 / n, 1.0 / n) / (nf_ref[...] + nb_ref[...])
    kre = jnp.dot(fre, hf, preferred_element_type=F32) + jnp.dot(fre, hb, preferred_element_type=F32)
    im_f = jnp.dot(fim, hf, preferred_element_type=F32)
    im_b = jnp.dot(fim, hb, preferred_element_type=F32)
    kre_ref[...] = kre * scale
    kim_ref[...] = jnp.where(row0, im_f + im_b, im_f - im_b) * scale


def _filter_spectrum(fre, fim, taps, norm, c):
    n = fre.shape[0]
    tf = _tile(n, 512)
    tc = _tile(c, 512)
    nc = c // tc
    fspec = pl.BlockSpec((tf, n), lambda j, i: (i, 0))
    ospec = pl.BlockSpec((tf, tc), lambda j, i: (i, j))
    shape = jax.ShapeDtypeStruct((n, c), F32)
    return pl.pallas_call(
        functools.partial(_filter_dft_kernel, n),
        grid=(nc, n // tf),
        in_specs=[fspec, fspec,
                  pl.BlockSpec((n, tc), lambda j, i: (0, j)),
                  pl.BlockSpec((n, tc), lambda j, i: (0, j + nc)),
                  pl.BlockSpec((1, tc), lambda j, i: (0, j)),
                  pl.BlockSpec((1, tc), lambda j, i: (0, j + nc))],
        out_specs=[ospec, ospec],
        out_shape=[shape, shape],
        compiler_params=_params(("parallel", "arbitrary")),
        name="filter_spectrum",
    )(fre, fim, taps, taps, norm, norm)


def _conv_fwd_kernel(fre_ref, fim_ref, w_ref, kre_ref, kim_ref, yre_ref, yim_ref):
    i = pl.program_id(2)
    tf = fre_ref.shape[0]
    w = w_ref[0]
    ure = jnp.dot(fre_ref[...], w, preferred_element_type=F32)
    uim = jnp.dot(fim_ref[...], w, preferred_element_type=F32)
    kre = kre_ref[...]
    kim = kim_ref[...]
    row0 = (i * tf + lax.broadcasted_iota(jnp.int32, (tf, 1), 0)) == 0
    yre_ref[0] = jnp.where(row0, ure * kre, ure * kre - uim * kim).astype(yre_ref.dtype)
    yim_ref[0] = jnp.where(row0, uim * kim, ure * kim + uim * kre).astype(yim_ref.dtype)


def _conv_forward(fre, fim, wb, kre, kim):
    b, n, c = wb.shape
    tf = _tile(n, 512)
    tc = _tile(c, 512)
    fspec = pl.BlockSpec((tf, n), lambda bi, j, i: (i, 0))
    kspec = pl.BlockSpec((tf, tc), lambda bi, j, i: (i, j))
    ospec = pl.BlockSpec((1, tf, tc), lambda bi, j, i: (bi, i, j))
    shape = jax.ShapeDtypeStruct((b, n, c), BF16)
    return pl.pallas_call(
        _conv_fwd_kernel,
        grid=(b, c // tc, n // tf),
        in_specs=[fspec, fspec, pl.BlockSpec((1, n, tc), lambda bi, j, i: (bi, 0, j)), kspec, kspec],
        out_specs=[ospec, ospec],
        out_shape=[shape, shape],
        compiler_params=_params(("parallel", "parallel", "arbitrary")),
        name="conv_forward_dft",
    )(fre, fim, wb, kre, kim)


def _conv_inv_kernel(fre_ref, fimt_ref, yre_ref, yim_ref, wf_ref, x0_ref, bias_ref, o_ref):
    z = jnp.dot(fre_ref[...], yre_ref[0], preferred_element_type=F32)
    z = z + jnp.dot(fimt_ref[...], yim_ref[0], preferred_element_type=F32)
    z = z + wf_ref[0] * bias_ref[...]
    o_ref[0] = (x0_ref[0] * z).astype(o_ref.dtype)


def _conv_inverse(fre, fimt, yre, yim, wf, x0, bias):
    b, n, c = wf.shape
    tt = _tile(n, 512)
    tc = _tile(c, 512)
    fspec = pl.BlockSpec((tt, n), lambda bi, j, i: (i, 0))
    yspec = pl.BlockSpec((1, n, tc), lambda bi, j, i: (bi, 0, j))
    espec = pl.BlockSpec((1, tt, tc), lambda bi, j, i: (bi, i, j))
    return pl.pallas_call(
        _conv_inv_kernel,
        grid=(b, c // tc, n // tt),
        in_specs=[fspec, fspec, yspec, yspec, espec, espec,
                  pl.BlockSpec((1, tc), lambda bi, j, i: (0, j))],
        out_specs=espec,
        out_shape=jax.ShapeDtypeStruct((b, n, c), BF16),
        compiler_params=_params(("parallel", "parallel", "arbitrary")),
        name="conv_inverse_dft",
    )(fre, fimt, yre, yim, wf, x0, bias.reshape(1, c))


SCAN_UNROLL = 4


def _sigmoid(v):
    return 0.5 * jnp.tanh(0.5 * v) + 0.5


def _tile_scan(a, b, reverse):
    rows, c = a.shape
    a3 = a.reshape(rows // SUBLANES, SUBLANES, c)
    b3 = b.reshape(rows // SUBLANES, SUBLANES, c)
    sub = lax.broadcasted_iota(jnp.int32, (1, SUBLANES, 1), 1)
    for k in (1, 2, 4):
        shift = SUBLANES - k if reverse else k
        valid = (sub < SUBLANES - k) if reverse else (sub >= k)
        b3 = b3 + a3 * jnp.where(valid, pltpu.roll(b3, shift, axis=1), 0.0)
        a3 = a3 * jnp.where(valid, pltpu.roll(a3, shift, axis=1), 1.0)
    return a3.reshape(rows, c), b3.reshape(rows, c)


def _lru_kernel(group, has_gate, *refs):
    if has_gate:
        (x_ref, gate_ref, cw_ref, cb_ref, wa_ref, ba_ref, wx_ref, bx_ref, lam_ref, h0_ref,
         y_ref, ht_ref, af_ref, bf_ref, ab_ref, bb_ref) = refs
    else:
        (x_ref, cw_ref, cb_ref, wa_ref, ba_ref, wx_ref, bx_ref, lam_ref, h0_ref,
         ht_ref, af_ref, bf_ref, ab_ref, bb_ref) = refs
    t_len = x_ref.shape[1]
    chunk = _tile(t_len, 512)
    a_refs = (af_ref, ab_ref)
    b_refs = (bf_ref, bb_ref)

    def coeffs(ci, carry):
        r0 = pl.multiple_of(ci * chunk, chunk)
        xc = _short_conv(x_ref[0, pl.ds(r0, chunk), :], cw_ref, cb_ref[...], LRU_CONV_LEFT, group)
        xcb = xc.astype(BF16)
        for d in range(2):
            r = _sigmoid(jnp.dot(xcb, wa_ref[d, 0], preferred_element_type=F32) + ba_ref[d])
            gi = _sigmoid(jnp.dot(xcb, wx_ref[d, 0], preferred_element_type=F32) + bx_ref[d])
            lam = lam_ref[d]
            softplus_neg = jnp.maximum(-lam, 0.0) + jnp.log(1.0 + jnp.exp(-jnp.abs(lam)))
            log_a = -LRU_C * r * softplus_neg
            a = jnp.exp(log_a)
            a_tile, b_tile = _tile_scan(a, jnp.sqrt(1.0 - a * a) * (gi * xc), d == 1)
            a_refs[d][pl.ds(r0, chunk), :] = a_tile
            b_refs[d][pl.ds(r0, chunk), :] = b_tile
        return carry

    lax.fori_loop(0, t_len // chunk, coeffs, 0)

    n_tiles = t_len // SUBLANES
    unroll = _tile(n_tiles, SCAN_UNROLL)

    def scan(i, carry):
        hf, hb = carry
        for u in range(unroll):
            kf = i * unroll + u
            rf = pl.multiple_of(kf * SUBLANES, SUBLANES)
            tile_f = af_ref[pl.ds(rf, SUBLANES), :] * hf + bf_ref[pl.ds(rf, SUBLANES), :]
            bf_ref[pl.ds(rf, SUBLANES), :] = tile_f
            hf = tile_f[SUBLANES - 1:SUBLANES, :]
            rb = pl.multiple_of((n_tiles - 1 - kf) * SUBLANES, SUBLANES)
            tile_b = ab_ref[pl.ds(rb, SUBLANES), :] * hb + bb_ref[pl.ds(rb, SUBLANES), :]
            bb_ref[pl.ds(rb, SUBLANES), :] = tile_b
            hb = tile_b[0:1, :]
        return hf, hb

    hf, hb = lax.fori_loop(0, n_tiles // unroll, scan, (h0_ref[0, 0:1, :], h0_ref[0, 1:2, :]))
    ht_ref[0, 0:1, :] = hf
    ht_ref[0, 1:2, :] = hb

    if has_gate:
        def emit(ci, carry):
            r0 = pl.multiple_of(ci * chunk, chunk)
            hs = bf_ref[pl.ds(r0, chunk), :] + bb_ref[pl.ds(r0, chunk), :]
            y_ref[0, pl.ds(r0, chunk), :] = (jax.nn.gelu(gate_ref[0, pl.ds(r0, chunk), :]) * hs).astype(y_ref.dtype)
            return carry

        lax.fori_loop(0, t_len // chunk, emit, 0)


def _block_diag(w, heads_per_tile):
    d2, h, hd, _ = w.shape
    w = w.reshape(d2, h // heads_per_tile, heads_per_tile, hd, hd)
    eye = jnp.eye(heads_per_tile, dtype=w.dtype)
    bd = jnp.einsum('dghij,hq->dghiqj', w, eye)
    return bd.reshape(d2, h // heads_per_tile, heads_per_tile * hd, heads_per_tile * hd).astype(BF16)


def _rglru(xsrc, x_col, gsrc, g_col, conv_w, conv_b, wa, ba, wx, bx, lam, h0, group):
    b, t, _ = xsrc.shape
    c = conv_w.shape[1]
    hd = c // LRU_HEADS
    tc = min(c, max(hd, 256))
    hp = tc // hd
    nt = c // tc
    has_gate = gsrc is not None
    wa_bd = _block_diag(wa, hp)
    wx_bd = _block_diag(wx, hp)
    xo, go = x_col // tc, (g_col // tc if has_gate else 0)
    vec = lambda: pl.BlockSpec((2, 1, tc), lambda i, j: (0, 0, j))
    mat = lambda: pl.BlockSpec((2, 1, tc, tc), lambda i, j: (0, j, 0, 0))
    in_specs = [pl.BlockSpec((1, t, tc), lambda i, j: (i, 0, j + xo))]
    args = [xsrc]
    if has_gate:
        in_specs.append(pl.BlockSpec((1, t, tc), lambda i, j: (i, 0, j + go)))
        args.append(gsrc)
    in_specs += [pl.BlockSpec((conv_w.shape[0], tc), lambda i, j: (0, j)),
                 pl.BlockSpec((1, tc), lambda i, j: (0, j)),
                 mat(), vec(), mat(), vec(), vec(),
                 pl.BlockSpec((1, 2, tc), lambda i, j: (i, 0, j))]
    args += [conv_w, conv_b.reshape(1, c), wa_bd, ba.reshape(2, 1, c), wx_bd, bx.reshape(2, 1, c),
             lam.reshape(2, 1, c), h0]
    ht_spec = pl.BlockSpec((1, 2, tc), lambda i, j: (i, 0, j))
    ht_shape = jax.ShapeDtypeStruct((b, 2, c), F32)
    if has_gate:
        out_specs = [pl.BlockSpec((1, t, tc), lambda i, j: (i, 0, j)), ht_spec]
        out_shape = [jax.ShapeDtypeStruct((b, t, c), BF16), ht_shape]
    else:
        out_specs = [ht_spec]
        out_shape = [ht_shape]
    return pl.pallas_call(
        functools.partial(_lru_kernel, group, has_gate),
        grid=(b, nt),
        in_specs=in_specs,
        out_specs=out_specs,
        out_shape=out_shape,
        scratch_shapes=[pltpu.VMEM((t, tc), F32) for _ in range(4)],
        compiler_params=_params(("parallel", "parallel")),
        name="rglru" if has_gate else "rglru_context",
    )(*args)


def _out_proj_kernel(a1_ref, a2_ref, w1_ref, w2_ref, b_ref, x_ref, g_ref, o_ref):
    acc = jnp.dot(a1_ref[...], w1_ref[...], preferred_element_type=F32)
    acc = acc + jnp.dot(a2_ref[...], w2_ref[...], preferred_element_type=F32)
    o_ref[...] = x_ref[...] + g_ref[0] * (acc + b_ref[...])


def _out_projection(a1, a2, w, bias, x, gate, t):
    m, k1 = a1.shape
    n = w.shape[1]
    tm = _tile(t, 1024)
    tn = _tile(n, 512)
    per_b = t // tm
    k1_blocks = 1
    return pl.pallas_call(
        _out_proj_kernel,
        grid=(m // tm, n // tn),
        in_specs=[pl.BlockSpec((tm, k1), lambda i, j: (i, 0)),
                  pl.BlockSpec((tm, k1), lambda i, j: (i, 0)),
                  pl.BlockSpec((k1, tn), lambda i, j: (0, j)),
                  pl.BlockSpec((k1, tn), lambda i, j: (k1_blocks, j)),
                  pl.BlockSpec((1, tn), lambda i, j: (0, j)),
                  pl.BlockSpec((tm, tn), lambda i, j: (i, j)),
                  pl.BlockSpec((1, 1, tn), lambda i, j: (i // per_b, 0, j))],
        out_specs=pl.BlockSpec((tm, tn), lambda i, j: (i, j)),
        out_shape=jax.ShapeDtypeStruct((m, n), F32),
        compiler_params=_params(("parallel", "arbitrary")),
        name="out_projection",
    )(a1, a2, w, w, bias.reshape(1, n), x, gate)


def _router_kernel(x_ref, g_ref, sh_ref, sc_ref, wr_ref, h_ref, aff_ref):
    h = _rms_mod(x_ref[0], g_ref[...], sh_ref[0], sc_ref[0])
    h_ref[...] = h
    h_hi, h_lo = _split_bf16(h)
    w_hi, w_lo = _split_bf16(wr_ref[...])
    nt = (((1,), (1,)), ((), ()))
    logits = (lax.dot_general(w_hi, h_hi, nt, preferred_element_type=F32)
              + lax.dot_general(w_lo, h_hi, nt, preferred_element_type=F32)
              + lax.dot_general(w_hi, h_lo, nt, preferred_element_type=F32))
    z = jnp.exp(logits - jnp.max(logits, axis=0, keepdims=True))
    aff_ref[0] = z / jnp.sum(z, axis=0, keepdims=True)


def _router(x1, g, shift, scale, w_router):
    b, t, d = x1.shape
    e = w_router.shape[1]
    tt = _tile(t, 256)
    per_b = t // tt
    mod_map = lambda i, j: (i, 0, 0)
    return pl.pallas_call(
        _router_kernel,
        grid=(b, per_b),
        in_specs=[pl.BlockSpec((1, tt, d), lambda i, j: (i, j, 0)),
                  pl.BlockSpec((1, d), lambda i, j: (0, 0)),
                  pl.BlockSpec((1, 1, d), mod_map),
                  pl.BlockSpec((1, 1, d), mod_map),
                  pl.BlockSpec((e, d), lambda i, j: (0, 0))],
        out_specs=[pl.BlockSpec((tt, d), lambda i, j: (i * per_b + j, 0)),
                   pl.BlockSpec((1, e, tt), lambda i, j: (i, 0, j))],
        out_shape=[jax.ShapeDtypeStruct((b * t, d), F32),
                   jax.ShapeDtypeStruct((b, e, t), F32)],
        compiler_params=_params(("parallel", "parallel")),
        name="router",
    )(x1, g.reshape(1, d), shift, scale, w_router.T)


def _prefix_count(mask_ref, out_ref):
    e, t = mask_ref.shape
    blk = min(t, LANES)
    tri = (lax.broadcasted_iota(jnp.int32, (blk, blk), 0)
           < lax.broadcasted_iota(jnp.int32, (blk, blk), 1)).astype(BF16)
    carry = jnp.zeros((e, 1), F32)
    for k in range(t // blk):
        m = mask_ref[:, k * blk:(k + 1) * blk]
        out_ref[:, k * blk:(k + 1) * blk] = jnp.dot(m.astype(BF16), tri, preferred_element_type=F32) + carry
        carry = carry + jnp.sum(m, axis=1, keepdims=True)
    return carry


def _topk_kernel(cap, aff_ref, idx_ref, gate_ref, dest_ref, span_ref, mask_ref, pos_ref, cnt_ref, q_ref):
    a = aff_ref[0]
    e, t = a.shape
    min_normal = 0x00800000

    def refine(i, thr):
        cand = thr | jnp.left_shift(jnp.int32(1), 30 - i)
        cnt = jnp.sum(jnp.where(a >= pltpu.bitcast(cand, F32), 1.0, 0.0), axis=1, keepdims=True)
        return jnp.where((cnt >= cap) & (cand >= min_normal), cand, thr)

    thr = lax.fori_loop(0, 31, refine, jnp.zeros((e, 1), jnp.int32))
    above = a >= pltpu.bitcast(jnp.maximum(thr + 1, min_normal), F32)
    tied = (a >= pltpu.bitcast(thr, F32)) & jnp.logical_not(above)
    need = cap - jnp.sum(jnp.where(above, 1.0, 0.0), axis=1, keepdims=True)
    mask_ref[...] = jnp.where(tied, 1.0, 0.0)
    _prefix_count(mask_ref, pos_ref)
    sel = above | (tied & (pos_ref[...] < need))
    mask_ref[...] = jnp.where(sel, 1.0, 0.0)
    _prefix_count(mask_ref, pos_ref)

    mask = mask_ref[...]
    cnt = jnp.sum(mask, axis=0, keepdims=True)
    cnt_ref[...] = jnp.broadcast_to(cnt, cnt_ref.shape)
    _prefix_count(cnt_ref, q_ref.at[0:SUBLANES])
    first = q_ref[0:1, :]
    srow = lax.broadcasted_iota(jnp.int32, (SUBLANES, 1), 0)
    span_ref[0] = jnp.where(srow == 0, first, jnp.where(srow == 1, first + cnt, 0.0))
    lower = (lax.broadcasted_iota(jnp.int32, (e, e), 1)
             < lax.broadcasted_iota(jnp.int32, (e, e), 0)).astype(BF16)
    q_ref[...] = first + jnp.dot(lower, mask.astype(BF16), preferred_element_type=F32)

    tok = lax.broadcasted_iota(jnp.int32, (1, t), 1)
    tok_hi = (tok >> 6).astype(F32)
    tok_lo = (tok & 63).astype(F32)
    slot = lax.broadcasted_iota(jnp.int32, (cap, 1), 0).astype(F32)
    vrow = lax.broadcasted_iota(jnp.int32, (SUBLANES, 1), 0)
    dest_base = pl.program_id(0) * (e * cap)

    def compact(ei, carry):
        g = aff_ref[0, pl.ds(ei, 1), :]
        g_hi = g.astype(BF16).astype(F32)
        g_mid = (g - g_hi).astype(BF16).astype(F32)
        g_lo = g - g_hi - g_mid
        q = q_ref[pl.ds(ei, 1), :]
        q_hi = jnp.floor(q * (1.0 / 64.0))
        q_lo = q - 64.0 * q_hi
        vals = jnp.where(vrow == 0, tok_hi, jnp.where(vrow == 1, tok_lo, jnp.where(
            vrow == 2, g_hi, jnp.where(vrow == 3, g_mid, jnp.where(vrow == 4, g_lo, jnp.where(
                vrow == 5, q_hi, jnp.where(vrow == 6, q_lo, 0.0))))))).astype(BF16)
        hit = (pos_ref[pl.ds(ei, 1), :] == slot) & (mask_ref[pl.ds(ei, 1), :] > 0.0)
        onehot = jnp.where(hit, 1.0, 0.0).astype(BF16)
        res = lax.dot_general(vals, onehot, (((1,), (1,)), ((), ())), preferred_element_type=F32)
        idx_ref[0, pl.ds(ei, 1), :] = (res[0:1] * 64.0 + res[1:2]).astype(jnp.int32)
        gate_ref[0, pl.ds(ei, 1), :] = res[2:3] + res[3:4] + res[4:5]
        dest_ref[0, pl.ds(ei, 1), :] = (res[5:6] * 64.0 + res[6:7]).astype(jnp.int32) + dest_base
        return carry

    lax.fori_loop(0, e, compact, 0)


def _select_tokens(aff, cap):
    b, e, t = aff.shape
    assert t <= 64 * 256 and e * cap <= 64 * 256 and e >= SUBLANES
    spec = pl.BlockSpec((1, e, cap), lambda i: (i, 0, 0))
    return pl.pallas_call(
        functools.partial(_topk_kernel, cap),
        grid=(b,),
        in_specs=[pl.BlockSpec((1, e, t), lambda i: (i, 0, 0))],
        out_specs=[spec, spec, spec, pl.BlockSpec((1, SUBLANES, t), lambda i: (i, 0, 0))],
        out_shape=[jax.ShapeDtypeStruct((b, e, cap), jnp.int32), jax.ShapeDtypeStruct((b, e, cap), F32),
                   jax.ShapeDtypeStruct((b, e, cap), jnp.int32), jax.ShapeDtypeStruct((b, SUBLANES, t), F32)],
        scratch_shapes=[pltpu.VMEM((e, t), F32), pltpu.VMEM((e, t), F32),
                        pltpu.VMEM((SUBLANES, t), F32), pltpu.VMEM((e, t), F32)],
        compiler_params=_params(("parallel",)),
        name="select_tokens",
    )(aff)


def _gather_kernel(rows_ref, h_hbm, o_ref, buf, sem):
    i = pl.program_id(0)
    r_blk = buf.shape[1]

    def issue_block(blk, slot):
        def issue(r, carry):
            row = rows_ref[blk * r_blk + r]
            pltpu.make_async_copy(h_hbm.at[pl.ds(row, 1), :], buf.at[slot, pl.ds(r, 1), :], sem.at[slot]).start()
            return carry

        lax.fori_loop(0, r_blk, issue, 0, unroll=8)

    @pl.when(i == 0)
    def _():
        issue_block(0, 0)

    @pl.when(i + 1 < pl.num_programs(0))
    def _():
        issue_block(i + 1, (i + 1) & 1)

    slot = i & 1
    pltpu.make_async_copy(h_hbm.at[pl.ds(0, r_blk), :], buf.at[slot], sem.at[slot]).wait()
    o_ref[...] = buf[slot].astype(o_ref.dtype)


def _gather_rows(rows, h):
    n_rows = rows.shape[0]
    d = h.shape[1]
    r_blk = _tile(n_rows, 256)
    grid_spec = pltpu.PrefetchScalarGridSpec(
        num_scalar_prefetch=1,
        grid=(n_rows // r_blk,),
        in_specs=[pl.BlockSpec(memory_space=pl.ANY)],
        out_specs=pl.BlockSpec((r_blk, d), lambda i, rows_ref: (i, 0)),
        scratch_shapes=[pltpu.VMEM((2, r_blk, d), F32), pltpu.SemaphoreType.DMA((2,))],
    )
    return pl.pallas_call(
        _gather_kernel,
        grid_spec=grid_spec,
        out_shape=jax.ShapeDtypeStruct((n_rows, d), BF16),
        compiler_params=_params(("arbitrary",)),
        name="gather_rows",
    )(rows, h)


def _expert_up_kernel(xs_ref, wg_ref, wu_ref, h_ref):
    xs = xs_ref[...]
    g = jnp.dot(xs, wg_ref[0].astype(BF16), preferred_element_type=F32)
    u = jnp.dot(xs, wu_ref[0].astype(BF16), preferred_element_type=F32)
    h_ref[...] = (g * jax.nn.sigmoid(g) * u).astype(h_ref.dtype)


def _expert_up(xs, w_gate, w_up, rows_per_expert):
    m, d = xs.shape
    f = w_gate.shape[2]
    tm = _tile(rows_per_expert, 1024)
    per_e = rows_per_expert // tm
    tf = _tile(f, 256)
    wspec = pl.BlockSpec((1, d, tf), lambda i, j: (i // per_e, 0, j))
    return pl.pallas_call(
        _expert_up_kernel,
        grid=(m // tm, f // tf),
        in_specs=[pl.BlockSpec((tm, d), lambda i, j: (i, 0)), wspec, wspec],
        out_specs=pl.BlockSpec((tm, tf), lambda i, j: (i, j)),
        out_shape=jax.ShapeDtypeStruct((m, f), BF16),
        compiler_params=_params(("parallel", "arbitrary")),
        name="expert_up",
    )(xs, w_gate, w_up)


def _expert_down_kernel(dest_ref, h_ref, wd_ref, g_ref, ys_hbm, ybuf, sem):
    i = pl.program_id(0)
    j = pl.program_id(1)
    n_i = pl.num_programs(0)
    _, tm, d = ybuf.shape
    td = wd_ref.shape[2]
    slot = i & 1
    y = jnp.dot(h_ref[...], wd_ref[0].astype(BF16), preferred_element_type=F32) * g_ref[...]

    def wait_scatter(s):
        pltpu.make_async_copy(ybuf.at[s], ys_hbm.at[pl.ds(0, tm), :], sem.at[s]).wait()

    @pl.when((j == 0) & (i >= 2))
    def _():
        wait_scatter(slot)

    for jj in range(d // td):
        @pl.when(j == jj)
        def _(jj=jj):
            ybuf[slot, :, jj * td:(jj + 1) * td] = y

    @pl.when(j == pl.num_programs(1) - 1)
    def _():
        def issue(r, carry):
            row = dest_ref[i * tm + r]
            pltpu.make_async_copy(ybuf.at[slot, pl.ds(r, 1), :], ys_hbm.at[pl.ds(row, 1), :], sem.at[slot]).start()
            return carry

        lax.fori_loop(0, tm, issue, 0, unroll=8)

        @pl.when(i == n_i - 1)
        def _():
            @pl.when(i >= 1)
            def _():
                wait_scatter(1 - slot)
            wait_scatter(slot)


def _expert_down(dest, h, w_down, gates, rows_per_expert):
    m, f = h.shape
    d = w_down.shape[2]
    tm = _tile(rows_per_expert, 1024)
    per_e = rows_per_expert // tm
    td = _tile(d, 512)
    grid_spec = pltpu.PrefetchScalarGridSpec(
        num_scalar_prefetch=1,
        grid=(m // tm, d // td),
        in_specs=[pl.BlockSpec((tm, f), lambda i, j, dest_ref: (i, 0)),
                  pl.BlockSpec((1, f, td), lambda i, j, dest_ref: (i // per_e, 0, j)),
                  pl.BlockSpec((tm, 1), lambda i, j, dest_ref: (i, 0))],
        out_specs=pl.BlockSpec(memory_space=pl.ANY),
        scratch_shapes=[pltpu.VMEM((2, tm, d), F32), pltpu.SemaphoreType.DMA((2,))],
    )
    return pl.pallas_call(
        _expert_down_kernel,
        grid_spec=grid_spec,
        out_shape=jax.ShapeDtypeStruct((m, d), F32),
        compiler_params=_params(("arbitrary", "arbitrary")),
        name="expert_down",
    )(dest, h, w_down, gates)


def _combine_kernel(slots_per_batch, tb_ref, span_ref, x_ref, gt_ref, g_ref, ys_hbm, o_ref,
                    buf, sem, acc_ref):
    bi = pl.program_id(0)
    i = pl.program_id(1)
    n_t = pl.num_programs(1)
    kc = buf.shape[1]
    total = ys_hbm.shape[0]
    c_lo = tb_ref[bi * (n_t + 1) + i]
    c_hi = tb_ref[bi * (n_t + 1) + i + 1]
    k_first = (c_lo >> 3) << 3
    n_chunks = (c_hi - k_first + kc - 1) // kc

    def chunk_start(c):
        return pl.multiple_of(jnp.minimum(k_first + c * kc, total - kc), SUBLANES)

    def copy(c, slot):
        return pltpu.make_async_copy(ys_hbm.at[pl.ds(chunk_start(c), kc), :], buf.at[slot], sem.at[slot])

    @pl.when(n_chunks > 0)
    def _():
        copy(0, 0).start()

    acc_ref[...] = jnp.zeros_like(acc_ref)
    base = (bi * slots_per_batch).astype(F32)
    first = span_ref[0][:, 0:1] + base
    last = span_ref[0][:, 1:2] + base

    def body(c, carry):
        slot = c & 1
        copy(c, slot).wait()

        @pl.when(c + 1 < n_chunks)
        def _():
            copy(c + 1, 1 - slot).start()

        k = chunk_start(c) + lax.broadcasted_iota(jnp.int32, (1, kc), 1)
        k = jnp.where(k >= k_first + c * kc, k, -1).astype(F32)
        onehot = jnp.where((k >= first) & (k < last), 1.0, 0.0).astype(BF16)
        acc_ref[...] += jnp.dot(onehot, buf[slot].astype(BF16), preferred_element_type=F32)
        return carry

    lax.fori_loop(0, n_chunks, body, 0)
    x = x_ref[0] + gt_ref[0] * acc_ref[...]
    o_ref[0] = x * lax.rsqrt(jnp.mean(x * x, axis=-1, keepdims=True) + EPS) * g_ref[...]


def _combine(tile_bounds, span, ys, x1, gate, g_final, slots_per_batch):
    b, t, d = x1.shape
    tt = _tile(t, 256)
    kc = min(256, ys.shape[0])
    grid_spec = pltpu.PrefetchScalarGridSpec(
        num_scalar_prefetch=1,
        grid=(b, t // tt),
        in_specs=[pl.BlockSpec((1, tt, SUBLANES), lambda bi, i, tb: (bi, i, 0)),
                  pl.BlockSpec((1, tt, d), lambda bi, i, tb: (bi, i, 0)),
                  pl.BlockSpec((1, 1, d), lambda bi, i, tb: (bi, 0, 0)),
                  pl.BlockSpec((1, d), lambda bi, i, tb: (0, 0)),
                  pl.BlockSpec(memory_space=pl.ANY)],
        out_specs=pl.BlockSpec((1, tt, d), lambda bi, i, tb: (bi, i, 0)),
        scratch_shapes=[pltpu.VMEM((2, kc, d), F32), pltpu.SemaphoreType.DMA((2,)),
                        pltpu.VMEM((tt, d), F32)],
    )
    return pl.pallas_call(
        functools.partial(_combine_kernel, slots_per_batch),
        grid_spec=grid_spec,
        out_shape=jax.ShapeDtypeStruct((b, t, d), F32),
        compiler_params=_params(("arbitrary", "arbitrary")),
        name="combine",
    )(tile_bounds, span, x1, gate, g_final.reshape(1, d), ys)


def kernel(x, c, ctx, c_ctx, w_mod, b_mod, g_mix, g_ffn, w_in, b_in, hy_conv_w, hy_conv_b, hy_f_w1, hy_f_b1, hy_f_w2, hy_f_b2, hy_f_w3, hy_f_b3, hy_f_wout, hy_f_freq, hy_bias, lru_conv_w, lru_conv_b, lru_wa, lru_ba, lru_wx, lru_bx, lru_lambda, w_out, b_out, w_router, w_exp_gate, w_exp_up, w_exp_down, g_final):
    bsz, n_lat, d = x.shape
    n_ctx = ctx.shape[1]
    depth = w_mod.shape[0]
    c_hy = hy_bias.shape[1]
    c_lru = lru_conv_b.shape[1]
    in_gate = 3 * c_hy
    in_x = in_gate + c_lru
    n_exp = w_router.shape[2]
    cap = CAPACITY_FACTOR * n_lat // n_exp
    assert depth == 1, "context residual updates are only needed for depth > 1"
    assert n_lat % GRID_W == 0 and n_ctx & (n_ctx - 1) == 0

    rows = -(-(bsz + 1) // SUBLANES) * SUBLANES
    c_all = jnp.zeros((rows, d), F32).at[:bsz].set(c).at[bsz].set(c_ctx)
    fre, fim, fimt = _dft_matrices(n_lat)

    l = 0
    mod = _modulation(c_all, w_mod[l], b_mod[l])
    mx = mod[:bsz].reshape(bsz, 1, N_MOD, d)
    sh1, sc1, gt1, sh2, sc2, gt2 = (mx[:, :, k] for k in range(N_MOD))
    mc = mod[bsz].reshape(1, 1, N_MOD, d)
    csh1, csc1 = mc[:, :, 0], mc[:, :, 1]

    w_in_b = w_in[l].astype(BF16)
    w_out_b = w_out[l].astype(BF16)

    hx = _norm_mod(x, g_mix[l], sh1, sc1, True)
    hc = _norm_mod(ctx, g_mix[l], csh1, csc1, False)
    px = _projection(hx.reshape(bsz * n_lat, d), w_in_b, b_in[l], 0, w_in_b.shape[1], F32)
    px = px.reshape(bsz, n_lat, -1)
    pc_lx = _projection(hc.reshape(bsz * n_ctx, d), w_in_b, b_in[l], in_x, c_lru, F32)
    pc_lx = pc_lx.reshape(bsz, n_ctx, c_lru)

    lru_args = (lru_conv_w[l], lru_conv_b[l], lru_wa[l], lru_ba[l], lru_wx[l], lru_bx[l], lru_lambda[l])
    (h_ctx,) = _rglru(pc_lx, 0, None, 0, *lru_args, jnp.zeros((bsz, 2, c_lru), F32), n_ctx)
    y_lru, _ = _rglru(px, in_x, px, in_gate, *lru_args, h_ctx, GRID_W)

    taps, norm = _hyena_filter(n_lat, hy_f_w1[l], hy_f_b1[l], hy_f_w2[l], hy_f_b2[l], hy_f_w3[l],
                               hy_f_b3[l], hy_f_wout[l], hy_f_freq[l])
    kre, kim = _filter_spectrum(fre, fim, taps, norm, c_hy)
    x0, wf, wb = _hyena_pre(px, hy_conv_w[l], hy_conv_b[l], c_hy)
    yre, yim = _conv_forward(fre, fim, wb, kre, kim)
    y_hy = _conv_inverse(fre, fimt, yre, yim, wf, x0, hy_bias[l])

    x1 = _out_projection(y_hy.reshape(bsz * n_lat, c_hy), y_lru.reshape(bsz * n_lat, c_lru), w_out_b,
                         b_out[l], x.reshape(bsz * n_lat, d), gt1, n_lat)
    x1 = x1.reshape(bsz, n_lat, d)

    h2, aff = _router(x1, g_ffn[l], sh2, sc2, w_router[l])
    idx, gates, dest, span = _select_tokens(aff, cap)
    rows_g = (idx + (jnp.arange(bsz, dtype=jnp.int32) * n_lat)[:, None, None])
    expert_major = lambda v: jnp.swapaxes(v, 0, 1).reshape(-1)
    xs = _gather_rows(expert_major(rows_g), h2)
    hmid = _expert_up(xs, w_exp_gate[l], w_exp_up[l], bsz * cap)
    ys = _expert_down(expert_major(dest), hmid, w_exp_down[l], expert_major(gates).reshape(-1, 1), bsz * cap)
    slots = n_exp * cap
    tt = _tile(n_lat, 256)
    base = (jnp.arange(bsz, dtype=jnp.int32) * slots)[:, None]
    bounds = jnp.concatenate([span[:, 0, ::tt].astype(jnp.int32) + base, base + slots], axis=1).reshape(-1)
    return _combine(bounds, jnp.swapaxes(span, 1, 2), ys, x1, gt2, g_final, slots)
```

```python
import functools
import math

import jax
import jax.numpy as jnp
from jax import lax
from jax.experimental import pallas as pl
from jax.experimental.pallas import tpu as pltpu

F32 = jnp.float32
BF16 = jnp.bfloat16

GRID_W = 64
HY_SHORT_LEFT = 1
FILT_BANDS = 16
DECAY_TARGET = 1e-2
MIN_DECAY = math.log(DECAY_TARGET) / 1.5
MAX_DECAY = math.log(DECAY_TARGET) / 0.3
LRU_HEADS = 16
LRU_CONV_LEFT = 2
LRU_C = 8.0
CAPACITY_FACTOR = 2
N_MOD = 6
EPS = 1e-6

LANES = 128
SUBLANES = 8
VMEM_LIMIT = 56 * 2 ** 20


def _params(sem):
    return pltpu.CompilerParams(dimension_semantics=sem, vmem_limit_bytes=VMEM_LIMIT)


def _tile(n, pref):
    t = min(n, pref)
    while n % t:
        t //= 2
    return t


def _split_bf16(v):
    hi = v.astype(BF16)
    lo = (v - hi.astype(F32)).astype(BF16)
    return hi, lo


def _mod_kernel(c_ref, w_ref, b_ref, o_ref):
    c = c_ref[...]
    s = c * jax.nn.sigmoid(c)
    s_hi, s_lo = _split_bf16(s)
    w_hi, w_lo = _split_bf16(w_ref[...])
    rows = c.shape[0]
    r = jnp.dot(jnp.concatenate([s_hi, s_lo], axis=0), w_hi, preferred_element_type=F32)
    acc = r[:rows] + r[rows:] + jnp.dot(s_hi, w_lo, preferred_element_type=F32)
    o_ref[...] = acc + b_ref[...]


def _modulation(c_all, w_mod, b_mod):
    rows, d = c_all.shape
    n = w_mod.shape[1]
    tn = _tile(n, 256)
    return pl.pallas_call(
        _mod_kernel,
        grid=(n // tn,),
        in_specs=[pl.BlockSpec((rows, d), lambda j: (0, 0)),
                  pl.BlockSpec((d, tn), lambda j: (0, j)),
                  pl.BlockSpec((1, tn), lambda j: (0, j))],
        out_specs=pl.BlockSpec((rows, tn), lambda j: (0, j)),
        out_shape=jax.ShapeDtypeStruct((rows, n), F32),
        compiler_params=_params(("parallel",)),
        name="modulation",
    )(c_all, w_mod, b_mod.reshape(1, n))


def _rms_mod(x, g, shift, scale):
    y = x * lax.rsqrt(jnp.mean(x * x, axis=-1, keepdims=True) + EPS) * g
    return y * (1.0 + scale) + shift


def _norm_kernel(x_ref, g_ref, sh_ref, sc_ref, o_ref):
    o_ref[0] = _rms_mod(x_ref[0], g_ref[...], sh_ref[0], sc_ref[0]).astype(o_ref.dtype)


def _norm_mod(x, g, shift, scale, per_batch):
    b, t, d = x.shape
    tt = _tile(t, 512)
    mod_map = (lambda i, j: (i, 0, 0)) if per_batch else (lambda i, j: (0, 0, 0))
    return pl.pallas_call(
        _norm_kernel,
        grid=(b, t // tt),
        in_specs=[pl.BlockSpec((1, tt, d), lambda i, j: (i, j, 0)),
                  pl.BlockSpec((1, d), lambda i, j: (0, 0)),
                  pl.BlockSpec((1, 1, d), mod_map),
                  pl.BlockSpec((1, 1, d), mod_map)],
        out_specs=pl.BlockSpec((1, tt, d), lambda i, j: (i, j, 0)),
        out_shape=jax.ShapeDtypeStruct((b, t, d), BF16),
        compiler_params=_params(("parallel", "parallel")),
        name="adaln_norm",
    )(x, g.reshape(1, d), shift, scale)


def _proj_kernel(a_ref, w_ref, b_ref, o_ref):
    acc = jnp.dot(a_ref[...], w_ref[...], preferred_element_type=F32)
    o_ref[...] = (acc + b_ref[...]).astype(o_ref.dtype)


def _projection(a, w, bias, col_start, n_cols, out_dtype):
    m, k = a.shape
    tm = _tile(m, 1024)
    tn = _tile(n_cols, 512)
    off = col_start // tn
    return pl.pallas_call(
        _proj_kernel,
        grid=(m // tm, n_cols // tn),
        in_specs=[pl.BlockSpec((tm, k), lambda i, j: (i, 0)),
                  pl.BlockSpec((k, tn), lambda i, j: (0, j + off)),
                  pl.BlockSpec((1, tn), lambda i, j: (0, j + off))],
        out_specs=pl.BlockSpec((tm, tn), lambda i, j: (i, j)),
        out_shape=jax.ShapeDtypeStruct((m, n_cols), out_dtype),
        compiler_params=_params(("parallel", "arbitrary")),
        name="projection",
    )(a, w, bias.reshape(1, -1))


def _short_conv(x, w_ref, bias, left, group):
    rows = x.shape[0]
    pos = lax.broadcasted_iota(jnp.int32, (rows, 1), 0) & (group - 1)
    y = bias + w_ref[left:left + 1, :] * x
    for k in range(w_ref.shape[0]):
        off = k - left
        if off == 0:
            continue
        shifted = pltpu.roll(x, (-off) % rows, axis=0)
        valid = (pos + off >= 0) & (pos + off < group)
        y = y + w_ref[k:k + 1, :] * jnp.where(valid, shifted, 0.0)
    return y


def _hy_pre_kernel(p0_ref, p1_ref, p2_ref, w0_ref, w1_ref, w2_ref, b0_ref, b1_ref, b2_ref,
                   x0_ref, wf_ref):
    x0_ref[0] = _short_conv(p0_ref[0], w0_ref, b0_ref[...], HY_SHORT_LEFT, GRID_W)
    x1 = _short_conv(p1_ref[0], w1_ref, b1_ref[...], HY_SHORT_LEFT, GRID_W)
    v = _short_conv(p2_ref[0], w2_ref, b2_ref[...], HY_SHORT_LEFT, GRID_W)
    wf_ref[0] = x1 * v


def _hyena_pre(px, conv_w, conv_b, c):
    b, t, _ = px.shape
    tt = _tile(t, 512)
    tc = _tile(c, 512)
    nc = c // tc
    kw = conv_w.shape[0]

    def pspec(g):
        return pl.BlockSpec((1, tt, tc), lambda i, j, k: (i, j, k + g * nc))

    def wspec(g):
        return pl.BlockSpec((kw, tc), lambda i, j, k: (0, k + g * nc))

    def bspec(g):
        return pl.BlockSpec((1, tc), lambda i, j, k: (0, k + g * nc))

    ospec = pl.BlockSpec((1, tt, tc), lambda i, j, k: (i, j, k))
    cb = conv_b.reshape(1, -1)
    return pl.pallas_call(
        _hy_pre_kernel,
        grid=(b, t // tt, nc),
        in_specs=[pspec(0), pspec(1), pspec(2), wspec(0), wspec(1), wspec(2),
                  bspec(0), bspec(1), bspec(2)],
        out_specs=[ospec, ospec],
        out_shape=[jax.ShapeDtypeStruct((b, t, c), F32),
                   jax.ShapeDtypeStruct((b, t, c), F32)],
        compiler_params=_params(("parallel", "parallel", "parallel")),
        name="hyena_pre",
    )(px, px, px, conv_w, conv_w, conv_w, cb, cb, cb)


def _hp_dot(a, b):
    return jnp.dot(a, b, preferred_element_type=F32, precision=lax.Precision.HIGHEST)


def _filter_kernel(n, w1t_ref, w1c_ref, w1s_ref, b1_ref, w2_ref, b2_ref, w3_ref, b3_ref,
                   wout_ref, freq_ref, delta_ref, h_ref, norm_ref):
    i = pl.program_id(0)
    tn = h_ref.shape[0]
    pos_i = i * tn + lax.broadcasted_iota(jnp.int32, (tn, 1), 0)
    pos = pos_i.astype(F32)
    t = pos * (1.0 / (n - 1))
    band_step = (FILT_BANDS - 1 - 1e-4) / (FILT_BANDS - 1)
    bands = 1e-4 + band_step * lax.broadcasted_iota(jnp.int32, (1, FILT_BANDS), 1).astype(F32)
    ang = (2.0 * math.pi * pos / n) * bands
    fr = freq_ref[...]
    pre = t * w1t_ref[...] + _hp_dot(jnp.cos(ang), w1c_ref[...]) - _hp_dot(jnp.sin(ang), w1s_ref[...])
    h = jnp.sin(fr * (pre + b1_ref[...]))
    h = jnp.sin(fr * (_hp_dot(h, w2_ref[...]) + b2_ref[...]))
    h = jnp.sin(fr * (_hp_dot(h, w3_ref[...]) + b3_ref[...]))
    taps = _hp_dot(h, wout_ref[...]) * jnp.exp(-t * delta_ref[...])

    @pl.when(i == 0)
    def _():
        norm_ref[...] = jnp.zeros_like(norm_ref)

    norm_ref[...] += jnp.sum(jnp.abs(taps), axis=0, keepdims=True)
    c = taps.shape[1] // 2
    col = lax.broadcasted_iota(jnp.int32, (1, taps.shape[1]), 1)
    drop = (pos_i == 0) & (col >= c)
    h_ref[...] = jnp.where(drop, 0.0, taps).astype(h_ref.dtype)


def _hyena_filter(n, w1, b1, w2, b2, w3, b3, wout, freq):
    hid = w1.shape[1]
    c2 = wout.shape[1]
    c = c2 // 2
    tn = _tile(n, 512)
    deltas = jnp.abs(jnp.linspace(MIN_DECAY, MAX_DECAY, c, dtype=F32))
    deltas = jnp.concatenate([deltas, deltas]).reshape(1, c2)
    full = lambda shape: pl.BlockSpec(shape, lambda i: (0, 0))
    return pl.pallas_call(
        functools.partial(_filter_kernel, n),
        grid=(n // tn,),
        in_specs=[full((1, hid)), full((FILT_BANDS, hid)), full((FILT_BANDS, hid)), full((1, hid)),
                  full((hid, hid)), full((1, hid)), full((hid, hid)), full((1, hid)),
                  full((hid, c2)), full((1, hid)), full((1, c2))],
        out_specs=[pl.BlockSpec((tn, c2), lambda i: (i, 0)), full((1, c2))],
        out_shape=[jax.ShapeDtypeStruct((n, c2), F32), jax.ShapeDtypeStruct((1, c2), F32)],
        compiler_params=_params(("arbitrary",)),
        name="hyena_filter",
    )(w1[0:1], w1[1:1 + FILT_BANDS], w1[1 + FILT_BANDS:], b1.reshape(1, hid), w2, b2.reshape(1, hid),
      w3, b3.reshape(1, hid), wout, freq.reshape(1, hid), deltas)


FFT_P = LANES
FFT_R = SUBLANES


def _phase(num, den):
    return (num % den).astype(F32) * (2.0 * math.pi / den)


def _fft_matrices(n):
    p, r = FFT_P, FFT_R
    q = n // p
    n2 = 2 * n
    iq = jnp.arange(q, dtype=jnp.int32)
    eye = jnp.eye(r, dtype=F32)
    alt = lambda v: (1 - 2 * (v & 1)).astype(F32)

    ang = _phase(iq[:, None] * iq[None, :], 2 * q)
    a_re = jnp.cos(ang)
    a_im = (-jnp.sin(ang)).at[0].set(alt(iq))
    core = jnp.stack([a_re, a_im], axis=1)
    m1 = jnp.einsum('fks,rt->fkrst', core, eye).reshape(q * 2 * r, q * r)

    b_c = jnp.cos(ang.T)
    b_s = (-jnp.sin(ang.T)).at[:, 0].set(alt(iq))
    core = jnp.stack([b_c, b_s], axis=2)
    i2 = jnp.einsum('tfk,rs->trfks', core, eye).reshape(q * r, q * 2 * r)

    h = p // 2
    f2 = jnp.arange(h, dtype=jnp.int32)
    s2 = jnp.arange(p, dtype=jnp.int32)
    f_lo = iq[:, None] + 2 * q * f2[None, :]
    f_hi = jnp.where(iq[:, None] == 0, q, 2 * q - iq[:, None]) + 2 * q * f2[None, :]
    freq = jnp.stack([f_lo, f_hi], axis=1)
    phi = _phase(freq[..., None] * s2, n2)
    c, s = jnp.cos(phi), jnp.sin(phi)
    zero = jnp.zeros_like(c[:, 0])
    first = (iq == 0)[:, None, None]
    dcrow = (first & (f2 == 0)[None, :, None])
    nyq = jnp.broadcast_to(alt(s2), c[:, 0].shape)
    on_gr = jnp.stack([jnp.where(first, c[:, 0], c[:, 0]),
                       jnp.where(dcrow, nyq, -s[:, 0]),
                       jnp.where(first, zero, c[:, 1]),
                       jnp.where(first, zero, -s[:, 1])], axis=1)
    on_gi = jnp.stack([jnp.where(first, zero, s[:, 0]),
                       jnp.where(first, zero, c[:, 0]),
                       jnp.where(first, c[:, 1], -s[:, 1]),
                       jnp.where(first, -s[:, 1], -c[:, 1])], axis=1)
    m2 = jnp.concatenate([on_gr, on_gi], axis=-1).reshape(q, 4 * h, 2 * p)
    ct, st = jnp.swapaxes(c, 2, 3), jnp.swapaxes(s, 2, 3)
    zt = jnp.zeros_like(ct[:, 0])
    dccol = (first & (f2 == 0)[None, None, :])
    nyq_t = jnp.broadcast_to(alt(s2)[:, None], ct[:, 0].shape)
    hc = jnp.stack([ct[:, 0], jnp.where(dccol, nyq_t, -st[:, 0]),
                    jnp.where(first, zt, ct[:, 1]), jnp.where(first, zt, -st[:, 1])], axis=2)
    hs = jnp.stack([jnp.where(first, zt, st[:, 0]), jnp.where(first, zt, ct[:, 0]),
                    jnp.where(first, ct[:, 1], -st[:, 1]), jnp.where(first, -st[:, 1], -ct[:, 1])], axis=2)
    i1 = jnp.stack([hc, hs], axis=1).reshape(q, 2 * p, 4 * h)
    return tuple(m.astype(BF16) for m in (m1, m2, i1, i2))


def _fft_stage_a(u_ref, m1_ref, g_ref):
    q, groups, r, tc = u_ref.shape
    m1 = m1_ref[...]
    for g in range(groups):
        blk = u_ref[:, g].reshape(q * r, tc).astype(BF16)
        g_ref[g] = jnp.dot(m1, blk, preferred_element_type=F32).reshape(q, 2, r, tc)


def _fft_stage_b(g_ref, f1, m2):
    groups, _, _, r, tc = g_ref.shape
    z = jnp.concatenate([g_ref[:, f1, 0].reshape(groups * r, tc),
                         g_ref[:, f1, 1].reshape(groups * r, tc)], axis=0).astype(BF16)
    return jnp.dot(m2, z, preferred_element_type=F32)


def _cmul_packed(x, k, first):
    h = x.shape[0] // 4
    xr = (x[0:h], x[2 * h:3 * h])
    xi = (x[h:2 * h], x[3 * h:4 * h])
    kr = (k[0:h], k[2 * h:3 * h])
    ki = (k[h:2 * h], k[3 * h:4 * h])
    real_pair = first & (lax.broadcasted_iota(jnp.int32, (h, 1), 0) == 0)
    lo_re = jnp.where(real_pair, xr[0] * kr[0], xr[0] * kr[0] - xi[0] * ki[0])
    lo_im = jnp.where(real_pair, xi[0] * ki[0], xr[0] * ki[0] + xi[0] * kr[0])
    return jnp.concatenate([lo_re, lo_im, xr[1] * kr[1] - xi[1] * ki[1], xr[1] * ki[1] + xi[1] * kr[1]], axis=0)


def _filter_spec_kernel(n, hf_ref, hb_ref, nf_ref, nb_ref, m1_ref, m2_ref, k_ref, gf_ref, gb_ref):
    qi = pl.program_id(1)
    qb = m2_ref.shape[0]

    @pl.when(qi == 0)
    def _():
        _fft_stage_a(hf_ref, m1_ref, gf_ref)
        _fft_stage_a(hb_ref, m1_ref, gb_ref)

    inv_norm = 1.0 / (nf_ref[...] + nb_ref[...])
    h = m2_ref.shape[1] // 4
    row = lax.broadcasted_iota(jnp.int32, (4 * h, 1), 0)
    for k in range(qb):
        f1 = qi * qb + k
        xf = _fft_stage_b(gf_ref, f1, m2_ref[k])
        xb = _fft_stage_b(gb_ref, f1, m2_ref[k])
        real_pair = (f1 == 0) & ((row == 0) | (row == h))
        imag_row = ((row >= h) & (row < 2 * h)) | (row >= 3 * h)
        spec = jnp.where(imag_row & jnp.logical_not(real_pair), xf - xb, xf + xb)
        k_ref[k] = (spec * jnp.where(real_pair, 0.5 / n, 1.0 / n) * inv_norm).reshape(k_ref.shape[1:])


def _time_view(v, n):
    return v.reshape(v.shape[:-2] + (n // FFT_P, FFT_P // FFT_R, FFT_R, v.shape[-1]))


def _filter_spectrum(taps, norm, mats, c):
    n = taps.shape[0]
    m1, m2, _, _ = mats
    q = n // FFT_P
    h = FFT_P // 2
    tc = _tile(c, 256)
    nc = c // tc
    qb = _tile(q, 8)
    tv = _time_view(taps, n)
    blk = (q, FFT_P // FFT_R, FFT_R, tc)
    return pl.pallas_call(
        functools.partial(_filter_spec_kernel, n),
        grid=(nc, q // qb),
        in_specs=[pl.BlockSpec(blk, lambda j, i: (0, 0, 0, j)),
                  pl.BlockSpec(blk, lambda j, i: (0, 0, 0, j + nc)),
                  pl.BlockSpec((1, tc), lambda j, i: (0, j)),
                  pl.BlockSpec((1, tc), lambda j, i: (0, j + nc)),
                  pl.BlockSpec(m1.shape, lambda j, i: (0, 0)),
                  pl.BlockSpec((qb,) + m2.shape[1:], lambda j, i: (i, 0, 0))],
        out_specs=pl.BlockSpec((qb, 4, h, tc), lambda j, i: (i, 0, 0, j)),
        out_shape=jax.ShapeDtypeStruct((q, 4, h, c), F32),
        scratch_shapes=[pltpu.VMEM((FFT_P // FFT_R, q, 2, FFT_R, tc), F32) for _ in range(2)],
        compiler_params=_params(("parallel", "arbitrary")),
        name="filter_spectrum",
    )(tv, tv, norm, norm, m1, m2)


def _conv_fwd_kernel(w_ref, m1_ref, m2_ref, k_ref, y_ref, g_ref):
    qi = pl.program_id(2)
    qb = m2_ref.shape[0]

    @pl.when(qi == 0)
    def _():
        _fft_stage_a(w_ref.at[0], m1_ref, g_ref)

    for k in range(qb):
        f1 = qi * qb + k
        x = _fft_stage_b(g_ref, f1, m2_ref[k])
        y = _cmul_packed(x, k_ref[k].reshape(x.shape), f1 == 0)
        y_ref[0, k] = y.reshape(y_ref.shape[2:]).astype(y_ref.dtype)


def _conv_forward(wf, kspec, mats):
    b, n, c = wf.shape
    m1, m2, _, _ = mats
    q = n // FFT_P
    h = FFT_P // 2
    tc = _tile(c, 256)
    qb = _tile(q, 8)
    return pl.pallas_call(
        _conv_fwd_kernel,
        grid=(c // tc, b, q // qb),
        in_specs=[pl.BlockSpec((1, q, FFT_P // FFT_R, FFT_R, tc), lambda j, bi, i: (bi, 0, 0, 0, j)),
                  pl.BlockSpec(m1.shape, lambda j, bi, i: (0, 0)),
                  pl.BlockSpec((qb,) + m2.shape[1:], lambda j, bi, i: (i, 0, 0)),
                  pl.BlockSpec((qb, 4, h, tc), lambda j, bi, i: (i, 0, 0, j))],
        out_specs=pl.BlockSpec((1, qb, 4, h, tc), lambda j, bi, i: (bi, i, 0, 0, j)),
        out_shape=jax.ShapeDtypeStruct((b, q, 4, h, c), BF16),
        scratch_shapes=[pltpu.VMEM((FFT_P // FFT_R, q, 2, FFT_R, tc), F32)],
        compiler_params=_params(("parallel", "parallel", "arbitrary")),
        name="conv_forward_fft",
    )(_time_view(wf, n), m1, m2, kspec)


def _conv_inv_kernel(y_ref, i1_ref, i2_ref, wf_ref, x0_ref, bias_ref, o_ref, h_ref):
    qi = pl.program_id(2)
    qb = i1_ref.shape[0]
    groups, q, _, r, tc = h_ref.shape
    for k in range(qb):
        hv = jnp.dot(i1_ref[k], y_ref[0, k].reshape(i1_ref.shape[2], tc), preferred_element_type=F32)
        half = hv.shape[0] // 2
        h_ref[:, qi * qb + k, 0] = hv[:half].reshape(groups, r, tc)
        h_ref[:, qi * qb + k, 1] = hv[half:].reshape(groups, r, tc)

    @pl.when(qi == pl.num_programs(2) - 1)
    def _():
        i2 = i2_ref[...]
        bias = bias_ref[...]
        for g2 in range(groups // 2):
            parts = []
            for g in (2 * g2, 2 * g2 + 1):
                z = jnp.dot(i2, h_ref[g].reshape(q * 2 * r, tc).astype(BF16), preferred_element_type=F32)
                z = z.reshape(q, r, tc)
                parts.append(x0_ref[0, :, g] * (z + wf_ref[0, :, g] * bias))
            o_ref[0, :, g2] = jnp.concatenate(parts, axis=1).astype(o_ref.dtype)


def _conv_inverse(yspec, wf, x0, bias, mats):
    b, n, c = wf.shape
    _, _, i1, i2 = mats
    q = n // FFT_P
    h = FFT_P // 2
    groups = FFT_P // FFT_R
    tc = _tile(c, 256)
    qb = _tile(q, 8)
    tspec = pl.BlockSpec((1, q, groups, FFT_R, tc), lambda j, bi, i: (bi, 0, 0, 0, j))
    out = pl.pallas_call(
        _conv_inv_kernel,
        grid=(c // tc, b, q // qb),
        in_specs=[pl.BlockSpec((1, qb, 4, h, tc), lambda j, bi, i: (bi, i, 0, 0, j)),
                  pl.BlockSpec((qb,) + i1.shape[1:], lambda j, bi, i: (i, 0, 0)),
                  pl.BlockSpec(i2.shape, lambda j, bi, i: (0, 0)),
                  tspec, tspec,
                  pl.BlockSpec((1, tc), lambda j, bi, i: (0, j))],
        out_specs=pl.BlockSpec((1, q, groups // 2, 2 * FFT_R, tc), lambda j, bi, i: (bi, 0, 0, 0, j)),
        out_shape=jax.ShapeDtypeStruct((b, q, groups // 2, 2 * FFT_R, c), BF16),
        scratch_shapes=[pltpu.VMEM((groups, q, 2, FFT_R, tc), F32)],
        compiler_params=_params(("parallel", "parallel", "arbitrary")),
        name="conv_inverse_fft",
    )(yspec, i1, i2, _time_view(wf, n), _time_view(x0, n), bias.reshape(1, c))
    return out.reshape(b, n, c)


SCAN_UNROLL = 4


def _sigmoid(v):
    return 0.5 * jnp.tanh(0.5 * v) + 0.5


def _tile_scan(a, b, reverse):
    rows, c = a.shape
    a3 = a.reshape(rows // SUBLANES, SUBLANES, c)
    b3 = b.reshape(rows // SUBLANES, SUBLANES, c)
    sub = lax.broadcasted_iota(jnp.int32, (1, SUBLANES, 1), 1)
    for k in (1, 2, 4):
        shift = SUBLANES - k if reverse else k
        valid = (sub < SUBLANES - k) if reverse else (sub >= k)
        b3 = b3 + a3 * jnp.where(valid, pltpu.roll(b3, shift, axis=1), 0.0)
        a3 = a3 * jnp.where(valid, pltpu.roll(a3, shift, axis=1), 1.0)
    return a3.reshape(rows, c), b3.reshape(rows, c)


def _lru_kernel(group, has_gate, *refs):
    if has_gate:
        (x_ref, gate_ref, cw_ref, cb_ref, wa_ref, ba_ref, wx_ref, bx_ref, lam_ref, h0_ref,
         y_ref, ht_ref, af_ref, bf_ref, ab_ref, bb_ref) = refs
    else:
        (x_ref, cw_ref, cb_ref, wa_ref, ba_ref, wx_ref, bx_ref, lam_ref, h0_ref,
         ht_ref, af_ref, bf_ref, ab_ref, bb_ref) = refs
    t_len = x_ref.shape[1]
    chunk = _tile(t_len, 512)
    a_refs = (af_ref, ab_ref)
    b_refs = (bf_ref, bb_ref)

    def coeffs(ci, carry):
        r0 = pl.multiple_of(ci * chunk, chunk)
        xc = _short_conv(x_ref[0, pl.ds(r0, chunk), :], cw_ref, cb_ref[...], LRU_CONV_LEFT, group)
        xcb = xc.astype(BF16)
        for d in range(2):
            r = _sigmoid(jnp.dot(xcb, wa_ref[d, 0], preferred_element_type=F32) + ba_ref[d])
            gi = _sigmoid(jnp.dot(xcb, wx_ref[d, 0], preferred_element_type=F32) + bx_ref[d])
            lam = lam_ref[d]
            softplus_neg = jnp.maximum(-lam, 0.0) + jnp.log(1.0 + jnp.exp(-jnp.abs(lam)))
            log_a = -LRU_C * r * softplus_neg
            a = jnp.exp(log_a)
            a_tile, b_tile = _tile_scan(a, jnp.sqrt(1.0 - a * a) * (gi * xc), d == 1)
            a_refs[d][pl.ds(r0, chunk), :] = a_tile
            b_refs[d][pl.ds(r0, chunk), :] = b_tile
        return carry

    lax.fori_loop(0, t_len // chunk, coeffs, 0)

    n_tiles = t_len // SUBLANES
    unroll = _tile(n_tiles, SCAN_UNROLL)

    def scan(i, carry):
        hf, hb = carry
        for u in range(unroll):
            kf = i * unroll + u
            rf = pl.multiple_of(kf * SUBLANES, SUBLANES)
            tile_f = af_ref[pl.ds(rf, SUBLANES), :] * hf + bf_ref[pl.ds(rf, SUBLANES), :]
            bf_ref[pl.ds(rf, SUBLANES), :] = tile_f
            hf = tile_f[SUBLANES - 1:SUBLANES, :]
            rb = pl.multiple_of((n_tiles - 1 - kf) * SUBLANES, SUBLANES)
            tile_b = ab_ref[pl.ds(rb, SUBLANES), :] * hb + bb_ref[pl.ds(rb, SUBLANES), :]
            bb_ref[pl.ds(rb, SUBLANES), :] = tile_b
            hb = tile_b[0:1, :]
        return hf, hb

    hf, hb = lax.fori_loop(0, n_tiles // unroll, scan, (h0_ref[0, 0:1, :], h0_ref[0, 1:2, :]))
    ht_ref[0, 0:1, :] = hf
    ht_ref[0, 1:2, :] = hb

    if has_gate:
        def emit(ci, carry):
            r0 = pl.multiple_of(ci * chunk, chunk)
            hs = bf_ref[pl.ds(r0, chunk), :] + bb_ref[pl.ds(r0, chunk), :]
            y_ref[0, pl.ds(r0, chunk), :] = (jax.nn.gelu(gate_ref[0, pl.ds(r0, chunk), :]) * hs).astype(y_ref.dtype)
            return carry

        lax.fori_loop(0, t_len // chunk, emit, 0)


def _block_diag(w, heads_per_tile):
    d2, h, hd, _ = w.shape
    w = w.reshape(d2, h // heads_per_tile, heads_per_tile, hd, hd)
    eye = jnp.eye(heads_per_tile, dtype=w.dtype)
    bd = jnp.einsum('dghij,hq->dghiqj', w, eye)
    return bd.reshape(d2, h // heads_per_tile, heads_per_tile * hd, heads_per_tile * hd).astype(BF16)


def _rglru(xsrc, x_col, gsrc, g_col, conv_w, conv_b, wa, ba, wx, bx, lam, h0, group):
    b, t, _ = xsrc.shape
    c = conv_w.shape[1]
    hd = c // LRU_HEADS
    tc = min(c, max(hd, 256))
    hp = tc // hd
    nt = c // tc
    has_gate = gsrc is not None
    wa_bd = _block_diag(wa, hp)
    wx_bd = _block_diag(wx, hp)
    xo, go = x_col // tc, (g_col // tc if has_gate else 0)
    vec = lambda: pl.BlockSpec((2, 1, tc), lambda i, j: (0, 0, j))
    mat = lambda: pl.BlockSpec((2, 1, tc, tc), lambda i, j: (0, j, 0, 0))
    in_specs = [pl.BlockSpec((1, t, tc), lambda i, j: (i, 0, j + xo))]
    args = [xsrc]
    if has_gate:
        in_specs.append(pl.BlockSpec((1, t, tc), lambda i, j: (i, 0, j + go)))
        args.append(gsrc)
    in_specs += [pl.BlockSpec((conv_w.shape[0], tc), lambda i, j: (0, j)),
                 pl.BlockSpec((1, tc), lambda i, j: (0, j)),
                 mat(), vec(), mat(), vec(), vec(),
                 pl.BlockSpec((1, 2, tc), lambda i, j: (i, 0, j))]
    args += [conv_w, conv_b.reshape(1, c), wa_bd, ba.reshape(2, 1, c), wx_bd, bx.reshape(2, 1, c),
             lam.reshape(2, 1, c), h0]
    ht_spec = pl.BlockSpec((1, 2, tc), lambda i, j: (i, 0, j))
    ht_shape = jax.ShapeDtypeStruct((b, 2, c), F32)
    if has_gate:
        out_specs = [pl.BlockSpec((1, t, tc), lambda i, j: (i, 0, j)), ht_spec]
        out_shape = [jax.ShapeDtypeStruct((b, t, c), BF16), ht_shape]
    else:
        out_specs = [ht_spec]
        out_shape = [ht_shape]
    return pl.pallas_call(
        functools.partial(_lru_kernel, group, has_gate),
        grid=(b, nt),
        in_specs=in_specs,
        out_specs=out_specs,
        out_shape=out_shape,
        scratch_shapes=[pltpu.VMEM((t, tc), F32) for _ in range(4)],
        compiler_params=_params(("parallel", "parallel")),
        name="rglru" if has_gate else "rglru_context",
    )(*args)


def _out_proj_kernel(a1_ref, a2_ref, w1_ref, w2_ref, b_ref, x_ref, g_ref, o_ref):
    acc = jnp.dot(a1_ref[...], w1_ref[...], preferred_element_type=F32)
    acc = acc + jnp.dot(a2_ref[...], w2_ref[...], preferred_element_type=F32)
    o_ref[...] = x_ref[...] + g_ref[0] * (acc + b_ref[...])


def _out_projection(a1, a2, w, bias, x, gate, t):
    m, k1 = a1.shape
    n = w.shape[1]
    tm = _tile(t, 1024)
    tn = _tile(n, 512)
    per_b = t // tm
    k1_blocks = 1
    return pl.pallas_call(
        _out_proj_kernel,
        grid=(m // tm, n // tn),
        in_specs=[pl.BlockSpec((tm, k1), lambda i, j: (i, 0)),
                  pl.BlockSpec((tm, k1), lambda i, j: (i, 0)),
                  pl.BlockSpec((k1, tn), lambda i, j: (0, j)),
                  pl.BlockSpec((k1, tn), lambda i, j: (k1_blocks, j)),
                  pl.BlockSpec((1, tn), lambda i, j: (0, j)),
                  pl.BlockSpec((tm, tn), lambda i, j: (i, j)),
                  pl.BlockSpec((1, 1, tn), lambda i, j: (i // per_b, 0, j))],
        out_specs=pl.BlockSpec((tm, tn), lambda i, j: (i, j)),
        out_shape=jax.ShapeDtypeStruct((m, n), F32),
        compiler_params=_params(("parallel", "arbitrary")),
        name="out_projection",
    )(a1, a2, w, w, bias.reshape(1, n), x, gate)


def _router_kernel(x_ref, g_ref, sh_ref, sc_ref, wr_ref, h_ref, aff_ref):
    h = _rms_mod(x_ref[0], g_ref[...], sh_ref[0], sc_ref[0])
    h_ref[...] = h
    h_hi, h_lo = _split_bf16(h)
    w_hi, w_lo = _split_bf16(wr_ref[...])
    nt = (((1,), (1,)), ((), ()))
    logits = (lax.dot_general(w_hi, h_hi, nt, preferred_element_type=F32)
              + lax.dot_general(w_lo, h_hi, nt, preferred_element_type=F32)
              + lax.dot_general(w_hi, h_lo, nt, preferred_element_type=F32))
    z = jnp.exp(logits - jnp.max(logits, axis=0, keepdims=True))
    aff_ref[0] = z / jnp.sum(z, axis=0, keepdims=True)


def _router(x1, g, shift, scale, w_router):
    b, t, d = x1.shape
    e = w_router.shape[1]
    tt = _tile(t, 256)
    per_b = t // tt
    mod_map = lambda i, j: (i, 0, 0)
    return pl.pallas_call(
        _router_kernel,
        grid=(b, per_b),
        in_specs=[pl.BlockSpec((1, tt, d), lambda i, j: (i, j, 0)),
                  pl.BlockSpec((1, d), lambda i, j: (0, 0)),
                  pl.BlockSpec((1, 1, d), mod_map),
                  pl.BlockSpec((1, 1, d), mod_map),
                  pl.BlockSpec((e, d), lambda i, j: (0, 0))],
        out_specs=[pl.BlockSpec((tt, d), lambda i, j: (i * per_b + j, 0)),
                   pl.BlockSpec((1, e, tt), lambda i, j: (i, 0, j))],
        out_shape=[jax.ShapeDtypeStruct((b * t, d), F32),
                   jax.ShapeDtypeStruct((b, e, t), F32)],
        compiler_params=_params(("parallel", "parallel")),
        name="router",
    )(x1, g.reshape(1, d), shift, scale, w_router.T)


def _prefix_count(mask_ref, out_ref):
    e, t = mask_ref.shape
    blk = min(t, LANES)
    tri = (lax.broadcasted_iota(jnp.int32, (blk, blk), 0)
           < lax.broadcasted_iota(jnp.int32, (blk, blk), 1)).astype(BF16)
    carry = jnp.zeros((e, 1), F32)
    for k in range(t // blk):
        m = mask_ref[:, k * blk:(k + 1) * blk]
        out_ref[:, k * blk:(k + 1) * blk] = jnp.dot(m.astype(BF16), tri, preferred_element_type=F32) + carry
        carry = carry + jnp.sum(m, axis=1, keepdims=True)
    return carry


def _topk_kernel(cap, aff_ref, idx_ref, gate_ref, dest_ref, span_ref, mask_ref, pos_ref, cnt_ref, q_ref):
    a = aff_ref[0]
    e, t = a.shape
    min_normal = 0x00800000

    def refine(i, thr):
        cand = thr | jnp.left_shift(jnp.int32(1), 30 - i)
        cnt = jnp.sum(jnp.where(a >= pltpu.bitcast(cand, F32), 1.0, 0.0), axis=1, keepdims=True)
        return jnp.where((cnt >= cap) & (cand >= min_normal), cand, thr)

    thr = lax.fori_loop(0, 31, refine, jnp.zeros((e, 1), jnp.int32))
    above = a >= pltpu.bitcast(jnp.maximum(thr + 1, min_normal), F32)
    tied = (a >= pltpu.bitcast(thr, F32)) & jnp.logical_not(above)
    need = cap - jnp.sum(jnp.where(above, 1.0, 0.0), axis=1, keepdims=True)
    mask_ref[...] = jnp.where(tied, 1.0, 0.0)
    _prefix_count(mask_ref, pos_ref)
    sel = above | (tied & (pos_ref[...] < need))
    mask_ref[...] = jnp.where(sel, 1.0, 0.0)
    _prefix_count(mask_ref, pos_ref)

    mask = mask_ref[...]
    cnt = jnp.sum(mask, axis=0, keepdims=True)
    cnt_ref[...] = jnp.broadcast_to(cnt, cnt_ref.shape)
    _prefix_count(cnt_ref, q_ref.at[0:SUBLANES])
    first = q_ref[0:1, :]
    srow = lax.broadcasted_iota(jnp.int32, (SUBLANES, 1), 0)
    span_ref[0] = jnp.where(srow == 0, first, jnp.where(srow == 1, first + cnt, 0.0))
    lower = (lax.broadcasted_iota(jnp.int32, (e, e), 1)
             < lax.broadcasted_iota(jnp.int32, (e, e), 0)).astype(BF16)
    q_ref[...] = first + jnp.dot(lower, mask.astype(BF16), preferred_element_type=F32)

    tok = lax.broadcasted_iota(jnp.int32, (1, t), 1)
    tok_hi = (tok >> 6).astype(F32)
    tok_lo = (tok & 63).astype(F32)
    slot = lax.broadcasted_iota(jnp.int32, (cap, 1), 0).astype(F32)
    vrow = lax.broadcasted_iota(jnp.int32, (SUBLANES, 1), 0)
    dest_base = pl.program_id(0) * (e * cap)

    def compact(ei, carry):
        g = aff_ref[0, pl.ds(ei, 1), :]
        g_hi = g.astype(BF16).astype(F32)
        g_mid = (g - g_hi).astype(BF16).astype(F32)
        g_lo = g - g_hi - g_mid
        q = q_ref[pl.ds(ei, 1), :]
        q_hi = jnp.floor(q * (1.0 / 64.0))
        q_lo = q - 64.0 * q_hi
        vals = jnp.where(vrow == 0, tok_hi, jnp.where(vrow == 1, tok_lo, jnp.where(
            vrow == 2, g_hi, jnp.where(vrow == 3, g_mid, jnp.where(vrow == 4, g_lo, jnp.where(
                vrow == 5, q_hi, jnp.where(vrow == 6, q_lo, 0.0))))))).astype(BF16)
        hit = (pos_ref[pl.ds(ei, 1), :] == slot) & (mask_ref[pl.ds(ei, 1), :] > 0.0)
        onehot = jnp.where(hit, 1.0, 0.0).astype(BF16)
        res = lax.dot_general(vals, onehot, (((1,), (1,)), ((), ())), preferred_element_type=F32)
        idx_ref[0, pl.ds(ei, 1), :] = (res[0:1] * 64.0 + res[1:2]).astype(jnp.int32)
        gate_ref[0, pl.ds(ei, 1), :] = res[2:3] + res[3:4] + res[4:5]
        dest_ref[0, pl.ds(ei, 1), :] = (res[5:6] * 64.0 + res[6:7]).astype(jnp.int32) + dest_base
        return carry

    lax.fori_loop(0, e, compact, 0)


def _select_tokens(aff, cap):
    b, e, t = aff.shape
    assert t <= 64 * 256 and e * cap <= 64 * 256 and e >= SUBLANES
    spec = pl.BlockSpec((1, e, cap), lambda i: (i, 0, 0))
    return pl.pallas_call(
        functools.partial(_topk_kernel, cap),
        grid=(b,),
        in_specs=[pl.BlockSpec((1, e, t), lambda i: (i, 0, 0))],
        out_specs=[spec, spec, spec, pl.BlockSpec((1, SUBLANES, t), lambda i: (i, 0, 0))],
        out_shape=[jax.ShapeDtypeStruct((b, e, cap), jnp.int32), jax.ShapeDtypeStruct((b, e, cap), F32),
                   jax.ShapeDtypeStruct((b, e, cap), jnp.int32), jax.ShapeDtypeStruct((b, SUBLANES, t), F32)],
        scratch_shapes=[pltpu.VMEM((e, t), F32), pltpu.VMEM((e, t), F32),
                        pltpu.VMEM((SUBLANES, t), F32), pltpu.VMEM((e, t), F32)],
        compiler_params=_params(("parallel",)),
        name="select_tokens",
    )(aff)


def _gather_kernel(rows_ref, h_hbm, o_ref, buf, sem):
    i = pl.program_id(0)
    r_blk = buf.shape[1]

    def issue_block(blk, slot):
        def issue(r, carry):
            row = rows_ref[blk * r_blk + r]
            pltpu.make_async_copy(h_hbm.at[pl.ds(row, 1), :], buf.at[slot, pl.ds(r, 1), :], sem.at[slot]).start()
            return carry

        lax.fori_loop(0, r_blk, issue, 0, unroll=8)

    @pl.when(i == 0)
    def _():
        issue_block(0, 0)

    @pl.when(i + 1 < pl.num_programs(0))
    def _():
        issue_block(i + 1, (i + 1) & 1)

    slot = i & 1
    pltpu.make_async_copy(h_hbm.at[pl.ds(0, r_blk), :], buf.at[slot], sem.at[slot]).wait()
    o_ref[...] = buf[slot].astype(o_ref.dtype)


def _gather_rows(rows, h):
    n_rows = rows.shape[0]
    d = h.shape[1]
    r_blk = _tile(n_rows, 256)
    grid_spec = pltpu.PrefetchScalarGridSpec(
        num_scalar_prefetch=1,
        grid=(n_rows // r_blk,),
        in_specs=[pl.BlockSpec(memory_space=pl.ANY)],
        out_specs=pl.BlockSpec((r_blk, d), lambda i, rows_ref: (i, 0)),
        scratch_shapes=[pltpu.VMEM((2, r_blk, d), F32), pltpu.SemaphoreType.DMA((2,))],
    )
    return pl.pallas_call(
        _gather_kernel,
        grid_spec=grid_spec,
        out_shape=jax.ShapeDtypeStruct((n_rows, d), BF16),
        compiler_params=_params(("arbitrary",)),
        name="gather_rows",
    )(rows, h)


def _expert_up_kernel(xs_ref, wg_ref, wu_ref, h_ref):
    xs = xs_ref[...]
    g = jnp.dot(xs, wg_ref[0].astype(BF16), preferred_element_type=F32)
    u = jnp.dot(xs, wu_ref[0].astype(BF16), preferred_element_type=F32)
    h_ref[...] = (g * jax.nn.sigmoid(g) * u).astype(h_ref.dtype)


def _expert_up(xs, w_gate, w_up, rows_per_expert):
    m, d = xs.shape
    f = w_gate.shape[2]
    tm = _tile(rows_per_expert, 1024)
    per_e = rows_per_expert // tm
    tf = _tile(f, 256)
    wspec = pl.BlockSpec((1, d, tf), lambda i, j: (i // per_e, 0, j))
    return pl.pallas_call(
        _expert_up_kernel,
        grid=(m // tm, f // tf),
        in_specs=[pl.BlockSpec((tm, d), lambda i, j: (i, 0)), wspec, wspec],
        out_specs=pl.BlockSpec((tm, tf), lambda i, j: (i, j)),
        out_shape=jax.ShapeDtypeStruct((m, f), BF16),
        compiler_params=_params(("parallel", "arbitrary")),
        name="expert_up",
    )(xs, w_gate, w_up)


def _expert_down_kernel(dest_ref, h_ref, wd_ref, g_ref, ys_hbm, ybuf, sem):
    i = pl.program_id(0)
    j = pl.program_id(1)
    n_i = pl.num_programs(0)
    _, tm, d = ybuf.shape
    td = wd_ref.shape[2]
    slot = i & 1
    y = jnp.dot(h_ref[...], wd_ref[0].astype(BF16), preferred_element_type=F32) * g_ref[...]

    def wait_scatter(s):
        pltpu.make_async_copy(ybuf.at[s], ys_hbm.at[pl.ds(0, tm), :], sem.at[s]).wait()

    @pl.when((j == 0) & (i >= 2))
    def _():
        wait_scatter(slot)

    for jj in range(d // td):
        @pl.when(j == jj)
        def _(jj=jj):
            ybuf[slot, :, jj * td:(jj + 1) * td] = y

    @pl.when(j == pl.num_programs(1) - 1)
    def _():
        def issue(r, carry):
            row = dest_ref[i * tm + r]
            pltpu.make_async_copy(ybuf.at[slot, pl.ds(r, 1), :], ys_hbm.at[pl.ds(row, 1), :], sem.at[slot]).start()
            return carry

        lax.fori_loop(0, tm, issue, 0, unroll=8)

        @pl.when(i == n_i - 1)
        def _():
            @pl.when(i >= 1)
            def _():
                wait_scatter(1 - slot)
            wait_scatter(slot)


def _expert_down(dest, h, w_down, gates, rows_per_expert):
    m, f = h.shape
    d = w_down.shape[2]
    tm = _tile(rows_per_expert, 1024)
    per_e = rows_per_expert // tm
    td = _tile(d, 512)
    grid_spec = pltpu.PrefetchScalarGridSpec(
        num_scalar_prefetch=1,
        grid=(m // tm, d // td),
        in_specs=[pl.BlockSpec((tm, f), lambda i, j, dest_ref: (i, 0)),
                  pl.BlockSpec((1, f, td), lambda i, j, dest_ref: (i // per_e, 0, j)),
                  pl.BlockSpec((tm, 1), lambda i, j, dest_ref: (i, 0))],
        out_specs=pl.BlockSpec(memory_space=pl.ANY),
        scratch_shapes=[pltpu.VMEM((2, tm, d), F32), pltpu.SemaphoreType.DMA((2,))],
    )
    return pl.pallas_call(
        _expert_down_kernel,
        grid_spec=grid_spec,
        out_shape=jax.ShapeDtypeStruct((m, d), F32),
        compiler_params=_params(("arbitrary", "arbitrary")),
        name="expert_down",
    )(dest, h, w_down, gates)


def _combine_kernel(slots_per_batch, tb_ref, span_ref, x_ref, gt_ref, g_ref, ys_hbm, o_ref,
                    buf, sem, acc_ref):
    bi = pl.program_id(0)
    i = pl.program_id(1)
    n_t = pl.num_programs(1)
    kc = buf.shape[1]
    total = ys_hbm.shape[0]
    c_lo = tb_ref[bi * (n_t + 1) + i]
    c_hi = tb_ref[bi * (n_t + 1) + i + 1]
    k_first = (c_lo >> 3) << 3
    n_chunks = (c_hi - k_first + kc - 1) // kc

    def chunk_start(c):
        return pl.multiple_of(jnp.minimum(k_first + c * kc, total - kc), SUBLANES)

    def copy(c, slot):
        return pltpu.make_async_copy(ys_hbm.at[pl.ds(chunk_start(c), kc), :], buf.at[slot], sem.at[slot])

    @pl.when(n_chunks > 0)
    def _():
        copy(0, 0).start()

    acc_ref[...] = jnp.zeros_like(acc_ref)
    base = (bi * slots_per_batch).astype(F32)
    first = span_ref[0][:, 0:1] + base
    last = span_ref[0][:, 1:2] + base

    def body(c, carry):
        slot = c & 1
        copy(c, slot).wait()

        @pl.when(c + 1 < n_chunks)
        def _():
            copy(c + 1, 1 - slot).start()

        k = chunk_start(c) + lax.broadcasted_iota(jnp.int32, (1, kc), 1)
        k = jnp.where(k >= k_first + c * kc, k, -1).astype(F32)
        onehot = jnp.where((k >= first) & (k < last), 1.0, 0.0).astype(BF16)
        acc_ref[...] += jnp.dot(onehot, buf[slot].astype(BF16), preferred_element_type=F32)
        return carry

    lax.fori_loop(0, n_chunks, body, 0)
    x = x_ref[0] + gt_ref[0] * acc_ref[...]
    o_ref[0] = x * lax.rsqrt(jnp.mean(x * x, axis=-1, keepdims=True) + EPS) * g_ref[...]


def _combine(tile_bounds, span, ys, x1, gate, g_final, slots_per_batch):
    b, t, d = x1.shape
    tt = _tile(t, 256)
    kc = min(256, ys.shape[0])
    grid_spec = pltpu.PrefetchScalarGridSpec(
        num_scalar_prefetch=1,
        grid=(b, t // tt),
        in_specs=[pl.BlockSpec((1, tt, SUBLANES), lambda bi, i, tb: (bi, i, 0)),
                  pl.BlockSpec((1, tt, d), lambda bi, i, tb: (bi, i, 0)),
                  pl.BlockSpec((1, 1, d), lambda bi, i, tb: (bi, 0, 0)),
                  pl.BlockSpec((1, d), lambda bi, i, tb: (0, 0)),
                  pl.BlockSpec(memory_space=pl.ANY)],
        out_specs=pl.BlockSpec((1, tt, d), lambda bi, i, tb: (bi, i, 0)),
        scratch_shapes=[pltpu.VMEM((2, kc, d), F32), pltpu.SemaphoreType.DMA((2,)),
                        pltpu.VMEM((tt, d), F32)],
    )
    return pl.pallas_call(
        functools.partial(_combine_kernel, slots_per_batch),
        grid_spec=grid_spec,
        out_shape=jax.ShapeDtypeStruct((b, t, d), F32),
        compiler_params=_params(("arbitrary", "arbitrary")),
        name="combine",
    )(tile_bounds, span, x1, gate, g_final.reshape(1, d), ys)


def kernel(x, c, ctx, c_ctx, w_mod, b_mod, g_mix, g_ffn, w_in, b_in, hy_conv_w, hy_conv_b, hy_f_w1, hy_f_b1, hy_f_w2, hy_f_b2, hy_f_w3, hy_f_b3, hy_f_wout, hy_f_freq, hy_bias, lru_conv_w, lru_conv_b, lru_wa, lru_ba, lru_wx, lru_bx, lru_lambda, w_out, b_out, w_router, w_exp_gate, w_exp_up, w_exp_down, g_final):
    bsz, n_lat, d = x.shape
    n_ctx = ctx.shape[1]
    depth = w_mod.shape[0]
    c_hy = hy_bias.shape[1]
    c_lru = lru_conv_b.shape[1]
    in_gate = 3 * c_hy
    in_x = in_gate + c_lru
    n_exp = w_router.shape[2]
    cap = CAPACITY_FACTOR * n_lat // n_exp
    assert depth == 1, "context residual updates are only needed for depth > 1"
    assert n_lat % GRID_W == 0 and n_ctx & (n_ctx - 1) == 0

    rows = -(-(bsz + 1) // SUBLANES) * SUBLANES
    c_all = jnp.zeros((rows, d), F32).at[:bsz].set(c).at[bsz].set(c_ctx)
    mats = _fft_matrices(n_lat)

    l = 0
    mod = _modulation(c_all, w_mod[l], b_mod[l])
    mx = mod[:bsz].reshape(bsz, 1, N_MOD, d)
    sh1, sc1, gt1, sh2, sc2, gt2 = (mx[:, :, k] for k in range(N_MOD))
    mc = mod[bsz].reshape(1, 1, N_MOD, d)
    csh1, csc1 = mc[:, :, 0], mc[:, :, 1]

    w_in_b = w_in[l].astype(BF16)
    w_out_b = w_out[l].astype(BF16)

    hx = _norm_mod(x, g_mix[l], sh1, sc1, True)
    hc = _norm_mod(ctx, g_mix[l], csh1, csc1, False)
    px = _projection(hx.reshape(bsz * n_lat, d), w_in_b, b_in[l], 0, w_in_b.shape[1], F32)
    px = px.reshape(bsz, n_lat, -1)
    pc_lx = _projection(hc.reshape(bsz * n_ctx, d), w_in_b, b_in[l], in_x, c_lru, F32)
    pc_lx = pc_lx.reshape(bsz, n_ctx, c_lru)

    lru_args = (lru_conv_w[l], lru_conv_b[l], lru_wa[l], lru_ba[l], lru_wx[l], lru_bx[l], lru_lambda[l])
    (h_ctx,) = _rglru(pc_lx, 0, None, 0, *lru_args, jnp.zeros((bsz, 2, c_lru), F32), n_ctx)
    y_lru, _ = _rglru(px, in_x, px, in_gate, *lru_args, h_ctx, GRID_W)

    taps, norm = _hyena_filter(n_lat, hy_f_w1[l], hy_f_b1[l], hy_f_w2[l], hy_f_b2[l], hy_f_w3[l],
                               hy_f_b3[l], hy_f_wout[l], hy_f_freq[l])
    kspec = _filter_spectrum(taps, norm, mats, c_hy)
    x0, wf = _hyena_pre(px, hy_conv_w[l], hy_conv_b[l], c_hy)
    yspec = _conv_forward(wf, kspec, mats)
    y_hy = _conv_inverse(yspec, wf, x0, hy_bias[l], mats)

    x1 = _out_projection(y_hy.reshape(bsz * n_lat, c_hy), y_lru.reshape(bsz * n_lat, c_lru), w_out_b,
                         b_out[l], x.reshape(bsz * n_lat, d), gt1, n_lat)
    x1 = x1.reshape(bsz, n_lat, d)

    h2, aff = _router(x1, g_ffn[l], sh2, sc2, w_router[l])
    idx, gates, dest, span = _select_tokens(aff, cap)
    rows_g = (idx + (jnp.arange(bsz, dtype=jnp.int32) * n_lat)[:, None, None])
    expert_major = lambda v: jnp.swapaxes(v, 0, 1).reshape(-1)
    xs = _gather_rows(expert_major(rows_g), h2)
    hmid = _expert_up(xs, w_exp_gate[l], w_exp_up[l], bsz * cap)
    ys = _expert_down(expert_major(dest), hmid, w_exp_down[l], expert_major(gates).reshape(-1, 1), bsz * cap)
    slots = n_exp * cap
    tt = _tile(n_lat, 256)
    base = (jnp.arange(bsz, dtype=jnp.int32) * slots)[:, None]
    bounds = jnp.concatenate([span[:, 0, ::tt].astype(jnp.int32) + base, base + slots], axis=1).reshape(-1)
    return _combine(bounds, jnp.swapaxes(span, 1, 2), ys, x1, gt2, g_final, slots)
```

```python
import functools
import math

import jax
import jax.numpy as jnp
from jax import lax
from jax.experimental import pallas as pl
from jax.experimental.pallas import tpu as pltpu

F32 = jnp.float32
BF16 = jnp.bfloat16

GRID_W = 64
HY_SHORT_LEFT = 1
FILT_BANDS = 16
DECAY_TARGET = 1e-2
MIN_DECAY = math.log(DECAY_TARGET) / 1.5
MAX_DECAY = math.log(DECAY_TARGET) / 0.3
LRU_HEADS = 16
LRU_CONV_LEFT = 2
LRU_C = 8.0
CAPACITY_FACTOR = 2
N_MOD = 6
EPS = 1e-6

LANES = 128
SUBLANES = 8
VMEM_LIMIT = 56 * 2 ** 20


def _params(sem):
    return pltpu.CompilerParams(dimension_semantics=sem, vmem_limit_bytes=VMEM_LIMIT)


def _tile(n, pref):
    t = min(n, pref)
    while n % t:
        t //= 2
    return t


def _split_bf16(v):
    hi = v.astype(BF16)
    lo = (v - hi.astype(F32)).astype(BF16)
    return hi, lo


def _mod_kernel(c_ref, w_ref, b_ref, o_ref):
    c = c_ref[...]
    s = c * jax.nn.sigmoid(c)
    s_hi, s_lo = _split_bf16(s)
    w_hi, w_lo = _split_bf16(w_ref[...])
    rows = c.shape[0]
    r = jnp.dot(jnp.concatenate([s_hi, s_lo], axis=0), w_hi, preferred_element_type=F32)
    acc = r[:rows] + r[rows:] + jnp.dot(s_hi, w_lo, preferred_element_type=F32)
    o_ref[...] = acc + b_ref[...]


def _modulation(c_all, w_mod, b_mod):
    rows, d = c_all.shape
    n = w_mod.shape[1]
    tn = _tile(n, 256)
    return pl.pallas_call(
        _mod_kernel,
        grid=(n // tn,),
        in_specs=[pl.BlockSpec((rows, d), lambda j: (0, 0)),
                  pl.BlockSpec((d, tn), lambda j: (0, j)),
                  pl.BlockSpec((1, tn), lambda j: (0, j))],
        out_specs=pl.BlockSpec((rows, tn), lambda j: (0, j)),
        out_shape=jax.ShapeDtypeStruct((rows, n), F32),
        compiler_params=_params(("parallel",)),
        name="modulation",
    )(c_all, w_mod, b_mod.reshape(1, n))


def _rms_mod(x, g, shift, scale):
    y = x * lax.rsqrt(jnp.mean(x * x, axis=-1, keepdims=True) + EPS) * g
    return y * (1.0 + scale) + shift


def _norm_kernel(x_ref, g_ref, sh_ref, sc_ref, o_ref):
    o_ref[0] = _rms_mod(x_ref[0], g_ref[...], sh_ref[0], sc_ref[0]).astype(o_ref.dtype)


def _norm_mod(x, g, shift, scale, per_batch):
    b, t, d = x.shape
    tt = _tile(t, 512)
    mod_map = (lambda i, j: (i, 0, 0)) if per_batch else (lambda i, j: (0, 0, 0))
    return pl.pallas_call(
        _norm_kernel,
        grid=(b, t // tt),
        in_specs=[pl.BlockSpec((1, tt, d), lambda i, j: (i, j, 0)),
                  pl.BlockSpec((1, d), lambda i, j: (0, 0)),
                  pl.BlockSpec((1, 1, d), mod_map),
                  pl.BlockSpec((1, 1, d), mod_map)],
        out_specs=pl.BlockSpec((1, tt, d), lambda i, j: (i, j, 0)),
        out_shape=jax.ShapeDtypeStruct((b, t, d), BF16),
        compiler_params=_params(("parallel", "parallel")),
        name="adaln_norm",
    )(x, g.reshape(1, d), shift, scale)


def _proj_kernel(a_ref, w_ref, b_ref, o_ref):
    acc = jnp.dot(a_ref[...], w_ref[...], preferred_element_type=F32)
    o_ref[...] = (acc + b_ref[...]).astype(o_ref.dtype)


def _projection(a, w, bias, col_start, n_cols, out_dtype):
    m, k = a.shape
    tm = _tile(m, 1024)
    tn = _tile(n_cols, 512)
    off = col_start // tn
    return pl.pallas_call(
        _proj_kernel,
        grid=(m // tm, n_cols // tn),
        in_specs=[pl.BlockSpec((tm, k), lambda i, j: (i, 0)),
                  pl.BlockSpec((k, tn), lambda i, j: (0, j + off)),
                  pl.BlockSpec((1, tn), lambda i, j: (0, j + off))],
        out_specs=pl.BlockSpec((tm, tn), lambda i, j: (i, j)),
        out_shape=jax.ShapeDtypeStruct((m, n_cols), out_dtype),
        compiler_params=_params(("parallel", "arbitrary")),
        name="projection",
    )(a, w, bias.reshape(1, -1))


def _short_conv(x, w_ref, bias, left, group):
    rows = x.shape[0]
    pos = lax.broadcasted_iota(jnp.int32, (rows, 1), 0) & (group - 1)
    y = bias + w_ref[left:left + 1, :] * x
    for k in range(w_ref.shape[0]):
        off = k - left
        if off == 0:
            continue
        shifted = pltpu.roll(x, (-off) % rows, axis=0)
        valid = (pos + off >= 0) & (pos + off < group)
        y = y + w_ref[k:k + 1, :] * jnp.where(valid, shifted, 0.0)
    return y


def _proj_hyena_kernel(a_ref, w0_ref, w1_ref, w2_ref, b0_ref, b1_ref, b2_ref,
                       cw0_ref, cw1_ref, cw2_ref, cb0_ref, cb1_ref, cb2_ref, x0_ref, wf_ref):
    a = a_ref[...]

    def branch(w_ref, b_ref, cw_ref, cb_ref):
        p = jnp.dot(a, w_ref[...], preferred_element_type=F32) + b_ref[...]
        return _short_conv(p, cw_ref, cb_ref[...], HY_SHORT_LEFT, GRID_W)

    x0_ref[...] = branch(w0_ref, b0_ref, cw0_ref, cb0_ref)
    wf_ref[...] = branch(w1_ref, b1_ref, cw1_ref, cb1_ref) * branch(w2_ref, b2_ref, cw2_ref, cb2_ref)


def _projection_hyena(a, w, bias, conv_w, conv_b, c):
    m, k = a.shape
    tm = _tile(m, 1024)
    assert tm % GRID_W == 0
    tc = _tile(c, 256)
    nc = c // tc
    kw = conv_w.shape[0]
    wspec = lambda g: pl.BlockSpec((k, tc), lambda i, j: (0, j + g * nc))
    bspec = lambda g: pl.BlockSpec((1, tc), lambda i, j: (0, j + g * nc))
    cwspec = lambda g: pl.BlockSpec((kw, tc), lambda i, j: (0, j + g * nc))
    ospec = pl.BlockSpec((tm, tc), lambda i, j: (i, j))
    b2 = bias.reshape(1, -1)
    cb = conv_b.reshape(1, -1)
    shape = jax.ShapeDtypeStruct((m, c), F32)
    return pl.pallas_call(
        _proj_hyena_kernel,
        grid=(m // tm, nc),
        in_specs=[pl.BlockSpec((tm, k), lambda i, j: (i, 0)),
                  wspec(0), wspec(1), wspec(2), bspec(0), bspec(1), bspec(2),
                  cwspec(0), cwspec(1), cwspec(2), bspec(0), bspec(1), bspec(2)],
        out_specs=[ospec, ospec],
        out_shape=[shape, shape],
        compiler_params=_params(("parallel", "arbitrary")),
        name="projection_hyena",
    )(a, w, w, w, b2, b2, b2, conv_w, conv_w, conv_w, cb, cb, cb)


def _hp_dot(a, b):
    return jnp.dot(a, b, preferred_element_type=F32, precision=lax.Precision.HIGHEST)


def _filter_kernel(n, w1t_ref, w1c_ref, w1s_ref, b1_ref, w2_ref, b2_ref, w3_ref, b3_ref,
                   wout_ref, freq_ref, delta_ref, h_ref, norm_ref):
    i = pl.program_id(0)
    tn = h_ref.shape[0]
    pos_i = i * tn + lax.broadcasted_iota(jnp.int32, (tn, 1), 0)
    pos = pos_i.astype(F32)
    t = pos * (1.0 / (n - 1))
    band_step = (FILT_BANDS - 1 - 1e-4) / (FILT_BANDS - 1)
    bands = 1e-4 + band_step * lax.broadcasted_iota(jnp.int32, (1, FILT_BANDS), 1).astype(F32)
    ang = (2.0 * math.pi * pos / n) * bands
    fr = freq_ref[...]
    pre = t * w1t_ref[...] + _hp_dot(jnp.cos(ang), w1c_ref[...]) - _hp_dot(jnp.sin(ang), w1s_ref[...])
    h = jnp.sin(fr * (pre + b1_ref[...]))
    h = jnp.sin(fr * (_hp_dot(h, w2_ref[...]) + b2_ref[...]))
    h = jnp.sin(fr * (_hp_dot(h, w3_ref[...]) + b3_ref[...]))
    taps = _hp_dot(h, wout_ref[...]) * jnp.exp(-t * delta_ref[...])

    @pl.when(i == 0)
    def _():
        norm_ref[...] = jnp.zeros_like(norm_ref)

    norm_ref[...] += jnp.sum(jnp.abs(taps), axis=0, keepdims=True)
    c = taps.shape[1] // 2
    col = lax.broadcasted_iota(jnp.int32, (1, taps.shape[1]), 1)
    drop = (pos_i == 0) & (col >= c)
    h_ref[...] = jnp.where(drop, 0.0, taps).astype(h_ref.dtype)


def _hyena_filter(n, w1, b1, w2, b2, w3, b3, wout, freq):
    hid = w1.shape[1]
    c2 = wout.shape[1]
    c = c2 // 2
    tn = _tile(n, 512)
    deltas = jnp.abs(jnp.linspace(MIN_DECAY, MAX_DECAY, c, dtype=F32))
    deltas = jnp.concatenate([deltas, deltas]).reshape(1, c2)
    full = lambda shape: pl.BlockSpec(shape, lambda i: (0, 0))
    return pl.pallas_call(
        functools.partial(_filter_kernel, n),
        grid=(n // tn,),
        in_specs=[full((1, hid)), full((FILT_BANDS, hid)), full((FILT_BANDS, hid)), full((1, hid)),
                  full((hid, hid)), full((1, hid)), full((hid, hid)), full((1, hid)),
                  full((hid, c2)), full((1, hid)), full((1, c2))],
        out_specs=[pl.BlockSpec((tn, c2), lambda i: (i, 0)), full((1, c2))],
        out_shape=[jax.ShapeDtypeStruct((n, c2), F32), jax.ShapeDtypeStruct((1, c2), F32)],
        compiler_params=_params(("arbitrary",)),
        name="hyena_filter",
    )(w1[0:1], w1[1:1 + FILT_BANDS], w1[1 + FILT_BANDS:], b1.reshape(1, hid), w2, b2.reshape(1, hid),
      w3, b3.reshape(1, hid), wout, freq.reshape(1, hid), deltas)


FFT_P = LANES
FFT_R = SUBLANES


def _phase(num, den):
    return (num % den).astype(F32) * (2.0 * math.pi / den)


def _fft_matrices(n):
    p, r = FFT_P, FFT_R
    q = n // p
    n2 = 2 * n
    iq = jnp.arange(q, dtype=jnp.int32)
    eye = jnp.eye(r, dtype=F32)
    alt = lambda v: (1 - 2 * (v & 1)).astype(F32)

    ang = _phase(iq[:, None] * iq[None, :], 2 * q)
    a_re = jnp.cos(ang)
    a_im = (-jnp.sin(ang)).at[0].set(alt(iq))
    core = jnp.stack([a_re, a_im], axis=1)
    m1 = jnp.einsum('fks,rt->fkrst', core, eye).reshape(q * 2 * r, q * r)

    b_c = jnp.cos(ang.T)
    b_s = (-jnp.sin(ang.T)).at[:, 0].set(alt(iq))
    core = jnp.stack([b_c, b_s], axis=2)
    i2 = jnp.einsum('tfk,rs->trfks', core, eye).reshape(q * r, q * 2 * r)

    h = p // 2
    f2 = jnp.arange(h, dtype=jnp.int32)
    s2 = jnp.arange(p, dtype=jnp.int32)
    f_lo = iq[:, None] + 2 * q * f2[None, :]
    f_hi = jnp.where(iq[:, None] == 0, q, 2 * q - iq[:, None]) + 2 * q * f2[None, :]
    freq = jnp.stack([f_lo, f_hi], axis=1)
    phi = _phase(freq[..., None] * s2, n2)
    c, s = jnp.cos(phi), jnp.sin(phi)
    zero = jnp.zeros_like(c[:, 0])
    first = (iq == 0)[:, None, None]
    dcrow = (first & (f2 == 0)[None, :, None])
    nyq = jnp.broadcast_to(alt(s2), c[:, 0].shape)
    on_gr = jnp.stack([jnp.where(first, c[:, 0], c[:, 0]),
                       jnp.where(dcrow, nyq, -s[:, 0]),
                       jnp.where(first, zero, c[:, 1]),
                       jnp.where(first, zero, -s[:, 1])], axis=1)
    on_gi = jnp.stack([jnp.where(first, zero, s[:, 0]),
                       jnp.where(first, zero, c[:, 0]),
                       jnp.where(first, c[:, 1], -s[:, 1]),
                       jnp.where(first, -s[:, 1], -c[:, 1])], axis=1)
    m2 = jnp.concatenate([on_gr, on_gi], axis=-1).reshape(q, 4 * h, 2 * p)
    ct, st = jnp.swapaxes(c, 2, 3), jnp.swapaxes(s, 2, 3)
    zt = jnp.zeros_like(ct[:, 0])
    dccol = (first & (f2 == 0)[None, None, :])
    nyq_t = jnp.broadcast_to(alt(s2)[:, None], ct[:, 0].shape)
    hc = jnp.stack([ct[:, 0], jnp.where(dccol, nyq_t, -st[:, 0]),
                    jnp.where(first, zt, ct[:, 1]), jnp.where(first, zt, -st[:, 1])], axis=2)
    hs = jnp.stack([jnp.where(first, zt, st[:, 0]), jnp.where(first, zt, ct[:, 0]),
                    jnp.where(first, ct[:, 1], -st[:, 1]), jnp.where(first, -st[:, 1], -ct[:, 1])], axis=2)
    i1 = jnp.stack([hc, hs], axis=1).reshape(q, 2 * p, 4 * h)
    return tuple(m.astype(BF16) for m in (m1, m2, i1, i2))


def _fft_stage_a(u_ref, m1_ref, g_ref):
    q, groups, r, tc = u_ref.shape
    m1 = m1_ref[...]
    for g in range(groups):
        blk = u_ref[:, g].reshape(q * r, tc).astype(BF16)
        g_ref[g] = jnp.dot(m1, blk, preferred_element_type=F32).reshape(q, 2, r, tc)


def _fft_stage_b(g_ref, f1, m2):
    groups, _, _, r, tc = g_ref.shape
    z = jnp.concatenate([g_ref[:, f1, 0].reshape(groups * r, tc),
                         g_ref[:, f1, 1].reshape(groups * r, tc)], axis=0).astype(BF16)
    return jnp.dot(m2, z, preferred_element_type=F32)


def _cmul_packed(x, k, first):
    h = x.shape[0] // 4
    xr = (x[0:h], x[2 * h:3 * h])
    xi = (x[h:2 * h], x[3 * h:4 * h])
    kr = (k[0:h], k[2 * h:3 * h])
    ki = (k[h:2 * h], k[3 * h:4 * h])
    real_pair = first & (lax.broadcasted_iota(jnp.int32, (h, 1), 0) == 0)
    lo_re = jnp.where(real_pair, xr[0] * kr[0], xr[0] * kr[0] - xi[0] * ki[0])
    lo_im = jnp.where(real_pair, xi[0] * ki[0], xr[0] * ki[0] + xi[0] * kr[0])
    return jnp.concatenate([lo_re, lo_im, xr[1] * kr[1] - xi[1] * ki[1], xr[1] * ki[1] + xi[1] * kr[1]], axis=0)


def _filter_spec_kernel(n, hf_ref, hb_ref, nf_ref, nb_ref, m1_ref, m2_ref, k_ref, gf_ref, gb_ref):
    qi = pl.program_id(1)
    qb = m2_ref.shape[0]

    @pl.when(qi == 0)
    def _():
        _fft_stage_a(hf_ref, m1_ref, gf_ref)
        _fft_stage_a(hb_ref, m1_ref, gb_ref)

    inv_norm = 1.0 / (nf_ref[...] + nb_ref[...])
    h = m2_ref.shape[1] // 4
    row = lax.broadcasted_iota(jnp.int32, (4 * h, 1), 0)
    for k in range(qb):
        f1 = qi * qb + k
        xf = _fft_stage_b(gf_ref, f1, m2_ref[k])
        xb = _fft_stage_b(gb_ref, f1, m2_ref[k])
        real_pair = (f1 == 0) & ((row == 0) | (row == h))
        imag_row = ((row >= h) & (row < 2 * h)) | (row >= 3 * h)
        spec = jnp.where(imag_row & jnp.logical_not(real_pair), xf - xb, xf + xb)
        k_ref[k] = (spec * jnp.where(real_pair, 0.5 / n, 1.0 / n) * inv_norm).reshape(k_ref.shape[1:])


def _time_view(v, n):
    return v.reshape(v.shape[:-2] + (n // FFT_P, FFT_P // FFT_R, FFT_R, v.shape[-1]))


def _filter_spectrum(taps, norm, mats, c):
    n = taps.shape[0]
    m1, m2, _, _ = mats
    q = n // FFT_P
    h = FFT_P // 2
    tc = _tile(c, 256)
    nc = c // tc
    qb = _tile(q, 8)
    tv = _time_view(taps, n)
    blk = (q, FFT_P // FFT_R, FFT_R, tc)
    return pl.pallas_call(
        functools.partial(_filter_spec_kernel, n),
        grid=(nc, q // qb),
        in_specs=[pl.BlockSpec(blk, lambda j, i: (0, 0, 0, j)),
                  pl.BlockSpec(blk, lambda j, i: (0, 0, 0, j + nc)),
                  pl.BlockSpec((1, tc), lambda j, i: (0, j)),
                  pl.BlockSpec((1, tc), lambda j, i: (0, j + nc)),
                  pl.BlockSpec(m1.shape, lambda j, i: (0, 0)),
                  pl.BlockSpec((qb,) + m2.shape[1:], lambda j, i: (i, 0, 0))],
        out_specs=pl.BlockSpec((qb, 4, h, tc), lambda j, i: (i, 0, 0, j)),
        out_shape=jax.ShapeDtypeStruct((q, 4, h, c), F32),
        scratch_shapes=[pltpu.VMEM((FFT_P // FFT_R, q, 2, FFT_R, tc), F32) for _ in range(2)],
        compiler_params=_params(("parallel", "arbitrary")),
        name="filter_spectrum",
    )(tv, tv, norm, norm, m1, m2)


def _conv_fwd_kernel(w_ref, m1_ref, m2_ref, k_ref, y_ref, g_ref):
    qi = pl.program_id(2)
    qb = m2_ref.shape[0]

    @pl.when(qi == 0)
    def _():
        _fft_stage_a(w_ref.at[0], m1_ref, g_ref)

    for k in range(qb):
        f1 = qi * qb + k
        x = _fft_stage_b(g_ref, f1, m2_ref[k])
        y = _cmul_packed(x, k_ref[k].reshape(x.shape), f1 == 0)
        y_ref[0, k] = y.reshape(y_ref.shape[2:]).astype(y_ref.dtype)


def _conv_forward(wf, kspec, mats):
    b, n, c = wf.shape
    m1, m2, _, _ = mats
    q = n // FFT_P
    h = FFT_P // 2
    tc = _tile(c, 256)
    qb = _tile(q, 8)
    return pl.pallas_call(
        _conv_fwd_kernel,
        grid=(c // tc, b, q // qb),
        in_specs=[pl.BlockSpec((1, q, FFT_P // FFT_R, FFT_R, tc), lambda j, bi, i: (bi, 0, 0, 0, j)),
                  pl.BlockSpec(m1.shape, lambda j, bi, i: (0, 0)),
                  pl.BlockSpec((qb,) + m2.shape[1:], lambda j, bi, i: (i, 0, 0)),
                  pl.BlockSpec((qb, 4, h, tc), lambda j, bi, i: (i, 0, 0, j))],
        out_specs=pl.BlockSpec((1, qb, 4, h, tc), lambda j, bi, i: (bi, i, 0, 0, j)),
        out_shape=jax.ShapeDtypeStruct((b, q, 4, h, c), BF16),
        scratch_shapes=[pltpu.VMEM((FFT_P // FFT_R, q, 2, FFT_R, tc), F32)],
        compiler_params=_params(("parallel", "parallel", "arbitrary")),
        name="conv_forward_fft",
    )(_time_view(wf, n), m1, m2, kspec)


def _conv_inv_kernel(y_ref, i1_ref, i2_ref, wf_ref, x0_ref, bias_ref, o_ref, h_ref):
    qi = pl.program_id(2)
    qb = i1_ref.shape[0]
    groups, q, _, r, tc = h_ref.shape
    for k in range(qb):
        hv = jnp.dot(i1_ref[k], y_ref[0, k].reshape(i1_ref.shape[2], tc), preferred_element_type=F32)
        half = hv.shape[0] // 2
        h_ref[:, qi * qb + k, 0] = hv[:half].reshape(groups, r, tc)
        h_ref[:, qi * qb + k, 1] = hv[half:].reshape(groups, r, tc)

    @pl.when(qi == pl.num_programs(2) - 1)
    def _():
        i2 = i2_ref[...]
        bias = bias_ref[...]
        for g2 in range(groups // 2):
            parts = []
            for g in (2 * g2, 2 * g2 + 1):
                z = jnp.dot(i2, h_ref[g].reshape(q * 2 * r, tc).astype(BF16), preferred_element_type=F32)
                z = z.reshape(q, r, tc)
                parts.append(x0_ref[0, :, g] * (z + wf_ref[0, :, g] * bias))
            o_ref[0, :, g2] = jnp.concatenate(parts, axis=1).astype(o_ref.dtype)


def _conv_inverse(yspec, wf, x0, bias, mats):
    b, n, c = wf.shape
    _, _, i1, i2 = mats
    q = n // FFT_P
    h = FFT_P // 2
    groups = FFT_P // FFT_R
    tc = _tile(c, 256)
    qb = _tile(q, 8)
    tspec = pl.BlockSpec((1, q, groups, FFT_R, tc), lambda j, bi, i: (bi, 0, 0, 0, j))
    out = pl.pallas_call(
        _conv_inv_kernel,
        grid=(c // tc, b, q // qb),
        in_specs=[pl.BlockSpec((1, qb, 4, h, tc), lambda j, bi, i: (bi, i, 0, 0, j)),
                  pl.BlockSpec((qb,) + i1.shape[1:], lambda j, bi, i: (i, 0, 0)),
                  pl.BlockSpec(i2.shape, lambda j, bi, i: (0, 0)),
                  tspec, tspec,
                  pl.BlockSpec((1, tc), lambda j, bi, i: (0, j))],
        out_specs=pl.BlockSpec((1, q, groups // 2, 2 * FFT_R, tc), lambda j, bi, i: (bi, 0, 0, 0, j)),
        out_shape=jax.ShapeDtypeStruct((b, q, groups // 2, 2 * FFT_R, c), BF16),
        scratch_shapes=[pltpu.VMEM((groups, q, 2, FFT_R, tc), F32)],
        compiler_params=_params(("parallel", "parallel", "arbitrary")),
        name="conv_inverse_fft",
    )(yspec, i1, i2, _time_view(wf, n), _time_view(x0, n), bias.reshape(1, c))
    return out.reshape(b, n, c)


SCAN_UNROLL = 4


def _tile_scan(a, b, reverse):
    rows, c = a.shape
    a3 = a.reshape(rows // SUBLANES, SUBLANES, c)
    b3 = b.reshape(rows // SUBLANES, SUBLANES, c)
    sub = lax.broadcasted_iota(jnp.int32, (1, SUBLANES, 1), 1)
    for k in (1, 2, 4):
        shift = SUBLANES - k if reverse else k
        valid = (sub < SUBLANES - k) if reverse else (sub >= k)
        b3 = b3 + a3 * jnp.where(valid, pltpu.roll(b3, shift, axis=1), 0.0)
        a3 = a3 * jnp.where(valid, pltpu.roll(a3, shift, axis=1), 1.0)
    return a3.reshape(rows, c), b3.reshape(rows, c)


def _lru_kernel(group, has_gate, *refs):
    if has_gate:
        (x_ref, gate_ref, cw_ref, cb_ref, wa_ref, ba_ref, wx_ref, bx_ref, lam_ref, h0_ref,
         y_ref, ht_ref, af_ref, bf_ref, ab_ref, bb_ref) = refs
    else:
        (x_ref, cw_ref, cb_ref, wa_ref, ba_ref, wx_ref, bx_ref, lam_ref, h0_ref,
         ht_ref, af_ref, bf_ref, ab_ref, bb_ref) = refs
    t_len = x_ref.shape[1]
    chunk = _tile(t_len, 512)
    a_refs = (af_ref, ab_ref)
    b_refs = (bf_ref, bb_ref)

    def coeffs(ci, carry):
        r0 = pl.multiple_of(ci * chunk, chunk)
        xc = _short_conv(x_ref[0, pl.ds(r0, chunk), :], cw_ref, cb_ref[...], LRU_CONV_LEFT, group)
        xcb = xc.astype(BF16)
        half_xc = 0.5 * xc
        for d in range(2):
            ta = jnp.tanh(jnp.dot(xcb, wa_ref[d, 0], preferred_element_type=F32) + ba_ref[d])
            ti = jnp.tanh(jnp.dot(xcb, wx_ref[d, 0], preferred_element_type=F32) + bx_ref[d])
            lam = lam_ref[d]
            softplus_neg = jnp.maximum(-lam, 0.0) + jnp.log(1.0 + jnp.exp(-jnp.abs(lam)))
            c1 = (-0.5 * LRU_C) * softplus_neg
            a = jnp.exp(c1 * ta + c1)
            a_tile, b_tile = _tile_scan(a, jnp.sqrt(1.0 - a * a) * (half_xc * ti + half_xc), d == 1)
            a_refs[d][pl.ds(r0, chunk), :] = a_tile
            b_refs[d][pl.ds(r0, chunk), :] = b_tile
        return carry

    lax.fori_loop(0, t_len // chunk, coeffs, 0)

    n_tiles = t_len // SUBLANES
    unroll = _tile(n_tiles, SCAN_UNROLL)

    def scan(i, carry):
        hf, hb = carry
        for u in range(unroll):
            kf = i * unroll + u
            rf = pl.multiple_of(kf * SUBLANES, SUBLANES)
            tile_f = af_ref[pl.ds(rf, SUBLANES), :] * hf + bf_ref[pl.ds(rf, SUBLANES), :]
            bf_ref[pl.ds(rf, SUBLANES), :] = tile_f
            hf = tile_f[SUBLANES - 1:SUBLANES, :]
            rb = pl.multiple_of((n_tiles - 1 - kf) * SUBLANES, SUBLANES)
            tile_b = ab_ref[pl.ds(rb, SUBLANES), :] * hb + bb_ref[pl.ds(rb, SUBLANES), :]
            bb_ref[pl.ds(rb, SUBLANES), :] = tile_b
            hb = tile_b[0:1, :]
        return hf, hb

    hf, hb = lax.fori_loop(0, n_tiles // unroll, scan, (h0_ref[0, 0:1, :], h0_ref[0, 1:2, :]))
    ht_ref[0, 0:1, :] = hf
    ht_ref[0, 1:2, :] = hb

    if has_gate:
        def emit(ci, carry):
            r0 = pl.multiple_of(ci * chunk, chunk)
            hs = bf_ref[pl.ds(r0, chunk), :] + bb_ref[pl.ds(r0, chunk), :]
            y_ref[0, pl.ds(r0, chunk), :] = (jax.nn.gelu(gate_ref[0, pl.ds(r0, chunk), :]) * hs).astype(y_ref.dtype)
            return carry

        lax.fori_loop(0, t_len // chunk, emit, 0)


def _block_diag(w, heads_per_tile):
    d2, h, hd, _ = w.shape
    w = w.reshape(d2, h // heads_per_tile, heads_per_tile, hd, hd)
    eye = jnp.eye(heads_per_tile, dtype=w.dtype)
    bd = jnp.einsum('dghij,hq->dghiqj', w, eye)
    return bd.reshape(d2, h // heads_per_tile, heads_per_tile * hd, heads_per_tile * hd).astype(BF16)


def _rglru(xsrc, x_col, gsrc, g_col, conv_w, conv_b, wa, ba, wx, bx, lam, h0, group):
    b, t, _ = xsrc.shape
    c = conv_w.shape[1]
    hd = c // LRU_HEADS
    tc = min(c, max(hd, 256))
    hp = tc // hd
    nt = c // tc
    has_gate = gsrc is not None
    wa_bd = _block_diag(0.5 * wa, hp)
    wx_bd = _block_diag(0.5 * wx, hp)
    ba, bx = 0.5 * ba, 0.5 * bx
    xo, go = x_col // tc, (g_col // tc if has_gate else 0)
    vec = lambda: pl.BlockSpec((2, 1, tc), lambda i, j: (0, 0, j))
    mat = lambda: pl.BlockSpec((2, 1, tc, tc), lambda i, j: (0, j, 0, 0))
    in_specs = [pl.BlockSpec((1, t, tc), lambda i, j: (i, 0, j + xo))]
    args = [xsrc]
    if has_gate:
        in_specs.append(pl.BlockSpec((1, t, tc), lambda i, j: (i, 0, j + go)))
        args.append(gsrc)
    in_specs += [pl.BlockSpec((conv_w.shape[0], tc), lambda i, j: (0, j)),
                 pl.BlockSpec((1, tc), lambda i, j: (0, j)),
                 mat(), vec(), mat(), vec(), vec(),
                 pl.BlockSpec((1, 2, tc), lambda i, j: (i, 0, j))]
    args += [conv_w, conv_b.reshape(1, c), wa_bd, ba.reshape(2, 1, c), wx_bd, bx.reshape(2, 1, c),
             lam.reshape(2, 1, c), h0]
    ht_spec = pl.BlockSpec((1, 2, tc), lambda i, j: (i, 0, j))
    ht_shape = jax.ShapeDtypeStruct((b, 2, c), F32)
    if has_gate:
        out_specs = [pl.BlockSpec((1, t, tc), lambda i, j: (i, 0, j)), ht_spec]
        out_shape = [jax.ShapeDtypeStruct((b, t, c), BF16), ht_shape]
    else:
        out_specs = [ht_spec]
        out_shape = [ht_shape]
    return pl.pallas_call(
        functools.partial(_lru_kernel, group, has_gate),
        grid=(b, nt),
        in_specs=in_specs,
        out_specs=out_specs,
        out_shape=out_shape,
        scratch_shapes=[pltpu.VMEM((t, tc), F32) for _ in range(4)],
        compiler_params=_params(("parallel", "parallel")),
        name="rglru" if has_gate else "rglru_context",
    )(*args)


def _out_proj_kernel(a1_ref, a2_ref, w1_ref, w2_ref, b_ref, x_ref, g_ref, o_ref):
    acc = jnp.dot(a1_ref[...], w1_ref[...], preferred_element_type=F32)
    acc = acc + jnp.dot(a2_ref[...], w2_ref[...], preferred_element_type=F32)
    o_ref[...] = x_ref[...] + g_ref[0] * (acc + b_ref[...])


def _out_projection(a1, a2, w, bias, x, gate, t):
    m, k1 = a1.shape
    n = w.shape[1]
    tm = _tile(t, 1024)
    tn = _tile(n, 512)
    per_b = t // tm
    k1_blocks = 1
    return pl.pallas_call(
        _out_proj_kernel,
        grid=(m // tm, n // tn),
        in_specs=[pl.BlockSpec((tm, k1), lambda i, j: (i, 0)),
                  pl.BlockSpec((tm, k1), lambda i, j: (i, 0)),
                  pl.BlockSpec((k1, tn), lambda i, j: (0, j)),
                  pl.BlockSpec((k1, tn), lambda i, j: (k1_blocks, j)),
                  pl.BlockSpec((1, tn), lambda i, j: (0, j)),
                  pl.BlockSpec((tm, tn), lambda i, j: (i, j)),
                  pl.BlockSpec((1, 1, tn), lambda i, j: (i // per_b, 0, j))],
        out_specs=pl.BlockSpec((tm, tn), lambda i, j: (i, j)),
        out_shape=jax.ShapeDtypeStruct((m, n), F32),
        compiler_params=_params(("parallel", "arbitrary")),
        name="out_projection",
    )(a1, a2, w, w, bias.reshape(1, n), x, gate)


def _router_kernel(x_ref, g_ref, sh_ref, sc_ref, wr_ref, h_ref, aff_ref):
    h = _rms_mod(x_ref[0], g_ref[...], sh_ref[0], sc_ref[0])
    h_ref[...] = h
    h_hi, h_lo = _split_bf16(h)
    w_hi, w_lo = _split_bf16(wr_ref[...])
    nt = (((1,), (1,)), ((), ()))
    logits = (lax.dot_general(w_hi, h_hi, nt, preferred_element_type=F32)
              + lax.dot_general(w_lo, h_hi, nt, preferred_element_type=F32)
              + lax.dot_general(w_hi, h_lo, nt, preferred_element_type=F32))
    z = jnp.exp(logits - jnp.max(logits, axis=0, keepdims=True))
    aff_ref[0] = z / jnp.sum(z, axis=0, keepdims=True)


def _router(x1, g, shift, scale, w_router):
    b, t, d = x1.shape
    e = w_router.shape[1]
    tt = _tile(t, 256)
    per_b = t // tt
    mod_map = lambda i, j: (i, 0, 0)
    return pl.pallas_call(
        _router_kernel,
        grid=(b, per_b),
        in_specs=[pl.BlockSpec((1, tt, d), lambda i, j: (i, j, 0)),
                  pl.BlockSpec((1, d), lambda i, j: (0, 0)),
                  pl.BlockSpec((1, 1, d), mod_map),
                  pl.BlockSpec((1, 1, d), mod_map),
                  pl.BlockSpec((e, d), lambda i, j: (0, 0))],
        out_specs=[pl.BlockSpec((tt, d), lambda i, j: (i * per_b + j, 0)),
                   pl.BlockSpec((1, e, tt), lambda i, j: (i, 0, j))],
        out_shape=[jax.ShapeDtypeStruct((b * t, d), F32),
                   jax.ShapeDtypeStruct((b, e, t), F32)],
        compiler_params=_params(("parallel", "parallel")),
        name="router",
    )(x1, g.reshape(1, d), shift, scale, w_router.T)


def _prefix_count(mask_ref, out_ref):
    e, t = mask_ref.shape
    blk = min(t, LANES)
    tri = (lax.broadcasted_iota(jnp.int32, (blk, blk), 0)
           < lax.broadcasted_iota(jnp.int32, (blk, blk), 1)).astype(BF16)
    carry = jnp.zeros((e, 1), F32)
    for k in range(t // blk):
        m = mask_ref[:, k * blk:(k + 1) * blk]
        out_ref[:, k * blk:(k + 1) * blk] = jnp.dot(m.astype(BF16), tri, preferred_element_type=F32) + carry
        carry = carry + jnp.sum(m, axis=1, keepdims=True)
    return carry


def _topk_kernel(cap, aff_ref, idx_ref, gate_ref, dest_ref, span_ref, mask_ref, pos_ref, cnt_ref, q_ref):
    a = aff_ref[0]
    e, t = a.shape
    min_normal = 0x00800000

    def refine(i, thr):
        cand = thr | jnp.left_shift(jnp.int32(1), 30 - i)
        cnt = jnp.sum(jnp.where(a >= pltpu.bitcast(cand, F32), 1.0, 0.0), axis=1, keepdims=True)
        return jnp.where((cnt >= cap) & (cand >= min_normal), cand, thr)

    thr = lax.fori_loop(0, 31, refine, jnp.zeros((e, 1), jnp.int32))
    above = a >= pltpu.bitcast(jnp.maximum(thr + 1, min_normal), F32)
    tied = (a >= pltpu.bitcast(thr, F32)) & jnp.logical_not(above)
    need = cap - jnp.sum(jnp.where(above, 1.0, 0.0), axis=1, keepdims=True)
    mask_ref[...] = jnp.where(tied, 1.0, 0.0)
    _prefix_count(mask_ref, pos_ref)
    sel = above | (tied & (pos_ref[...] < need))
    mask_ref[...] = jnp.where(sel, 1.0, 0.0)
    _prefix_count(mask_ref, pos_ref)

    mask = mask_ref[...]
    cnt = jnp.sum(mask, axis=0, keepdims=True)
    cnt_ref[...] = jnp.broadcast_to(cnt, cnt_ref.shape)
    _prefix_count(cnt_ref, q_ref.at[0:SUBLANES])
    first = q_ref[0:1, :]
    srow = lax.broadcasted_iota(jnp.int32, (SUBLANES, 1), 0)
    span_ref[0] = jnp.where(srow == 0, first, jnp.where(srow == 1, first + cnt, 0.0))
    lower = (lax.broadcasted_iota(jnp.int32, (e, e), 1)
             < lax.broadcasted_iota(jnp.int32, (e, e), 0)).astype(BF16)
    q_ref[...] = first + jnp.dot(lower, mask.astype(BF16), preferred_element_type=F32)

    tok = lax.broadcasted_iota(jnp.int32, (1, t), 1)
    tok_hi = (tok >> 6).astype(F32)
    tok_lo = (tok & 63).astype(F32)
    slot = lax.broadcasted_iota(jnp.int32, (cap, 1), 0).astype(F32)
    vrow = lax.broadcasted_iota(jnp.int32, (SUBLANES, 1), 0)
    dest_base = pl.program_id(0) * (e * cap)

    def compact(ei, carry):
        g = aff_ref[0, pl.ds(ei, 1), :]
        g_hi = g.astype(BF16).astype(F32)
        g_mid = (g - g_hi).astype(BF16).astype(F32)
        g_lo = g - g_hi - g_mid
        q = q_ref[pl.ds(ei, 1), :]
        q_hi = jnp.floor(q * (1.0 / 64.0))
        q_lo = q - 64.0 * q_hi
        vals = jnp.where(vrow == 0, tok_hi, jnp.where(vrow == 1, tok_lo, jnp.where(
            vrow == 2, g_hi, jnp.where(vrow == 3, g_mid, jnp.where(vrow == 4, g_lo, jnp.where(
                vrow == 5, q_hi, jnp.where(vrow == 6, q_lo, 0.0))))))).astype(BF16)
        hit = (pos_ref[pl.ds(ei, 1), :] == slot) & (mask_ref[pl.ds(ei, 1), :] > 0.0)
        onehot = jnp.where(hit, 1.0, 0.0).astype(BF16)
        res = lax.dot_general(vals, onehot, (((1,), (1,)), ((), ())), preferred_element_type=F32)
        idx_ref[0, pl.ds(ei, 1), :] = (res[0:1] * 64.0 + res[1:2]).astype(jnp.int32)
        gate_ref[0, pl.ds(ei, 1), :] = res[2:3] + res[3:4] + res[4:5]
        dest_ref[0, pl.ds(ei, 1), :] = (res[5:6] * 64.0 + res[6:7]).astype(jnp.int32) + dest_base
        return carry

    lax.fori_loop(0, e, compact, 0)


def _select_tokens(aff, cap):
    b, e, t = aff.shape
    assert t <= 64 * 256 and e * cap <= 64 * 256 and e >= SUBLANES
    spec = pl.BlockSpec((1, e, cap), lambda i: (i, 0, 0))
    return pl.pallas_call(
        functools.partial(_topk_kernel, cap),
        grid=(b,),
        in_specs=[pl.BlockSpec((1, e, t), lambda i: (i, 0, 0))],
        out_specs=[spec, spec, spec, pl.BlockSpec((1, SUBLANES, t), lambda i: (i, 0, 0))],
        out_shape=[jax.ShapeDtypeStruct((b, e, cap), jnp.int32), jax.ShapeDtypeStruct((b, e, cap), F32),
                   jax.ShapeDtypeStruct((b, e, cap), jnp.int32), jax.ShapeDtypeStruct((b, SUBLANES, t), F32)],
        scratch_shapes=[pltpu.VMEM((e, t), F32), pltpu.VMEM((e, t), F32),
                        pltpu.VMEM((SUBLANES, t), F32), pltpu.VMEM((e, t), F32)],
        compiler_params=_params(("parallel",)),
        name="select_tokens",
    )(aff)


def _gather_kernel(rows_ref, h_hbm, o_ref, buf, sem):
    i = pl.program_id(0)
    r_blk = buf.shape[1]

    def issue_block(blk, slot):
        def issue(r, carry):
            row = rows_ref[blk * r_blk + r]
            pltpu.make_async_copy(h_hbm.at[pl.ds(row, 1), :], buf.at[slot, pl.ds(r, 1), :], sem.at[slot]).start()
            return carry

        lax.fori_loop(0, r_blk, issue, 0, unroll=8)

    @pl.when(i == 0)
    def _():
        issue_block(0, 0)

    @pl.when(i + 1 < pl.num_programs(0))
    def _():
        issue_block(i + 1, (i + 1) & 1)

    slot = i & 1
    pltpu.make_async_copy(h_hbm.at[pl.ds(0, r_blk), :], buf.at[slot], sem.at[slot]).wait()
    o_ref[...] = buf[slot].astype(o_ref.dtype)


def _gather_rows(rows, h):
    n_rows = rows.shape[0]
    d = h.shape[1]
    r_blk = _tile(n_rows, 256)
    grid_spec = pltpu.PrefetchScalarGridSpec(
        num_scalar_prefetch=1,
        grid=(n_rows // r_blk,),
        in_specs=[pl.BlockSpec(memory_space=pl.ANY)],
        out_specs=pl.BlockSpec((r_blk, d), lambda i, rows_ref: (i, 0)),
        scratch_shapes=[pltpu.VMEM((2, r_blk, d), F32), pltpu.SemaphoreType.DMA((2,))],
    )
    return pl.pallas_call(
        _gather_kernel,
        grid_spec=grid_spec,
        out_shape=jax.ShapeDtypeStruct((n_rows, d), BF16),
        compiler_params=_params(("arbitrary",)),
        name="gather_rows",
    )(rows, h)


def _expert_up_kernel(xs_ref, wg_ref, wu_ref, h_ref):
    xs = xs_ref[...]
    g = jnp.dot(xs, wg_ref[0].astype(BF16), preferred_element_type=F32)
    u = jnp.dot(xs, wu_ref[0].astype(BF16), preferred_element_type=F32)
    h_ref[...] = (g * jax.nn.sigmoid(g) * u).astype(h_ref.dtype)


def _expert_up(xs, w_gate, w_up, rows_per_expert):
    m, d = xs.shape
    f = w_gate.shape[2]
    tm = _tile(rows_per_expert, 1024)
    per_e = rows_per_expert // tm
    tf = _tile(f, 256)
    wspec = pl.BlockSpec((1, d, tf), lambda i, j: (i // per_e, 0, j))
    return pl.pallas_call(
        _expert_up_kernel,
        grid=(m // tm, f // tf),
        in_specs=[pl.BlockSpec((tm, d), lambda i, j: (i, 0)), wspec, wspec],
        out_specs=pl.BlockSpec((tm, tf), lambda i, j: (i, j)),
        out_shape=jax.ShapeDtypeStruct((m, f), BF16),
        compiler_params=_params(("parallel", "arbitrary")),
        name="expert_up",
    )(xs, w_gate, w_up)


DOWN_COL_TILE = 512


def _expert_down_kernel(dest_ref, h_ref, wd_ref, g_ref, ys_hbm, ybuf, sem):
    i = pl.program_id(0)
    j = pl.program_id(1)
    n_i = pl.num_programs(0)
    n_j = pl.num_programs(1)
    _, tm, dw = ybuf.shape
    hw = wd_ref.shape[2] // 2
    slot = i & 1

    def wait_scatter(s):
        pltpu.make_async_copy(ybuf.at[s], ys_hbm.at[pl.ds(0, tm), :], sem.at[s]).wait()

    @pl.when((j == 0) & (i >= 2))
    def _():
        wait_scatter(slot)

    y = jnp.dot(h_ref[...], wd_ref[0].astype(BF16), preferred_element_type=F32) * g_ref[...]
    packed = pltpu.pack_elementwise([y[:, :hw], y[:, hw:]], packed_dtype=BF16)
    for jj in range(dw // hw):
        @pl.when(j == jj)
        def _(jj=jj):
            ybuf[slot, :, jj * hw:(jj + 1) * hw] = packed

    @pl.when(j == n_j - 1)
    def _():
        def issue(r, carry):
            dst = dest_ref[i * tm + r]
            pltpu.make_async_copy(ybuf.at[slot, pl.ds(r, 1), :], ys_hbm.at[pl.ds(dst, 1), :], sem.at[slot]).start()
            return carry

        lax.fori_loop(0, tm, issue, 0, unroll=8)

        @pl.when(i == n_i - 1)
        def _():
            @pl.when(i >= 1)
            def _():
                wait_scatter(1 - slot)
            wait_scatter(slot)


def _expert_down(dest, h, w_down, gates, rows_per_expert):
    m, f = h.shape
    d = w_down.shape[2]
    tm = _tile(rows_per_expert, 1024)
    per_e = rows_per_expert // tm
    td = _tile(d, DOWN_COL_TILE)
    grid_spec = pltpu.PrefetchScalarGridSpec(
        num_scalar_prefetch=1,
        grid=(m // tm, d // td),
        in_specs=[pl.BlockSpec((tm, f), lambda i, j, dest_ref: (i, 0)),
                  pl.BlockSpec((1, f, td), lambda i, j, dest_ref: (i // per_e, 0, j)),
                  pl.BlockSpec((tm, 1), lambda i, j, dest_ref: (i, 0))],
        out_specs=pl.BlockSpec(memory_space=pl.ANY),
        scratch_shapes=[pltpu.VMEM((2, tm, d // 2), jnp.uint32), pltpu.SemaphoreType.DMA((2,))],
    )
    return pl.pallas_call(
        _expert_down_kernel,
        grid_spec=grid_spec,
        out_shape=jax.ShapeDtypeStruct((m, d // 2), jnp.uint32),
        compiler_params=_params(("arbitrary", "arbitrary")),
        name="expert_down",
    )(dest, h, w_down, gates)


def _combine_kernel(slots_per_batch, hw, tb_ref, span_ref, x_ref, gt_ref, g_ref, ys_hbm, o_ref,
                    buf, sem, acc_ref):
    bi = pl.program_id(0)
    i = pl.program_id(1)
    n_t = pl.num_programs(1)
    kc = buf.shape[1]
    total = ys_hbm.shape[0]
    c_lo = tb_ref[bi * (n_t + 1) + i]
    c_hi = tb_ref[bi * (n_t + 1) + i + 1]
    k_first = (c_lo >> 3) << 3
    n_chunks = (c_hi - k_first + kc - 1) // kc

    def chunk_start(c):
        return pl.multiple_of(jnp.minimum(k_first + c * kc, total - kc), SUBLANES)

    def copy(c, slot):
        return pltpu.make_async_copy(ys_hbm.at[pl.ds(chunk_start(c), kc), :], buf.at[slot], sem.at[slot])

    @pl.when(n_chunks > 0)
    def _():
        copy(0, 0).start()

    acc_ref[...] = jnp.zeros_like(acc_ref)
    base = (bi * slots_per_batch).astype(F32)
    first = span_ref[0][:, 0:1] + base
    last = span_ref[0][:, 1:2] + base

    def body(c, carry):
        slot = c & 1
        copy(c, slot).wait()

        @pl.when(c + 1 < n_chunks)
        def _():
            copy(c + 1, 1 - slot).start()

        k = chunk_start(c) + lax.broadcasted_iota(jnp.int32, (1, kc), 1)
        k = jnp.where(k >= k_first + c * kc, k, -1).astype(F32)
        onehot = jnp.where((k >= first) & (k < last), 1.0, 0.0).astype(BF16)
        words = buf[slot]
        lo = pltpu.unpack_elementwise(words, index=0, packed_dtype=BF16, unpacked_dtype=F32).astype(BF16)
        hi = pltpu.unpack_elementwise(words, index=1, packed_dtype=BF16, unpacked_dtype=F32).astype(BF16)
        acc_ref[0] += jnp.dot(onehot, lo, preferred_element_type=F32)
        acc_ref[1] += jnp.dot(onehot, hi, preferred_element_type=F32)
        return carry

    lax.fori_loop(0, n_chunks, body, 0)
    lo, hi = acc_ref[0], acc_ref[1]
    moe = jnp.concatenate([part[:, j * hw:(j + 1) * hw] for j in range(lo.shape[1] // hw)
                           for part in (lo, hi)], axis=1)
    x = x_ref[0] + gt_ref[0] * moe
    o_ref[0] = x * lax.rsqrt(jnp.mean(x * x, axis=-1, keepdims=True) + EPS) * g_ref[...]


COMBINE_ROWS = 512
COMBINE_CHUNK = 256


def _combine(tile_bounds, span, ys, x1, gate, g_final, slots_per_batch):
    b, t, d = x1.shape
    tt = _tile(t, COMBINE_ROWS)
    kc = min(COMBINE_CHUNK, ys.shape[0])
    grid_spec = pltpu.PrefetchScalarGridSpec(
        num_scalar_prefetch=1,
        grid=(b, t // tt),
        in_specs=[pl.BlockSpec((1, tt, SUBLANES), lambda bi, i, tb: (bi, i, 0)),
                  pl.BlockSpec((1, tt, d), lambda bi, i, tb: (bi, i, 0)),
                  pl.BlockSpec((1, 1, d), lambda bi, i, tb: (bi, 0, 0)),
                  pl.BlockSpec((1, d), lambda bi, i, tb: (0, 0)),
                  pl.BlockSpec(memory_space=pl.ANY)],
        out_specs=pl.BlockSpec((1, tt, d), lambda bi, i, tb: (bi, i, 0)),
        scratch_shapes=[pltpu.VMEM((2, kc, d // 2), jnp.uint32), pltpu.SemaphoreType.DMA((2,)),
                        pltpu.VMEM((2, tt, d // 2), F32)],
    )
    return pl.pallas_call(
        functools.partial(_combine_kernel, slots_per_batch, _tile(d, DOWN_COL_TILE) // 2),
        grid_spec=grid_spec,
        out_shape=jax.ShapeDtypeStruct((b, t, d), F32),
        compiler_params=_params(("arbitrary", "arbitrary")),
        name="combine",
    )(tile_bounds, span, x1, gate, g_final.reshape(1, d), ys)


def kernel(x, c, ctx, c_ctx, w_mod, b_mod, g_mix, g_ffn, w_in, b_in, hy_conv_w, hy_conv_b, hy_f_w1, hy_f_b1, hy_f_w2, hy_f_b2, hy_f_w3, hy_f_b3, hy_f_wout, hy_f_freq, hy_bias, lru_conv_w, lru_conv_b, lru_wa, lru_ba, lru_wx, lru_bx, lru_lambda, w_out, b_out, w_router, w_exp_gate, w_exp_up, w_exp_down, g_final):
    bsz, n_lat, d = x.shape
    n_ctx = ctx.shape[1]
    depth = w_mod.shape[0]
    c_hy = hy_bias.shape[1]
    c_lru = lru_conv_b.shape[1]
    in_gate = 3 * c_hy
    in_x = in_gate + c_lru
    n_exp = w_router.shape[2]
    cap = CAPACITY_FACTOR * n_lat // n_exp
    assert depth == 1, "context residual updates are only needed for depth > 1"
    assert n_lat % GRID_W == 0 and n_ctx & (n_ctx - 1) == 0

    rows = -(-(bsz + 1) // SUBLANES) * SUBLANES
    c_all = jnp.zeros((rows, d), F32).at[:bsz].set(c).at[bsz].set(c_ctx)
    mats = _fft_matrices(n_lat)

    l = 0
    mod = _modulation(c_all, w_mod[l], b_mod[l])
    mx = mod[:bsz].reshape(bsz, 1, N_MOD, d)
    sh1, sc1, gt1, sh2, sc2, gt2 = (mx[:, :, k] for k in range(N_MOD))
    mc = mod[bsz].reshape(1, 1, N_MOD, d)
    csh1, csc1 = mc[:, :, 0], mc[:, :, 1]

    w_in_b = w_in[l].astype(BF16)
    w_out_b = w_out[l].astype(BF16)

    hx = _norm_mod(x, g_mix[l], sh1, sc1, True)
    hc = _norm_mod(ctx, g_mix[l], csh1, csc1, False)
    hx = hx.reshape(bsz * n_lat, d)
    x0, wf = _projection_hyena(hx, w_in_b, b_in[l], hy_conv_w[l], hy_conv_b[l], c_hy)
    x0, wf = x0.reshape(bsz, n_lat, c_hy), wf.reshape(bsz, n_lat, c_hy)
    p_lru = _projection(hx, w_in_b, b_in[l], in_gate, 2 * c_lru, F32)
    p_lru = p_lru.reshape(bsz, n_lat, 2 * c_lru)
    pc_lx = _projection(hc.reshape(bsz * n_ctx, d), w_in_b, b_in[l], in_x, c_lru, F32)
    pc_lx = pc_lx.reshape(bsz, n_ctx, c_lru)

    lru_args = (lru_conv_w[l], lru_conv_b[l], lru_wa[l], lru_ba[l], lru_wx[l], lru_bx[l], lru_lambda[l])
    (h_ctx,) = _rglru(pc_lx, 0, None, 0, *lru_args, jnp.zeros((bsz, 2, c_lru), F32), n_ctx)
    y_lru, _ = _rglru(p_lru, c_lru, p_lru, 0, *lru_args, h_ctx, GRID_W)

    taps, norm = _hyena_filter(n_lat, hy_f_w1[l], hy_f_b1[l], hy_f_w2[l], hy_f_b2[l], hy_f_w3[l],
                               hy_f_b3[l], hy_f_wout[l], hy_f_freq[l])
    kspec = _filter_spectrum(taps, norm, mats, c_hy)
    yspec = _conv_forward(wf, kspec, mats)
    y_hy = _conv_inverse(yspec, wf, x0, hy_bias[l], mats)

    x1 = _out_projection(y_hy.reshape(bsz * n_lat, c_hy), y_lru.reshape(bsz * n_lat, c_lru), w_out_b,
                         b_out[l], x.reshape(bsz * n_lat, d), gt1, n_lat)
    x1 = x1.reshape(bsz, n_lat, d)

    h2, aff = _router(x1, g_ffn[l], sh2, sc2, w_router[l])
    idx, gates, dest, span = _select_tokens(aff, cap)
    rows_g = (idx + (jnp.arange(bsz, dtype=jnp.int32) * n_lat)[:, None, None])
    expert_major = lambda v: jnp.swapaxes(v, 0, 1).reshape(-1)
    xs = _gather_rows(expert_major(rows_g), h2)
    hmid = _expert_up(xs, w_exp_gate[l], w_exp_up[l], bsz * cap)
    ys = _expert_down(expert_major(dest), hmid, w_exp_down[l], expert_major(gates).reshape(-1, 1), bsz * cap)
    slots = n_exp * cap
    tt = _tile(n_lat, COMBINE_ROWS)
    base = (jnp.arange(bsz, dtype=jnp.int32) * slots)[:, None]
    bounds = jnp.concatenate([span[:, 0, ::tt].astype(jnp.int32) + base, base + slots], axis=1).reshape(-1)
    return _combine(bounds, jnp.swapaxes(span, 1, 2), ys, x1, gt2, g_final, slots)
```

```python
import functools
import math

import jax
import jax.numpy as jnp
from jax import lax
from jax.experimental import pallas as pl
from jax.experimental.pallas import tpu as pltpu

F32 = jnp.float32
BF16 = jnp.bfloat16

GRID_W = 64
HY_SHORT_LEFT = 1
FILT_BANDS = 16
DECAY_TARGET = 1e-2
MIN_DECAY = math.log(DECAY_TARGET) / 1.5
MAX_DECAY = math.log(DECAY_TARGET) / 0.3
LRU_HEADS = 16
LRU_CONV_LEFT = 2
LRU_C = 8.0
CAPACITY_FACTOR = 2
N_MOD = 6
EPS = 1e-6

LANES = 128
SUBLANES = 8
VMEM_LIMIT = 56 * 2 ** 20


def _params(sem):
    return pltpu.CompilerParams(dimension_semantics=sem, vmem_limit_bytes=VMEM_LIMIT)


def _tile(n, pref):
    t = min(n, pref)
    while n % t:
        t //= 2
    return t


def _split_bf16(v):
    hi = v.astype(BF16)
    lo = (v - hi.astype(F32)).astype(BF16)
    return hi, lo


def _mod_kernel(c_ref, w_ref, b_ref, o_ref):
    c = c_ref[...]
    s = c * jax.nn.sigmoid(c)
    s_hi, s_lo = _split_bf16(s)
    w_hi, w_lo = _split_bf16(w_ref[...])
    rows = c.shape[0]
    r = jnp.dot(jnp.concatenate([s_hi, s_lo], axis=0), w_hi, preferred_element_type=F32)
    acc = r[:rows] + r[rows:] + jnp.dot(s_hi, w_lo, preferred_element_type=F32)
    o_ref[...] = acc + b_ref[...]


def _modulation(c_all, w_mod, b_mod):
    rows, d = c_all.shape
    n = w_mod.shape[1]
    tn = _tile(n, 256)
    return pl.pallas_call(
        _mod_kernel,
        grid=(n // tn,),
        in_specs=[pl.BlockSpec((rows, d), lambda j: (0, 0)),
                  pl.BlockSpec((d, tn), lambda j: (0, j)),
                  pl.BlockSpec((1, tn), lambda j: (0, j))],
        out_specs=pl.BlockSpec((rows, tn), lambda j: (0, j)),
        out_shape=jax.ShapeDtypeStruct((rows, n), F32),
        compiler_params=_params(("parallel",)),
        name="modulation",
    )(c_all, w_mod, b_mod.reshape(1, n))


def _rms_mod(x, g, shift, scale):
    y = x * lax.rsqrt(jnp.mean(x * x, axis=-1, keepdims=True) + EPS) * g
    return y * (1.0 + scale) + shift


def _norm_kernel(x_ref, g_ref, sh_ref, sc_ref, o_ref):
    o_ref[0] = _rms_mod(x_ref[0], g_ref[...], sh_ref[0], sc_ref[0]).astype(o_ref.dtype)


def _norm_mod(x, g, shift, scale, per_batch):
    b, t, d = x.shape
    tt = _tile(t, 512)
    mod_map = (lambda i, j: (i, 0, 0)) if per_batch else (lambda i, j: (0, 0, 0))
    return pl.pallas_call(
        _norm_kernel,
        grid=(b, t // tt),
        in_specs=[pl.BlockSpec((1, tt, d), lambda i, j: (i, j, 0)),
                  pl.BlockSpec((1, d), lambda i, j: (0, 0)),
                  pl.BlockSpec((1, 1, d), mod_map),
                  pl.BlockSpec((1, 1, d), mod_map)],
        out_specs=pl.BlockSpec((1, tt, d), lambda i, j: (i, j, 0)),
        out_shape=jax.ShapeDtypeStruct((b, t, d), BF16),
        compiler_params=_params(("parallel", "parallel")),
        name="adaln_norm",
    )(x, g.reshape(1, d), shift, scale)


def _proj_kernel(a_ref, w_ref, b_ref, o_ref):
    acc = jnp.dot(a_ref[...], w_ref[...], preferred_element_type=F32)
    o_ref[...] = (acc + b_ref[...]).astype(o_ref.dtype)


def _projection(a, w, bias, col_start, n_cols, out_dtype):
    m, k = a.shape
    tm = _tile(m, 1024)
    tn = _tile(n_cols, 512)
    off = col_start // tn
    return pl.pallas_call(
        _proj_kernel,
        grid=(m // tm, n_cols // tn),
        in_specs=[pl.BlockSpec((tm, k), lambda i, j: (i, 0)),
                  pl.BlockSpec((k, tn), lambda i, j: (0, j + off)),
                  pl.BlockSpec((1, tn), lambda i, j: (0, j + off))],
        out_specs=pl.BlockSpec((tm, tn), lambda i, j: (i, j)),
        out_shape=jax.ShapeDtypeStruct((m, n_cols), out_dtype),
        compiler_params=_params(("parallel", "arbitrary")),
        name="projection",
    )(a, w, bias.reshape(1, -1))


def _short_conv(x, w_ref, bias, left, group):
    rows = x.shape[0]
    pos = lax.broadcasted_iota(jnp.int32, (rows, 1), 0) & (group - 1)
    y = bias + w_ref[left:left + 1, :] * x
    for k in range(w_ref.shape[0]):
        off = k - left
        if off == 0:
            continue
        shifted = pltpu.roll(x, (-off) % rows, axis=0)
        valid = (pos + off >= 0) & (pos + off < group)
        y = y + w_ref[k:k + 1, :] * jnp.where(valid, shifted, 0.0)
    return y


def _proj_hyena_kernel(a_ref, w0_ref, w1_ref, w2_ref, b0_ref, b1_ref, b2_ref,
                       cw0_ref, cw1_ref, cw2_ref, cb0_ref, cb1_ref, cb2_ref, x0_ref, wf_ref):
    a = a_ref[...]

    def branch(w_ref, b_ref, cw_ref, cb_ref):
        p = jnp.dot(a, w_ref[...], preferred_element_type=F32) + b_ref[...]
        return _short_conv(p, cw_ref, cb_ref[...], HY_SHORT_LEFT, GRID_W)

    x0_ref[...] = branch(w0_ref, b0_ref, cw0_ref, cb0_ref)
    wf_ref[...] = branch(w1_ref, b1_ref, cw1_ref, cb1_ref) * branch(w2_ref, b2_ref, cw2_ref, cb2_ref)


def _projection_hyena(a, w, bias, conv_w, conv_b, c):
    m, k = a.shape
    tm = _tile(m, 1024)
    assert tm % GRID_W == 0
    tc = _tile(c, 256)
    nc = c // tc
    kw = conv_w.shape[0]
    wspec = lambda g: pl.BlockSpec((k, tc), lambda i, j: (0, j + g * nc))
    bspec = lambda g: pl.BlockSpec((1, tc), lambda i, j: (0, j + g * nc))
    cwspec = lambda g: pl.BlockSpec((kw, tc), lambda i, j: (0, j + g * nc))
    ospec = pl.BlockSpec((tm, tc), lambda i, j: (i, j))
    b2 = bias.reshape(1, -1)
    cb = conv_b.reshape(1, -1)
    shape = jax.ShapeDtypeStruct((m, c), F32)
    return pl.pallas_call(
        _proj_hyena_kernel,
        grid=(m // tm, nc),
        in_specs=[pl.BlockSpec((tm, k), lambda i, j: (i, 0)),
                  wspec(0), wspec(1), wspec(2), bspec(0), bspec(1), bspec(2),
                  cwspec(0), cwspec(1), cwspec(2), bspec(0), bspec(1), bspec(2)],
        out_specs=[ospec, ospec],
        out_shape=[shape, shape],
        compiler_params=_params(("parallel", "arbitrary")),
        name="projection_hyena",
    )(a, w, w, w, b2, b2, b2, conv_w, conv_w, conv_w, cb, cb, cb)


def _hp_dot(a, b):
    return jnp.dot(a, b, preferred_element_type=F32, precision=lax.Precision.HIGHEST)


def _filter_kernel(n, w1t_ref, w1c_ref, w1s_ref, b1_ref, w2_ref, b2_ref, w3_ref, b3_ref,
                   wout_ref, freq_ref, delta_ref, h_ref, norm_ref):
    i = pl.program_id(0)
    tn = h_ref.shape[0]
    pos_i = i * tn + lax.broadcasted_iota(jnp.int32, (tn, 1), 0)
    pos = pos_i.astype(F32)
    t = pos * (1.0 / (n - 1))
    band_step = (FILT_BANDS - 1 - 1e-4) / (FILT_BANDS - 1)
    bands = 1e-4 + band_step * lax.broadcasted_iota(jnp.int32, (1, FILT_BANDS), 1).astype(F32)
    ang = (2.0 * math.pi * pos / n) * bands
    fr = freq_ref[...]
    pre = t * w1t_ref[...] + _hp_dot(jnp.cos(ang), w1c_ref[...]) - _hp_dot(jnp.sin(ang), w1s_ref[...])
    h = jnp.sin(fr * (pre + b1_ref[...]))
    h = jnp.sin(fr * (_hp_dot(h, w2_ref[...]) + b2_ref[...]))
    h = jnp.sin(fr * (_hp_dot(h, w3_ref[...]) + b3_ref[...]))
    taps = _hp_dot(h, wout_ref[...]) * jnp.exp(-t * delta_ref[...])

    @pl.when(i == 0)
    def _():
        norm_ref[...] = jnp.zeros_like(norm_ref)

    norm_ref[...] += jnp.sum(jnp.abs(taps), axis=0, keepdims=True)
    c = taps.shape[1] // 2
    col = lax.broadcasted_iota(jnp.int32, (1, taps.shape[1]), 1)
    drop = (pos_i == 0) & (col >= c)
    h_ref[...] = jnp.where(drop, 0.0, taps).astype(h_ref.dtype)


def _hyena_filter(n, w1, b1, w2, b2, w3, b3, wout, freq):
    hid = w1.shape[1]
    c2 = wout.shape[1]
    c = c2 // 2
    tn = _tile(n, 512)
    deltas = jnp.abs(jnp.linspace(MIN_DECAY, MAX_DECAY, c, dtype=F32))
    deltas = jnp.concatenate([deltas, deltas]).reshape(1, c2)
    full = lambda shape: pl.BlockSpec(shape, lambda i: (0, 0))
    return pl.pallas_call(
        functools.partial(_filter_kernel, n),
        grid=(n // tn,),
        in_specs=[full((1, hid)), full((FILT_BANDS, hid)), full((FILT_BANDS, hid)), full((1, hid)),
                  full((hid, hid)), full((1, hid)), full((hid, hid)), full((1, hid)),
                  full((hid, c2)), full((1, hid)), full((1, c2))],
        out_specs=[pl.BlockSpec((tn, c2), lambda i: (i, 0)), full((1, c2))],
        out_shape=[jax.ShapeDtypeStruct((n, c2), F32), jax.ShapeDtypeStruct((1, c2), F32)],
        compiler_params=_params(("arbitrary",)),
        name="hyena_filter",
    )(w1[0:1], w1[1:1 + FILT_BANDS], w1[1 + FILT_BANDS:], b1.reshape(1, hid), w2, b2.reshape(1, hid),
      w3, b3.reshape(1, hid), wout, freq.reshape(1, hid), deltas)


FFT_P = LANES
FFT_R = SUBLANES


def _phase(num, den):
    return (num % den).astype(F32) * (2.0 * math.pi / den)


def _fft_matrices(n):
    p, r = FFT_P, FFT_R
    q = n // p
    n2 = 2 * n
    iq = jnp.arange(q, dtype=jnp.int32)
    eye = jnp.eye(r, dtype=F32)
    alt = lambda v: (1 - 2 * (v & 1)).astype(F32)

    ang = _phase(iq[:, None] * iq[None, :], 2 * q)
    a_re = jnp.cos(ang)
    a_im = (-jnp.sin(ang)).at[0].set(alt(iq))
    core = jnp.stack([a_re, a_im], axis=1)
    m1 = jnp.einsum('fks,rt->fkrst', core, eye).reshape(q * 2 * r, q * r)

    b_c = jnp.cos(ang.T)
    b_s = (-jnp.sin(ang.T)).at[:, 0].set(alt(iq))
    core = jnp.stack([b_c, b_s], axis=2)
    i2 = jnp.einsum('tfk,rs->trfks', core, eye).reshape(q * r, q * 2 * r)

    h = p // 2
    f2 = jnp.arange(h, dtype=jnp.int32)
    s2 = jnp.arange(p, dtype=jnp.int32)
    f_lo = iq[:, None] + 2 * q * f2[None, :]
    f_hi = jnp.where(iq[:, None] == 0, q, 2 * q - iq[:, None]) + 2 * q * f2[None, :]
    freq = jnp.stack([f_lo, f_hi], axis=1)
    phi = _phase(freq[..., None] * s2, n2)
    c, s = jnp.cos(phi), jnp.sin(phi)
    zero = jnp.zeros_like(c[:, 0])
    first = (iq == 0)[:, None, None]
    dcrow = (first & (f2 == 0)[None, :, None])
    nyq = jnp.broadcast_to(alt(s2), c[:, 0].shape)
    on_gr = jnp.stack([jnp.where(first, c[:, 0], c[:, 0]),
                       jnp.where(dcrow, nyq, -s[:, 0]),
                       jnp.where(first, zero, c[:, 1]),
                       jnp.where(first, zero, -s[:, 1])], axis=1)
    on_gi = jnp.stack([jnp.where(first, zero, s[:, 0]),
                       jnp.where(first, zero, c[:, 0]),
                       jnp.where(first, c[:, 1], -s[:, 1]),
                       jnp.where(first, -s[:, 1], -c[:, 1])], axis=1)
    m2 = jnp.concatenate([on_gr, on_gi], axis=-1).reshape(q, 4 * h, 2 * p)
    ct, st = jnp.swapaxes(c, 2, 3), jnp.swapaxes(s, 2, 3)
    zt = jnp.zeros_like(ct[:, 0])
    dccol = (first & (f2 == 0)[None, None, :])
    nyq_t = jnp.broadcast_to(alt(s2)[:, None], ct[:, 0].shape)
    hc = jnp.stack([ct[:, 0], jnp.where(dccol, nyq_t, -st[:, 0]),
                    jnp.where(first, zt, ct[:, 1]), jnp.where(first, zt, -st[:, 1])], axis=2)
    hs = jnp.stack([jnp.where(first, zt, st[:, 0]), jnp.where(first, zt, ct[:, 0]),
                    jnp.where(first, ct[:, 1], -st[:, 1]), jnp.where(first, -st[:, 1], -ct[:, 1])], axis=2)
    i1 = jnp.stack([hc, hs], axis=1).reshape(q, 2 * p, 4 * h)
    return tuple(m.astype(BF16) for m in (m1, m2, i1, i2))


def _fft_stage_a(u_ref, m1_ref, g_ref):
    q, groups, r, tc = u_ref.shape
    m1 = m1_ref[...]
    for g in range(groups):
        blk = u_ref[:, g].reshape(q * r, tc).astype(BF16)
        g_ref[g] = jnp.dot(m1, blk, preferred_element_type=F32).reshape(q, 2, r, tc)


def _fft_stage_b(g_ref, f1, m2):
    groups, _, _, r, tc = g_ref.shape
    z = jnp.concatenate([g_ref[:, f1, 0].reshape(groups * r, tc),
                         g_ref[:, f1, 1].reshape(groups * r, tc)], axis=0).astype(BF16)
    return jnp.dot(m2, z, preferred_element_type=F32)


def _cmul_packed(x, k, first):
    h = x.shape[0] // 4
    xr = (x[0:h], x[2 * h:3 * h])
    xi = (x[h:2 * h], x[3 * h:4 * h])
    kr = (k[0:h], k[2 * h:3 * h])
    ki = (k[h:2 * h], k[3 * h:4 * h])
    real_pair = first & (lax.broadcasted_iota(jnp.int32, (h, 1), 0) == 0)
    lo_re = jnp.where(real_pair, xr[0] * kr[0], xr[0] * kr[0] - xi[0] * ki[0])
    lo_im = jnp.where(real_pair, xi[0] * ki[0], xr[0] * ki[0] + xi[0] * kr[0])
    return jnp.concatenate([lo_re, lo_im, xr[1] * kr[1] - xi[1] * ki[1], xr[1] * ki[1] + xi[1] * kr[1]], axis=0)


def _filter_spec_kernel(n, hf_ref, hb_ref, nf_ref, nb_ref, m1_ref, m2_ref, k_ref, gf_ref, gb_ref):
    qi = pl.program_id(1)
    qb = m2_ref.shape[0]

    @pl.when(qi == 0)
    def _():
        _fft_stage_a(hf_ref, m1_ref, gf_ref)
        _fft_stage_a(hb_ref, m1_ref, gb_ref)

    inv_norm = 1.0 / (nf_ref[...] + nb_ref[...])
    h = m2_ref.shape[1] // 4
    row = lax.broadcasted_iota(jnp.int32, (4 * h, 1), 0)
    for k in range(qb):
        f1 = qi * qb + k
        xf = _fft_stage_b(gf_ref, f1, m2_ref[k])
        xb = _fft_stage_b(gb_ref, f1, m2_ref[k])
        real_pair = (f1 == 0) & ((row == 0) | (row == h))
        imag_row = ((row >= h) & (row < 2 * h)) | (row >= 3 * h)
        spec = jnp.where(imag_row & jnp.logical_not(real_pair), xf - xb, xf + xb)
        spec = spec * jnp.where(real_pair, 0.5 / n, 1.0 / n) * inv_norm
        k_ref[k] = spec.reshape(k_ref.shape[1:]).astype(k_ref.dtype)


def _time_view(v, n):
    return v.reshape(v.shape[:-2] + (n // FFT_P, FFT_P // FFT_R, FFT_R, v.shape[-1]))


def _filter_spectrum(taps, norm, mats, c):
    n = taps.shape[0]
    m1, m2, _, _ = mats
    q = n // FFT_P
    h = FFT_P // 2
    tc = _tile(c, 256)
    nc = c // tc
    qb = _tile(q, 8)
    tv = _time_view(taps, n)
    blk = (q, FFT_P // FFT_R, FFT_R, tc)
    return pl.pallas_call(
        functools.partial(_filter_spec_kernel, n),
        grid=(nc, q // qb),
        in_specs=[pl.BlockSpec(blk, lambda j, i: (0, 0, 0, j)),
                  pl.BlockSpec(blk, lambda j, i: (0, 0, 0, j + nc)),
                  pl.BlockSpec((1, tc), lambda j, i: (0, j)),
                  pl.BlockSpec((1, tc), lambda j, i: (0, j + nc)),
                  pl.BlockSpec(m1.shape, lambda j, i: (0, 0)),
                  pl.BlockSpec((qb,) + m2.shape[1:], lambda j, i: (i, 0, 0))],
        out_specs=pl.BlockSpec((qb, 4, h, tc), lambda j, i: (i, 0, 0, j)),
        out_shape=jax.ShapeDtypeStruct((q, 4, h, c), BF16),
        scratch_shapes=[pltpu.VMEM((FFT_P // FFT_R, q, 2, FFT_R, tc), F32) for _ in range(2)],
        compiler_params=_params(("parallel", "arbitrary")),
        name="filter_spectrum",
    )(tv, tv, norm, norm, m1, m2)


def _conv_fwd_kernel(w_ref, m1_ref, m2_ref, k_ref, y_ref, g_ref):
    qi = pl.program_id(2)
    qb = m2_ref.shape[0]

    @pl.when(qi == 0)
    def _():
        _fft_stage_a(w_ref.at[0], m1_ref, g_ref)

    for k in range(qb):
        f1 = qi * qb + k
        x = _fft_stage_b(g_ref, f1, m2_ref[k])
        y = _cmul_packed(x, k_ref[k].reshape(x.shape).astype(F32), f1 == 0)
        y_ref[0, k] = y.reshape(y_ref.shape[2:]).astype(y_ref.dtype)


def _conv_forward(wf, kspec, mats):
    b, n, c = wf.shape
    m1, m2, _, _ = mats
    q = n // FFT_P
    h = FFT_P // 2
    tc = _tile(c, 256)
    qb = _tile(q, 8)
    return pl.pallas_call(
        _conv_fwd_kernel,
        grid=(c // tc, b, q // qb),
        in_specs=[pl.BlockSpec((1, q, FFT_P // FFT_R, FFT_R, tc), lambda j, bi, i: (bi, 0, 0, 0, j)),
                  pl.BlockSpec(m1.shape, lambda j, bi, i: (0, 0)),
                  pl.BlockSpec((qb,) + m2.shape[1:], lambda j, bi, i: (i, 0, 0)),
                  pl.BlockSpec((qb, 4, h, tc), lambda j, bi, i: (i, 0, 0, j))],
        out_specs=pl.BlockSpec((1, qb, 4, h, tc), lambda j, bi, i: (bi, i, 0, 0, j)),
        out_shape=jax.ShapeDtypeStruct((b, q, 4, h, c), BF16),
        scratch_shapes=[pltpu.VMEM((FFT_P // FFT_R, q, 2, FFT_R, tc), F32)],
        compiler_params=_params(("parallel", "parallel", "arbitrary")),
        name="conv_forward_fft",
    )(_time_view(wf, n), m1, m2, kspec)


def _conv_inv_kernel(y_ref, i1_ref, i2_ref, wf_ref, x0_ref, bias_ref, o_ref, h_ref):
    qi = pl.program_id(2)
    qb = i1_ref.shape[0]
    groups, q, _, r, tc = h_ref.shape
    for k in range(qb):
        hv = jnp.dot(i1_ref[k], y_ref[0, k].reshape(i1_ref.shape[2], tc), preferred_element_type=F32)
        half = hv.shape[0] // 2
        h_ref[:, qi * qb + k, 0] = hv[:half].reshape(groups, r, tc)
        h_ref[:, qi * qb + k, 1] = hv[half:].reshape(groups, r, tc)

    @pl.when(qi == pl.num_programs(2) - 1)
    def _():
        i2 = i2_ref[...]
        bias = bias_ref[...]
        for g2 in range(groups // 2):
            parts = []
            for g in (2 * g2, 2 * g2 + 1):
                z = jnp.dot(i2, h_ref[g].reshape(q * 2 * r, tc).astype(BF16), preferred_element_type=F32)
                z = z.reshape(q, r, tc)
                parts.append(x0_ref[0, :, g] * (z + wf_ref[0, :, g] * bias))
            o_ref[0, :, g2] = jnp.concatenate(parts, axis=1).astype(o_ref.dtype)


def _conv_inverse(yspec, wf, x0, bias, mats):
    b, n, c = wf.shape
    _, _, i1, i2 = mats
    q = n // FFT_P
    h = FFT_P // 2
    groups = FFT_P // FFT_R
    tc = _tile(c, 256)
    qb = _tile(q, 8)
    tspec = pl.BlockSpec((1, q, groups, FFT_R, tc), lambda j, bi, i: (bi, 0, 0, 0, j))
    out = pl.pallas_call(
        _conv_inv_kernel,
        grid=(c // tc, b, q // qb),
        in_specs=[pl.BlockSpec((1, qb, 4, h, tc), lambda j, bi, i: (bi, i, 0, 0, j)),
                  pl.BlockSpec((qb,) + i1.shape[1:], lambda j, bi, i: (i, 0, 0)),
                  pl.BlockSpec(i2.shape, lambda j, bi, i: (0, 0)),
                  tspec, tspec,
                  pl.BlockSpec((1, tc), lambda j, bi, i: (0, j))],
        out_specs=pl.BlockSpec((1, q, groups // 2, 2 * FFT_R, tc), lambda j, bi, i: (bi, 0, 0, 0, j)),
        out_shape=jax.ShapeDtypeStruct((b, q, groups // 2, 2 * FFT_R, c), BF16),
        scratch_shapes=[pltpu.VMEM((groups, q, 2, FFT_R, tc), F32)],
        compiler_params=_params(("parallel", "parallel", "arbitrary")),
        name="conv_inverse_fft",
    )(yspec, i1, i2, _time_view(wf, n), _time_view(x0, n), bias.reshape(1, c))
    return out.reshape(b, n, c)


SCAN_UNROLL = 4


def _tile_scan(a, b, reverse):
    rows, c = a.shape
    a3 = a.reshape(rows // SUBLANES, SUBLANES, c)
    b3 = b.reshape(rows // SUBLANES, SUBLANES, c)
    sub = lax.broadcasted_iota(jnp.int32, (1, SUBLANES, 1), 1)
    for k in (1, 2, 4):
        shift = SUBLANES - k if reverse else k
        valid = (sub < SUBLANES - k) if reverse else (sub >= k)
        b3 = b3 + a3 * jnp.where(valid, pltpu.roll(b3, shift, axis=1), 0.0)
        a3 = a3 * jnp.where(valid, pltpu.roll(a3, shift, axis=1), 1.0)
    return a3.reshape(rows, c), b3.reshape(rows, c)


def _lru_kernel(group, has_gate, *refs):
    if has_gate:
        (x_ref, gate_ref, cw_ref, cb_ref, wa_ref, ba_ref, wx_ref, bx_ref, lam_ref, h0_ref,
         y_ref, ht_ref, af_ref, bf_ref, ab_ref, bb_ref) = refs
    else:
        (x_ref, cw_ref, cb_ref, wa_ref, ba_ref, wx_ref, bx_ref, lam_ref, h0_ref,
         ht_ref, af_ref, bf_ref, ab_ref, bb_ref) = refs
    t_len = x_ref.shape[1]
    chunk = _tile(t_len, 512)
    a_refs = (af_ref, ab_ref)
    b_refs = (bf_ref, bb_ref)

    def coeffs(ci, carry):
        r0 = pl.multiple_of(ci * chunk, chunk)
        xc = _short_conv(x_ref[0, pl.ds(r0, chunk), :], cw_ref, cb_ref[...], LRU_CONV_LEFT, group)
        xcb = xc.astype(BF16)
        half_xc = 0.5 * xc
        for d in range(2):
            ta = jnp.tanh(jnp.dot(xcb, wa_ref[d, 0], preferred_element_type=F32) + ba_ref[d])
            ti = jnp.tanh(jnp.dot(xcb, wx_ref[d, 0], preferred_element_type=F32) + bx_ref[d])
            lam = lam_ref[d]
            softplus_neg = jnp.maximum(-lam, 0.0) + jnp.log(1.0 + jnp.exp(-jnp.abs(lam)))
            c1 = (-0.5 * LRU_C) * softplus_neg
            a = jnp.exp(c1 * ta + c1)
            a_tile, b_tile = _tile_scan(a, jnp.sqrt(1.0 - a * a) * (half_xc * ti + half_xc), d == 1)
            a_refs[d][pl.ds(r0, chunk), :] = a_tile
            b_refs[d][pl.ds(r0, chunk), :] = b_tile
        return carry

    lax.fori_loop(0, t_len // chunk, coeffs, 0)

    n_tiles = t_len // SUBLANES
    unroll = _tile(n_tiles, SCAN_UNROLL)

    def scan(i, carry):
        hf, hb = carry
        for u in range(unroll):
            kf = i * unroll + u
            rf = pl.multiple_of(kf * SUBLANES, SUBLANES)
            tile_f = af_ref[pl.ds(rf, SUBLANES), :] * hf + bf_ref[pl.ds(rf, SUBLANES), :]
            bf_ref[pl.ds(rf, SUBLANES), :] = tile_f
            hf = tile_f[SUBLANES - 1:SUBLANES, :]
            rb = pl.multiple_of((n_tiles - 1 - kf) * SUBLANES, SUBLANES)
            tile_b = ab_ref[pl.ds(rb, SUBLANES), :] * hb + bb_ref[pl.ds(rb, SUBLANES), :]
            bb_ref[pl.ds(rb, SUBLANES), :] = tile_b
            hb = tile_b[0:1, :]
        return hf, hb

    hf, hb = lax.fori_loop(0, n_tiles // unroll, scan, (h0_ref[0, 0:1, :], h0_ref[0, 1:2, :]))
    ht_ref[0, 0:1, :] = hf
    ht_ref[0, 1:2, :] = hb

    if has_gate:
        def emit(ci, carry):
            r0 = pl.multiple_of(ci * chunk, chunk)
            hs = bf_ref[pl.ds(r0, chunk), :] + bb_ref[pl.ds(r0, chunk), :]
            y_ref[0, pl.ds(r0, chunk), :] = (jax.nn.gelu(gate_ref[0, pl.ds(r0, chunk), :]) * hs).astype(y_ref.dtype)
            return carry

        lax.fori_loop(0, t_len // chunk, emit, 0)


def _block_diag(w, heads_per_tile):
    d2, h, hd, _ = w.shape
    w = w.reshape(d2, h // heads_per_tile, heads_per_tile, hd, hd)
    eye = jnp.eye(heads_per_tile, dtype=w.dtype)
    bd = jnp.einsum('dghij,hq->dghiqj', w, eye)
    return bd.reshape(d2, h // heads_per_tile, heads_per_tile * hd, heads_per_tile * hd).astype(BF16)


def _rglru(xsrc, x_col, gsrc, g_col, conv_w, conv_b, wa, ba, wx, bx, lam, h0, group):
    b, t, _ = xsrc.shape
    c = conv_w.shape[1]
    hd = c // LRU_HEADS
    tc = min(c, max(hd, 256))
    hp = tc // hd
    nt = c // tc
    has_gate = gsrc is not None
    wa_bd = _block_diag(0.5 * wa, hp)
    wx_bd = _block_diag(0.5 * wx, hp)
    ba, bx = 0.5 * ba, 0.5 * bx
    xo, go = x_col // tc, (g_col // tc if has_gate else 0)
    vec = lambda: pl.BlockSpec((2, 1, tc), lambda i, j: (0, 0, j))
    mat = lambda: pl.BlockSpec((2, 1, tc, tc), lambda i, j: (0, j, 0, 0))
    in_specs = [pl.BlockSpec((1, t, tc), lambda i, j: (i, 0, j + xo))]
    args = [xsrc]
    if has_gate:
        in_specs.append(pl.BlockSpec((1, t, tc), lambda i, j: (i, 0, j + go)))
        args.append(gsrc)
    in_specs += [pl.BlockSpec((conv_w.shape[0], tc), lambda i, j: (0, j)),
                 pl.BlockSpec((1, tc), lambda i, j: (0, j)),
                 mat(), vec(), mat(), vec(), vec(),
                 pl.BlockSpec((1, 2, tc), lambda i, j: (i, 0, j))]
    args += [conv_w, conv_b.reshape(1, c), wa_bd, ba.reshape(2, 1, c), wx_bd, bx.reshape(2, 1, c),
             lam.reshape(2, 1, c), h0]
    ht_spec = pl.BlockSpec((1, 2, tc), lambda i, j: (i, 0, j))
    ht_shape = jax.ShapeDtypeStruct((b, 2, c), F32)
    if has_gate:
        out_specs = [pl.BlockSpec((1, t, tc), lambda i, j: (i, 0, j)), ht_spec]
        out_shape = [jax.ShapeDtypeStruct((b, t, c), BF16), ht_shape]
    else:
        out_specs = [ht_spec]
        out_shape = [ht_shape]
    return pl.pallas_call(
        functools.partial(_lru_kernel, group, has_gate),
        grid=(b, nt),
        in_specs=in_specs,
        out_specs=out_specs,
        out_shape=out_shape,
        scratch_shapes=[pltpu.VMEM((t, tc), F32) for _ in range(4)],
        compiler_params=_params(("parallel", "parallel")),
        name="rglru" if has_gate else "rglru_context",
    )(*args)


def _out_proj_kernel(a1_ref, a2_ref, w1_ref, w2_ref, b_ref, x_ref, g_ref, o_ref):
    acc = jnp.dot(a1_ref[...], w1_ref[...], preferred_element_type=F32)
    acc = acc + jnp.dot(a2_ref[...], w2_ref[...], preferred_element_type=F32)
    o_ref[...] = x_ref[...] + g_ref[0] * (acc + b_ref[...])


def _out_projection(a1, a2, w, bias, x, gate, t):
    m, k1 = a1.shape
    n = w.shape[1]
    tm = _tile(t, 1024)
    tn = _tile(n, 512)
    per_b = t // tm
    k1_blocks = 1
    return pl.pallas_call(
        _out_proj_kernel,
        grid=(m // tm, n // tn),
        in_specs=[pl.BlockSpec((tm, k1), lambda i, j: (i, 0)),
                  pl.BlockSpec((tm, k1), lambda i, j: (i, 0)),
                  pl.BlockSpec((k1, tn), lambda i, j: (0, j)),
                  pl.BlockSpec((k1, tn), lambda i, j: (k1_blocks, j)),
                  pl.BlockSpec((1, tn), lambda i, j: (0, j)),
                  pl.BlockSpec((tm, tn), lambda i, j: (i, j)),
                  pl.BlockSpec((1, 1, tn), lambda i, j: (i // per_b, 0, j))],
        out_specs=pl.BlockSpec((tm, tn), lambda i, j: (i, j)),
        out_shape=jax.ShapeDtypeStruct((m, n), F32),
        compiler_params=_params(("parallel", "arbitrary")),
        name="out_projection",
    )(a1, a2, w, w, bias.reshape(1, n), x, gate)


def _router_kernel(x_ref, g_ref, sh_ref, sc_ref, wr_ref, h_ref, aff_ref):
    h = _rms_mod(x_ref[0], g_ref[...], sh_ref[0], sc_ref[0])
    h_ref[...] = h
    h_hi, h_lo = _split_bf16(h)
    w_hi, w_lo = _split_bf16(wr_ref[...])
    nt = (((1,), (1,)), ((), ()))
    logits = (lax.dot_general(w_hi, h_hi, nt, preferred_element_type=F32)
              + lax.dot_general(w_lo, h_hi, nt, preferred_element_type=F32)
              + lax.dot_general(w_hi, h_lo, nt, preferred_element_type=F32))
    z = jnp.exp(logits - jnp.max(logits, axis=0, keepdims=True))
    aff_ref[0] = z / jnp.sum(z, axis=0, keepdims=True)


def _router(x1, g, shift, scale, w_router):
    b, t, d = x1.shape
    e = w_router.shape[1]
    tt = _tile(t, 256)
    per_b = t // tt
    mod_map = lambda i, j: (i, 0, 0)
    return pl.pallas_call(
        _router_kernel,
        grid=(b, per_b),
        in_specs=[pl.BlockSpec((1, tt, d), lambda i, j: (i, j, 0)),
                  pl.BlockSpec((1, d), lambda i, j: (0, 0)),
                  pl.BlockSpec((1, 1, d), mod_map),
                  pl.BlockSpec((1, 1, d), mod_map),
                  pl.BlockSpec((e, d), lambda i, j: (0, 0))],
        out_specs=[pl.BlockSpec((tt, d), lambda i, j: (i * per_b + j, 0)),
                   pl.BlockSpec((1, e, tt), lambda i, j: (i, 0, j))],
        out_shape=[jax.ShapeDtypeStruct((b * t, d), F32),
                   jax.ShapeDtypeStruct((b, e, t), F32)],
        compiler_params=_params(("parallel", "parallel")),
        name="router",
    )(x1, g.reshape(1, d), shift, scale, w_router.T)


def _prefix_count(mask_ref, out_ref):
    e, t = mask_ref.shape
    blk = min(t, LANES)
    tri = (lax.broadcasted_iota(jnp.int32, (blk, blk), 0)
           < lax.broadcasted_iota(jnp.int32, (blk, blk), 1)).astype(BF16)
    carry = jnp.zeros((e, 1), F32)
    for k in range(t // blk):
        m = mask_ref[:, k * blk:(k + 1) * blk]
        out_ref[:, k * blk:(k + 1) * blk] = jnp.dot(m.astype(BF16), tri, preferred_element_type=F32) + carry
        carry = carry + jnp.sum(m, axis=1, keepdims=True)
    return carry


def _topk_kernel(cap, aff_ref, idx_ref, gate_ref, dest_ref, span_ref, mask_ref, pos_ref, cnt_ref, q_ref):
    a = aff_ref[0]
    e, t = a.shape
    min_normal = 0x00800000

    def refine(i, thr):
        cand = thr | jnp.left_shift(jnp.int32(1), 30 - i)
        cnt = jnp.sum(jnp.where(a >= pltpu.bitcast(cand, F32), 1.0, 0.0), axis=1, keepdims=True)
        return jnp.where((cnt >= cap) & (cand >= min_normal), cand, thr)

    thr = lax.fori_loop(0, 31, refine, jnp.zeros((e, 1), jnp.int32))
    above = a >= pltpu.bitcast(jnp.maximum(thr + 1, min_normal), F32)
    tied = (a >= pltpu.bitcast(thr, F32)) & jnp.logical_not(above)
    need = cap - jnp.sum(jnp.where(above, 1.0, 0.0), axis=1, keepdims=True)
    mask_ref[...] = jnp.where(tied, 1.0, 0.0)
    _prefix_count(mask_ref, pos_ref)
    sel = above | (tied & (pos_ref[...] < need))
    mask_ref[...] = jnp.where(sel, 1.0, 0.0)
    _prefix_count(mask_ref, pos_ref)

    mask = mask_ref[...]
    cnt = jnp.sum(mask, axis=0, keepdims=True)
    cnt_ref[...] = jnp.broadcast_to(cnt, cnt_ref.shape)
    _prefix_count(cnt_ref, q_ref.at[0:SUBLANES])
    first = q_ref[0:1, :]
    srow = lax.broadcasted_iota(jnp.int32, (SUBLANES, 1), 0)
    span_ref[0] = jnp.where(srow == 0, first, jnp.where(srow == 1, first + cnt, 0.0))
    lower = (lax.broadcasted_iota(jnp.int32, (e, e), 1)
             < lax.broadcasted_iota(jnp.int32, (e, e), 0)).astype(BF16)
    q_ref[...] = first + jnp.dot(lower, mask.astype(BF16), preferred_element_type=F32)

    tok = lax.broadcasted_iota(jnp.int32, (1, t), 1)
    tok_hi = (tok >> 6).astype(F32)
    tok_lo = (tok & 63).astype(F32)
    slot = lax.broadcasted_iota(jnp.int32, (cap, 1), 0).astype(F32)
    vrow = lax.broadcasted_iota(jnp.int32, (SUBLANES, 1), 0)
    dest_base = pl.program_id(0) * (e * cap)

    def compact(ei, carry):
        g = aff_ref[0, pl.ds(ei, 1), :]
        g_hi = g.astype(BF16).astype(F32)
        g_mid = (g - g_hi).astype(BF16).astype(F32)
        g_lo = g - g_hi - g_mid
        q = q_ref[pl.ds(ei, 1), :]
        q_hi = jnp.floor(q * (1.0 / 64.0))
        q_lo = q - 64.0 * q_hi
        vals = jnp.where(vrow == 0, tok_hi, jnp.where(vrow == 1, tok_lo, jnp.where(
            vrow == 2, g_hi, jnp.where(vrow == 3, g_mid, jnp.where(vrow == 4, g_lo, jnp.where(
                vrow == 5, q_hi, jnp.where(vrow == 6, q_lo, 0.0))))))).astype(BF16)
        hit = (pos_ref[pl.ds(ei, 1), :] == slot) & (mask_ref[pl.ds(ei, 1), :] > 0.0)
        onehot = jnp.where(hit, 1.0, 0.0).astype(BF16)
        res = lax.dot_general(vals, onehot, (((1,), (1,)), ((), ())), preferred_element_type=F32)
        idx_ref[0, pl.ds(ei, 1), :] = (res[0:1] * 64.0 + res[1:2]).astype(jnp.int32)
        gate_ref[0, pl.ds(ei, 1), :] = res[2:3] + res[3:4] + res[4:5]
        dest_ref[0, pl.ds(ei, 1), :] = (res[5:6] * 64.0 + res[6:7]).astype(jnp.int32) + dest_base
        return carry

    lax.fori_loop(0, e, compact, 0)


def _select_tokens(aff, cap):
    b, e, t = aff.shape
    assert t <= 64 * 256 and e * cap <= 64 * 256 and e >= SUBLANES
    spec = pl.BlockSpec((1, e, cap), lambda i: (i, 0, 0))
    return pl.pallas_call(
        functools.partial(_topk_kernel, cap),
        grid=(b,),
        in_specs=[pl.BlockSpec((1, e, t), lambda i: (i, 0, 0))],
        out_specs=[spec, spec, spec, pl.BlockSpec((1, SUBLANES, t), lambda i: (i, 0, 0))],
        out_shape=[jax.ShapeDtypeStruct((b, e, cap), jnp.int32), jax.ShapeDtypeStruct((b, e, cap), F32),
                   jax.ShapeDtypeStruct((b, e, cap), jnp.int32), jax.ShapeDtypeStruct((b, SUBLANES, t), F32)],
        scratch_shapes=[pltpu.VMEM((e, t), F32), pltpu.VMEM((e, t), F32),
                        pltpu.VMEM((SUBLANES, t), F32), pltpu.VMEM((e, t), F32)],
        compiler_params=_params(("parallel",)),
        name="select_tokens",
    )(aff)


def _gather_kernel(rows_ref, h_hbm, o_ref, buf, sem):
    i = pl.program_id(0)
    r_blk = buf.shape[1]

    def issue_block(blk, slot):
        def issue(r, carry):
            row = rows_ref[blk * r_blk + r]
            pltpu.make_async_copy(h_hbm.at[pl.ds(row, 1), :], buf.at[slot, pl.ds(r, 1), :], sem.at[slot]).start()
            return carry

        lax.fori_loop(0, r_blk, issue, 0, unroll=8)

    @pl.when(i == 0)
    def _():
        issue_block(0, 0)

    @pl.when(i + 1 < pl.num_programs(0))
    def _():
        issue_block(i + 1, (i + 1) & 1)

    slot = i & 1
    pltpu.make_async_copy(h_hbm.at[pl.ds(0, r_blk), :], buf.at[slot], sem.at[slot]).wait()
    o_ref[...] = buf[slot].astype(o_ref.dtype)


def _gather_rows(rows, h):
    n_rows = rows.shape[0]
    d = h.shape[1]
    r_blk = _tile(n_rows, 256)
    grid_spec = pltpu.PrefetchScalarGridSpec(
        num_scalar_prefetch=1,
        grid=(n_rows // r_blk,),
        in_specs=[pl.BlockSpec(memory_space=pl.ANY)],
        out_specs=pl.BlockSpec((r_blk, d), lambda i, rows_ref: (i, 0)),
        scratch_shapes=[pltpu.VMEM((2, r_blk, d), F32), pltpu.SemaphoreType.DMA((2,))],
    )
    return pl.pallas_call(
        _gather_kernel,
        grid_spec=grid_spec,
        out_shape=jax.ShapeDtypeStruct((n_rows, d), BF16),
        compiler_params=_params(("arbitrary",)),
        name="gather_rows",
    )(rows, h)


def _expert_up_kernel(xs_ref, wg_ref, wu_ref, h_ref):
    xs = xs_ref[...]
    g = jnp.dot(xs, wg_ref[0].astype(BF16), preferred_element_type=F32)
    u = jnp.dot(xs, wu_ref[0].astype(BF16), preferred_element_type=F32)
    h_ref[...] = (g * jax.nn.sigmoid(g) * u).astype(h_ref.dtype)


def _expert_up(xs, w_gate, w_up, rows_per_expert):
    m, d = xs.shape
    f = w_gate.shape[2]
    tm = _tile(rows_per_expert, 1024)
    per_e = rows_per_expert // tm
    tf = _tile(f, 256)
    wspec = pl.BlockSpec((1, d, tf), lambda i, j: (i // per_e, 0, j))
    return pl.pallas_call(
        _expert_up_kernel,
        grid=(m // tm, f // tf),
        in_specs=[pl.BlockSpec((tm, d), lambda i, j: (i, 0)), wspec, wspec],
        out_specs=pl.BlockSpec((tm, tf), lambda i, j: (i, j)),
        out_shape=jax.ShapeDtypeStruct((m, f), BF16),
        compiler_params=_params(("parallel", "arbitrary")),
        name="expert_up",
    )(xs, w_gate, w_up)


DOWN_COL_TILE = 1024


def _expert_down_kernel(dest_ref, h_ref, wd_ref, g_ref, ys_hbm, ybuf, sem):
    i = pl.program_id(0)
    j = pl.program_id(1)
    n_i = pl.num_programs(0)
    n_j = pl.num_programs(1)
    _, tm, dw = ybuf.shape
    hw = wd_ref.shape[2] // 2
    slot = i & 1

    def wait_scatter(s):
        pltpu.make_async_copy(ybuf.at[s], ys_hbm.at[pl.ds(0, tm), :], sem.at[s]).wait()

    @pl.when((j == 0) & (i >= 2))
    def _():
        wait_scatter(slot)

    y = jnp.dot(h_ref[...], wd_ref[0].astype(BF16), preferred_element_type=F32) * g_ref[...]
    packed = pltpu.pack_elementwise([y[:, :hw], y[:, hw:]], packed_dtype=BF16)
    for jj in range(dw // hw):
        @pl.when(j == jj)
        def _(jj=jj):
            ybuf[slot, :, jj * hw:(jj + 1) * hw] = packed

    @pl.when(j == n_j - 1)
    def _():
        def issue(r, carry):
            dst = dest_ref[i * tm + r]
            pltpu.make_async_copy(ybuf.at[slot, pl.ds(r, 1), :], ys_hbm.at[pl.ds(dst, 1), :], sem.at[slot]).start()
            return carry

        lax.fori_loop(0, tm, issue, 0, unroll=8)

        @pl.when(i == n_i - 1)
        def _():
            @pl.when(i >= 1)
            def _():
                wait_scatter(1 - slot)
            wait_scatter(slot)


def _expert_down(dest, h, w_down, gates, rows_per_expert):
    m, f = h.shape
    d = w_down.shape[2]
    tm = _tile(rows_per_expert, 1024)
    per_e = rows_per_expert // tm
    td = _tile(d, DOWN_COL_TILE)
    grid_spec = pltpu.PrefetchScalarGridSpec(
        num_scalar_prefetch=1,
        grid=(m // tm, d // td),
        in_specs=[pl.BlockSpec((tm, f), lambda i, j, dest_ref: (i, 0)),
                  pl.BlockSpec((1, f, td), lambda i, j, dest_ref: (i // per_e, 0, j)),
                  pl.BlockSpec((tm, 1), lambda i, j, dest_ref: (i, 0))],
        out_specs=pl.BlockSpec(memory_space=pl.ANY),
        scratch_shapes=[pltpu.VMEM((2, tm, d // 2), jnp.uint32), pltpu.SemaphoreType.DMA((2,))],
    )
    return pl.pallas_call(
        _expert_down_kernel,
        grid_spec=grid_spec,
        out_shape=jax.ShapeDtypeStruct((m, d // 2), jnp.uint32),
        compiler_params=_params(("arbitrary", "arbitrary")),
        name="expert_down",
    )(dest, h, w_down, gates)


def _combine_kernel(slots_per_batch, hw, tb_ref, span_ref, x_ref, gt_ref, g_ref, ys_hbm, o_ref,
                    buf, sem, acc_ref):
    bi = pl.program_id(0)
    i = pl.program_id(1)
    n_t = pl.num_programs(1)
    kc = buf.shape[1]
    total = ys_hbm.shape[0]
    c_lo = tb_ref[bi * (n_t + 1) + i]
    c_hi = tb_ref[bi * (n_t + 1) + i + 1]
    k_first = (c_lo >> 3) << 3
    n_chunks = (c_hi - k_first + kc - 1) // kc

    def chunk_start(c):
        return pl.multiple_of(jnp.minimum(k_first + c * kc, total - kc), SUBLANES)

    def copy(c, slot):
        return pltpu.make_async_copy(ys_hbm.at[pl.ds(chunk_start(c), kc), :], buf.at[slot], sem.at[slot])

    @pl.when(n_chunks > 0)
    def _():
        copy(0, 0).start()

    base = (bi * slots_per_batch).astype(F32)
    first = span_ref[0][:, 0:1] + base
    last = span_ref[0][:, 1:2] + base

    def accumulate(c, initial):
        slot = c & 1
        copy(c, slot).wait()

        @pl.when(c + 1 < n_chunks)
        def _():
            copy(c + 1, 1 - slot).start()

        k = chunk_start(c) + lax.broadcasted_iota(jnp.int32, (1, kc), 1)
        k = jnp.where(k >= k_first + c * kc, k, -1).astype(F32)
        onehot = jnp.where((k >= first) & (k < last), 1.0, 0.0).astype(BF16)
        words = buf[slot]
        lo = pltpu.unpack_elementwise(words, index=0, packed_dtype=BF16, unpacked_dtype=F32).astype(BF16)
        hi = pltpu.unpack_elementwise(words, index=1, packed_dtype=BF16, unpacked_dtype=F32).astype(BF16)
        for half, part in enumerate((lo, hi)):
            contrib = jnp.dot(onehot, part, preferred_element_type=F32)
            acc_ref[half] = contrib if initial else acc_ref[half] + contrib

    @pl.when(n_chunks == 0)
    def _():
        acc_ref[...] = jnp.zeros_like(acc_ref)

    @pl.when(n_chunks > 0)
    def _():
        accumulate(0, True)

    def body(c, carry):
        accumulate(c, False)
        return carry

    lax.fori_loop(1, n_chunks, body, 0)
    lo, hi = acc_ref[0], acc_ref[1]
    moe = jnp.concatenate([part[:, j * hw:(j + 1) * hw] for j in range(lo.shape[1] // hw)
                           for part in (lo, hi)], axis=1)
    x = x_ref[0] + gt_ref[0] * moe
    o_ref[0] = x * lax.rsqrt(jnp.mean(x * x, axis=-1, keepdims=True) + EPS) * g_ref[...]


COMBINE_ROWS = 512
COMBINE_CHUNK = 256


def _combine(tile_bounds, span, ys, x1, gate, g_final, slots_per_batch):
    b, t, d = x1.shape
    tt = _tile(t, COMBINE_ROWS)
    kc = min(COMBINE_CHUNK, ys.shape[0])
    grid_spec = pltpu.PrefetchScalarGridSpec(
        num_scalar_prefetch=1,
        grid=(b, t // tt),
        in_specs=[pl.BlockSpec((1, tt, SUBLANES), lambda bi, i, tb: (bi, i, 0)),
                  pl.BlockSpec((1, tt, d), lambda bi, i, tb: (bi, i, 0)),
                  pl.BlockSpec((1, 1, d), lambda bi, i, tb: (bi, 0, 0)),
                  pl.BlockSpec((1, d), lambda bi, i, tb: (0, 0)),
                  pl.BlockSpec(memory_space=pl.ANY)],
        out_specs=pl.BlockSpec((1, tt, d), lambda bi, i, tb: (bi, i, 0)),
        scratch_shapes=[pltpu.VMEM((2, kc, d // 2), jnp.uint32), pltpu.SemaphoreType.DMA((2,)),
                        pltpu.VMEM((2, tt, d // 2), F32)],
    )
    return pl.pallas_call(
        functools.partial(_combine_kernel, slots_per_batch, _tile(d, DOWN_COL_TILE) // 2),
        grid_spec=grid_spec,
        out_shape=jax.ShapeDtypeStruct((b, t, d), F32),
        compiler_params=_params(("arbitrary", "arbitrary")),
        name="combine",
    )(tile_bounds, span, x1, gate, g_final.reshape(1, d), ys)


def kernel(x, c, ctx, c_ctx, w_mod, b_mod, g_mix, g_ffn, w_in, b_in, hy_conv_w, hy_conv_b, hy_f_w1, hy_f_b1, hy_f_w2, hy_f_b2, hy_f_w3, hy_f_b3, hy_f_wout, hy_f_freq, hy_bias, lru_conv_w, lru_conv_b, lru_wa, lru_ba, lru_wx, lru_bx, lru_lambda, w_out, b_out, w_router, w_exp_gate, w_exp_up, w_exp_down, g_final):
    bsz, n_lat, d = x.shape
    n_ctx = ctx.shape[1]
    depth = w_mod.shape[0]
    c_hy = hy_bias.shape[1]
    c_lru = lru_conv_b.shape[1]
    in_gate = 3 * c_hy
    in_x = in_gate + c_lru
    n_exp = w_router.shape[2]
    cap = CAPACITY_FACTOR * n_lat // n_exp
    assert depth == 1, "context residual updates are only needed for depth > 1"
    assert n_lat % GRID_W == 0 and n_ctx & (n_ctx - 1) == 0

    rows = -(-(bsz + 1) // SUBLANES) * SUBLANES
    c_all = jnp.zeros((rows, d), F32).at[:bsz].set(c).at[bsz].set(c_ctx)
    mats = _fft_matrices(n_lat)

    l = 0
    mod = _modulation(c_all, w_mod[l], b_mod[l])
    mx = mod[:bsz].reshape(bsz, 1, N_MOD, d)
    sh1, sc1, gt1, sh2, sc2, gt2 = (mx[:, :, k] for k in range(N_MOD))
    mc = mod[bsz].reshape(1, 1, N_MOD, d)
    csh1, csc1 = mc[:, :, 0], mc[:, :, 1]

    w_in_b = w_in[l].astype(BF16)
    w_out_b = w_out[l].astype(BF16)

    hx = _norm_mod(x, g_mix[l], sh1, sc1, True)
    hc = _norm_mod(ctx, g_mix[l], csh1, csc1, False)
    hx = hx.reshape(bsz * n_lat, d)
    x0, wf = _projection_hyena(hx, w_in_b, b_in[l], hy_conv_w[l], hy_conv_b[l], c_hy)
    x0, wf = x0.reshape(bsz, n_lat, c_hy), wf.reshape(bsz, n_lat, c_hy)
    p_lru = _projection(hx, w_in_b, b_in[l], in_gate, 2 * c_lru, F32)
    p_lru = p_lru.reshape(bsz, n_lat, 2 * c_lru)
    pc_lx = _projection(hc.reshape(bsz * n_ctx, d), w_in_b, b_in[l], in_x, c_lru, F32)
    pc_lx = pc_lx.reshape(bsz, n_ctx, c_lru)

    lru_args = (lru_conv_w[l], lru_conv_b[l], lru_wa[l], lru_ba[l], lru_wx[l], lru_bx[l], lru_lambda[l])
    (h_ctx,) = _rglru(pc_lx, 0, None, 0, *lru_args, jnp.zeros((bsz, 2, c_lru), F32), n_ctx)
    y_lru, _ = _rglru(p_lru, c_lru, p_lru, 0, *lru_args, h_ctx, GRID_W)

    taps, norm = _hyena_filter(n_lat, hy_f_w1[l], hy_f_b1[l], hy_f_w2[l], hy_f_b2[l], hy_f_w3[l],
                               hy_f_b3[l], hy_f_wout[l], hy_f_freq[l])
    kspec = _filter_spectrum(taps, norm, mats, c_hy)
    yspec = _conv_forward(wf, kspec, mats)
    y_hy = _conv_inverse(yspec, wf, x0, hy_bias[l], mats)

    x1 = _out_projection(y_hy.reshape(bsz * n_lat, c_hy), y_lru.reshape(bsz * n_lat, c_lru), w_out_b,
                         b_out[l], x.reshape(bsz * n_lat, d), gt1, n_lat)
    x1 = x1.reshape(bsz, n_lat, d)

    h2, aff = _router(x1, g_ffn[l], sh2, sc2, w_router[l])
    idx, gates, dest, span = _select_tokens(aff, cap)
    rows_g = (idx + (jnp.arange(bsz, dtype=jnp.int32) * n_lat)[:, None, None])
    expert_major = lambda v: jnp.swapaxes(v, 0, 1).reshape(-1)
    xs = _gather_rows(expert_major(rows_g), h2)
    hmid = _expert_up(xs, w_exp_gate[l], w_exp_up[l], bsz * cap)
    ys = _expert_down(expert_major(dest), hmid, w_exp_down[l], expert_major(gates).reshape(-1, 1), bsz * cap)
    slots = n_exp * cap
    tt = _tile(n_lat, COMBINE_ROWS)
    base = (jnp.arange(bsz, dtype=jnp.int32) * slots)[:, None]
    bounds = jnp.concatenate([span[:, 0, ::tt].astype(jnp.int32) + base, base + slots], axis=1).reshape(-1)
    return _combine(bounds, jnp.swapaxes(span, 1, 2), ys, x1, gt2, g_final, slots)
```

```python
import functools
import math

import jax
import jax.numpy as jnp
from jax import lax
from jax.experimental import pallas as pl
from jax.experimental.pallas import tpu as pltpu

F32 = jnp.float32
BF16 = jnp.bfloat16

GRID_W = 64
HY_SHORT_LEFT = 1
FILT_BANDS = 16
DECAY_TARGET = 1e-2
MIN_DECAY = math.log(DECAY_TARGET) / 1.5
MAX_DECAY = math.log(DECAY_TARGET) / 0.3
LRU_HEADS = 16
LRU_CONV_LEFT = 2
LRU_C = 8.0
CAPACITY_FACTOR = 2
N_MOD = 6
EPS = 1e-6

LANES = 128
SUBLANES = 8
VMEM_LIMIT = 56 * 2 ** 20


def _params(sem):
    return pltpu.CompilerParams(dimension_semantics=sem, vmem_limit_bytes=VMEM_LIMIT)


def _tile(n, pref):
    t = min(n, pref)
    while n % t:
        t //= 2
    return t


def _split_bf16(v):
    hi = v.astype(BF16)
    lo = (v - hi.astype(F32)).astype(BF16)
    return hi, lo


def _mod_kernel(c_ref, w_ref, b_ref, o_ref):
    c = c_ref[...]
    s = c * jax.nn.sigmoid(c)
    s_hi, s_lo = _split_bf16(s)
    w_hi, w_lo = _split_bf16(w_ref[...])
    rows = c.shape[0]
    r = jnp.dot(jnp.concatenate([s_hi, s_lo], axis=0), w_hi, preferred_element_type=F32)
    acc = r[:rows] + r[rows:] + jnp.dot(s_hi, w_lo, preferred_element_type=F32)
    o_ref[...] = acc + b_ref[...]


def _modulation(c_all, w_mod, b_mod):
    rows, d = c_all.shape
    n = w_mod.shape[1]
    tn = _tile(n, 256)
    return pl.pallas_call(
        _mod_kernel,
        grid=(n // tn,),
        in_specs=[pl.BlockSpec((rows, d), lambda j: (0, 0)),
                  pl.BlockSpec((d, tn), lambda j: (0, j)),
                  pl.BlockSpec((1, tn), lambda j: (0, j))],
        out_specs=pl.BlockSpec((rows, tn), lambda j: (0, j)),
        out_shape=jax.ShapeDtypeStruct((rows, n), F32),
        compiler_params=_params(("parallel",)),
        name="modulation",
    )(c_all, w_mod, b_mod.reshape(1, n))


def _rms_mod(x, g, shift, scale):
    y = x * lax.rsqrt(jnp.mean(x * x, axis=-1, keepdims=True) + EPS) * g
    return y * (1.0 + scale) + shift


def _norm_kernel(x_ref, g_ref, sh_ref, sc_ref, o_ref):
    o_ref[0] = _rms_mod(x_ref[0], g_ref[...], sh_ref[0], sc_ref[0]).astype(o_ref.dtype)


def _norm_mod(x, g, shift, scale, per_batch):
    b, t, d = x.shape
    tt = _tile(t, 512)
    mod_map = (lambda i, j: (i, 0, 0)) if per_batch else (lambda i, j: (0, 0, 0))
    return pl.pallas_call(
        _norm_kernel,
        grid=(b, t // tt),
        in_specs=[pl.BlockSpec((1, tt, d), lambda i, j: (i, j, 0)),
                  pl.BlockSpec((1, d), lambda i, j: (0, 0)),
                  pl.BlockSpec((1, 1, d), mod_map),
                  pl.BlockSpec((1, 1, d), mod_map)],
        out_specs=pl.BlockSpec((1, tt, d), lambda i, j: (i, j, 0)),
        out_shape=jax.ShapeDtypeStruct((b, t, d), BF16),
        compiler_params=_params(("parallel", "parallel")),
        name="adaln_norm",
    )(x, g.reshape(1, d), shift, scale)


def _proj_kernel(a_ref, w_ref, b_ref, o_ref):
    acc = jnp.dot(a_ref[...], w_ref[...], preferred_element_type=F32)
    o_ref[...] = (acc + b_ref[...]).astype(o_ref.dtype)


def _projection(a, w, bias, col_start, n_cols, out_dtype):
    m, k = a.shape
    tm = _tile(m, 1024)
    tn = _tile(n_cols, 512)
    off = col_start // tn
    return pl.pallas_call(
        _proj_kernel,
        grid=(m // tm, n_cols // tn),
        in_specs=[pl.BlockSpec((tm, k), lambda i, j: (i, 0)),
                  pl.BlockSpec((k, tn), lambda i, j: (0, j + off)),
                  pl.BlockSpec((1, tn), lambda i, j: (0, j + off))],
        out_specs=pl.BlockSpec((tm, tn), lambda i, j: (i, j)),
        out_shape=jax.ShapeDtypeStruct((m, n_cols), out_dtype),
        compiler_params=_params(("parallel", "arbitrary")),
        name="projection",
    )(a, w, bias.reshape(1, -1))


def _short_conv(x, w_ref, bias, left, group):
    rows = x.shape[0]
    pos = lax.broadcasted_iota(jnp.int32, (rows, 1), 0) & (group - 1)
    y = bias + w_ref[left:left + 1, :] * x
    for k in range(w_ref.shape[0]):
        off = k - left
        if off == 0:
            continue
        shifted = pltpu.roll(x, (-off) % rows, axis=0)
        valid = (pos + off >= 0) & (pos + off < group)
        y = y + w_ref[k:k + 1, :] * jnp.where(valid, shifted, 0.0)
    return y


def _proj_hyena_kernel(a_ref, w0_ref, w1_ref, w2_ref, b0_ref, b1_ref, b2_ref,
                       cw0_ref, cw1_ref, cw2_ref, cb0_ref, cb1_ref, cb2_ref, x0_ref, wf_ref):
    a = a_ref[...]

    def branch(w_ref, b_ref, cw_ref, cb_ref):
        p = jnp.dot(a, w_ref[...], preferred_element_type=F32) + b_ref[...]
        return _short_conv(p, cw_ref, cb_ref[...], HY_SHORT_LEFT, GRID_W)

    x0_ref[...] = branch(w0_ref, b0_ref, cw0_ref, cb0_ref)
    wf_ref[...] = branch(w1_ref, b1_ref, cw1_ref, cb1_ref) * branch(w2_ref, b2_ref, cw2_ref, cb2_ref)


def _projection_hyena(a, w, bias, conv_w, conv_b, c):
    m, k = a.shape
    tm = _tile(m, 1024)
    assert tm % GRID_W == 0
    tc = _tile(c, 256)
    nc = c // tc
    kw = conv_w.shape[0]
    wspec = lambda g: pl.BlockSpec((k, tc), lambda i, j: (0, j + g * nc))
    bspec = lambda g: pl.BlockSpec((1, tc), lambda i, j: (0, j + g * nc))
    cwspec = lambda g: pl.BlockSpec((kw, tc), lambda i, j: (0, j + g * nc))
    ospec = pl.BlockSpec((tm, tc), lambda i, j: (i, j))
    b2 = bias.reshape(1, -1)
    cb = conv_b.reshape(1, -1)
    shape = jax.ShapeDtypeStruct((m, c), F32)
    return pl.pallas_call(
        _proj_hyena_kernel,
        grid=(m // tm, nc),
        in_specs=[pl.BlockSpec((tm, k), lambda i, j: (i, 0)),
                  wspec(0), wspec(1), wspec(2), bspec(0), bspec(1), bspec(2),
                  cwspec(0), cwspec(1), cwspec(2), bspec(0), bspec(1), bspec(2)],
        out_specs=[ospec, ospec],
        out_shape=[shape, shape],
        compiler_params=_params(("parallel", "arbitrary")),
        name="projection_hyena",
    )(a, w, w, w, b2, b2, b2, conv_w, conv_w, conv_w, cb, cb, cb)


def _hp_dot(a, b):
    return jnp.dot(a, b, preferred_element_type=F32, precision=lax.Precision.HIGHEST)


def _filter_kernel(n, w1t_ref, w1c_ref, w1s_ref, b1_ref, w2_ref, b2_ref, w3_ref, b3_ref,
                   wout_ref, freq_ref, delta_ref, h_ref, norm_ref):
    i = pl.program_id(0)
    tn = h_ref.shape[0]
    pos_i = i * tn + lax.broadcasted_iota(jnp.int32, (tn, 1), 0)
    pos = pos_i.astype(F32)
    t = pos * (1.0 / (n - 1))
    band_step = (FILT_BANDS - 1 - 1e-4) / (FILT_BANDS - 1)
    bands = 1e-4 + band_step * lax.broadcasted_iota(jnp.int32, (1, FILT_BANDS), 1).astype(F32)
    ang = (2.0 * math.pi * pos / n) * bands
    fr = freq_ref[...]
    pre = t * w1t_ref[...] + _hp_dot(jnp.cos(ang), w1c_ref[...]) - _hp_dot(jnp.sin(ang), w1s_ref[...])
    h = jnp.sin(fr * (pre + b1_ref[...]))
    h = jnp.sin(fr * (_hp_dot(h, w2_ref[...]) + b2_ref[...]))
    h = jnp.sin(fr * (_hp_dot(h, w3_ref[...]) + b3_ref[...]))
    taps = _hp_dot(h, wout_ref[...]) * jnp.exp(-t * delta_ref[...])

    @pl.when(i == 0)
    def _():
        norm_ref[...] = jnp.zeros_like(norm_ref)

    norm_ref[...] += jnp.sum(jnp.abs(taps), axis=0, keepdims=True)
    c = taps.shape[1] // 2
    col = lax.broadcasted_iota(jnp.int32, (1, taps.shape[1]), 1)
    drop = (pos_i == 0) & (col >= c)
    h_ref[...] = jnp.where(drop, 0.0, taps).astype(h_ref.dtype)


def _hyena_filter(n, w1, b1, w2, b2, w3, b3, wout, freq):
    hid = w1.shape[1]
    c2 = wout.shape[1]
    c = c2 // 2
    tn = _tile(n, 512)
    deltas = jnp.abs(jnp.linspace(MIN_DECAY, MAX_DECAY, c, dtype=F32))
    deltas = jnp.concatenate([deltas, deltas]).reshape(1, c2)
    full = lambda shape: pl.BlockSpec(shape, lambda i: (0, 0))
    return pl.pallas_call(
        functools.partial(_filter_kernel, n),
        grid=(n // tn,),
        in_specs=[full((1, hid)), full((FILT_BANDS, hid)), full((FILT_BANDS, hid)), full((1, hid)),
                  full((hid, hid)), full((1, hid)), full((hid, hid)), full((1, hid)),
                  full((hid, c2)), full((1, hid)), full((1, c2))],
        out_specs=[pl.BlockSpec((tn, c2), lambda i: (i, 0)), full((1, c2))],
        out_shape=[jax.ShapeDtypeStruct((n, c2), F32), jax.ShapeDtypeStruct((1, c2), F32)],
        compiler_params=_params(("arbitrary",)),
        name="hyena_filter",
    )(w1[0:1], w1[1:1 + FILT_BANDS], w1[1 + FILT_BANDS:], b1.reshape(1, hid), w2, b2.reshape(1, hid),
      w3, b3.reshape(1, hid), wout, freq.reshape(1, hid), deltas)


FFT_P = LANES
FFT_R = SUBLANES


def _phase(num, den):
    return (num % den).astype(F32) * (2.0 * math.pi / den)


def _fft_matrices(n):
    p, r = FFT_P, FFT_R
    q = n // p
    n2 = 2 * n
    iq = jnp.arange(q, dtype=jnp.int32)
    eye = jnp.eye(r, dtype=F32)
    alt = lambda v: (1 - 2 * (v & 1)).astype(F32)

    ang = _phase(iq[:, None] * iq[None, :], 2 * q)
    a_re = jnp.cos(ang)
    a_im = (-jnp.sin(ang)).at[0].set(alt(iq))
    core = jnp.stack([a_re, a_im], axis=1)
    m1 = jnp.einsum('fks,rt->fkrst', core, eye).reshape(q * 2 * r, q * r)

    b_c = jnp.cos(ang.T)
    b_s = (-jnp.sin(ang.T)).at[:, 0].set(alt(iq))
    core = jnp.stack([b_c, b_s], axis=2)
    i2 = jnp.einsum('tfk,rs->trfks', core, eye).reshape(q * r, q * 2 * r)

    h = p // 2
    f2 = jnp.arange(h, dtype=jnp.int32)
    s2 = jnp.arange(p, dtype=jnp.int32)
    f_lo = iq[:, None] + 2 * q * f2[None, :]
    f_hi = jnp.where(iq[:, None] == 0, q, 2 * q - iq[:, None]) + 2 * q * f2[None, :]
    freq = jnp.stack([f_lo, f_hi], axis=1)
    phi = _phase(freq[..., None] * s2, n2)
    c, s = jnp.cos(phi), jnp.sin(phi)
    zero = jnp.zeros_like(c[:, 0])
    first = (iq == 0)[:, None, None]
    dcrow = (first & (f2 == 0)[None, :, None])
    nyq = jnp.broadcast_to(alt(s2), c[:, 0].shape)
    on_gr = jnp.stack([jnp.where(first, c[:, 0], c[:, 0]),
                       jnp.where(dcrow, nyq, -s[:, 0]),
                       jnp.where(first, zero, c[:, 1]),
                       jnp.where(first, zero, -s[:, 1])], axis=1)
    on_gi = jnp.stack([jnp.where(first, zero, s[:, 0]),
                       jnp.where(first, zero, c[:, 0]),
                       jnp.where(first, c[:, 1], -s[:, 1]),
                       jnp.where(first, -s[:, 1], -c[:, 1])], axis=1)
    m2 = jnp.concatenate([on_gr, on_gi], axis=-1).reshape(q, 4 * h, 2 * p)
    ct, st = jnp.swapaxes(c, 2, 3), jnp.swapaxes(s, 2, 3)
    zt = jnp.zeros_like(ct[:, 0])
    dccol = (first & (f2 == 0)[None, None, :])
    nyq_t = jnp.broadcast_to(alt(s2)[:, None], ct[:, 0].shape)
    hc = jnp.stack([ct[:, 0], jnp.where(dccol, nyq_t, -st[:, 0]),
                    jnp.where(first, zt, ct[:, 1]), jnp.where(first, zt, -st[:, 1])], axis=2)
    hs = jnp.stack([jnp.where(first, zt, st[:, 0]), jnp.where(first, zt, ct[:, 0]),
                    jnp.where(first, ct[:, 1], -st[:, 1]), jnp.where(first, -st[:, 1], -ct[:, 1])], axis=2)
    i1 = jnp.stack([hc, hs], axis=1).reshape(q, 2 * p, 4 * h)
    return tuple(m.astype(BF16) for m in (m1, m2, i1, i2))


def _fft_stage_a(u_ref, m1_ref, g_ref):
    q, groups, r, tc = u_ref.shape
    m1 = m1_ref[...]
    for g in range(groups):
        blk = u_ref[:, g].reshape(q * r, tc).astype(BF16)
        g_ref[g] = jnp.dot(m1, blk, preferred_element_type=F32).reshape(q, 2, r, tc)


def _fft_stage_b(g_ref, f1, m2):
    groups, _, _, r, tc = g_ref.shape
    z = jnp.concatenate([g_ref[:, f1, 0].reshape(groups * r, tc),
                         g_ref[:, f1, 1].reshape(groups * r, tc)], axis=0).astype(BF16)
    return jnp.dot(m2, z, preferred_element_type=F32)


def _cmul_packed(x, k, first):
    h = x.shape[0] // 4
    xr = (x[0:h], x[2 * h:3 * h])
    xi = (x[h:2 * h], x[3 * h:4 * h])
    kr = (k[0:h], k[2 * h:3 * h])
    ki = (k[h:2 * h], k[3 * h:4 * h])
    real_pair = first & (lax.broadcasted_iota(jnp.int32, (h, 1), 0) == 0)
    lo_re = jnp.where(real_pair, xr[0] * kr[0], xr[0] * kr[0] - xi[0] * ki[0])
    lo_im = jnp.where(real_pair, xi[0] * ki[0], xr[0] * ki[0] + xi[0] * kr[0])
    return jnp.concatenate([lo_re, lo_im, xr[1] * kr[1] - xi[1] * ki[1], xr[1] * ki[1] + xi[1] * kr[1]], axis=0)


def _filter_spec_kernel(n, hf_ref, hb_ref, nf_ref, nb_ref, m1_ref, m2_ref, k_ref, gf_ref, gb_ref):
    qi = pl.program_id(1)
    qb = m2_ref.shape[0]

    @pl.when(qi == 0)
    def _():
        _fft_stage_a(hf_ref, m1_ref, gf_ref)
        _fft_stage_a(hb_ref, m1_ref, gb_ref)

    inv_norm = 1.0 / (nf_ref[...] + nb_ref[...])
    h = m2_ref.shape[1] // 4
    row = lax.broadcasted_iota(jnp.int32, (4 * h, 1), 0)
    for k in range(qb):
        f1 = qi * qb + k
        xf = _fft_stage_b(gf_ref, f1, m2_ref[k])
        xb = _fft_stage_b(gb_ref, f1, m2_ref[k])
        real_pair = (f1 == 0) & ((row == 0) | (row == h))
        imag_row = ((row >= h) & (row < 2 * h)) | (row >= 3 * h)
        spec = jnp.where(imag_row & jnp.logical_not(real_pair), xf - xb, xf + xb)
        spec = spec * jnp.where(real_pair, 0.5 / n, 1.0 / n) * inv_norm
        k_ref[k] = spec.reshape(k_ref.shape[1:]).astype(k_ref.dtype)


def _time_view(v, n):
    return v.reshape(v.shape[:-2] + (n // FFT_P, FFT_P // FFT_R, FFT_R, v.shape[-1]))


def _filter_spectrum(taps, norm, mats, c):
    n = taps.shape[0]
    m1, m2, _, _ = mats
    q = n // FFT_P
    h = FFT_P // 2
    tc = _tile(c, 256)
    nc = c // tc
    qb = _tile(q, 8)
    tv = _time_view(taps, n)
    blk = (q, FFT_P // FFT_R, FFT_R, tc)
    return pl.pallas_call(
        functools.partial(_filter_spec_kernel, n),
        grid=(nc, q // qb),
        in_specs=[pl.BlockSpec(blk, lambda j, i: (0, 0, 0, j)),
                  pl.BlockSpec(blk, lambda j, i: (0, 0, 0, j + nc)),
                  pl.BlockSpec((1, tc), lambda j, i: (0, j)),
                  pl.BlockSpec((1, tc), lambda j, i: (0, j + nc)),
                  pl.BlockSpec(m1.shape, lambda j, i: (0, 0)),
                  pl.BlockSpec((qb,) + m2.shape[1:], lambda j, i: (i, 0, 0))],
        out_specs=pl.BlockSpec((qb, 4, h, tc), lambda j, i: (i, 0, 0, j)),
        out_shape=jax.ShapeDtypeStruct((q, 4, h, c), BF16),
        scratch_shapes=[pltpu.VMEM((FFT_P // FFT_R, q, 2, FFT_R, tc), F32) for _ in range(2)],
        compiler_params=_params(("parallel", "arbitrary")),
        name="filter_spectrum",
    )(tv, tv, norm, norm, m1, m2)


def _conv_fwd_kernel(w_ref, m1_ref, m2_ref, k_ref, y_ref, g_ref):
    qi = pl.program_id(2)
    qb = m2_ref.shape[0]

    @pl.when(qi == 0)
    def _():
        _fft_stage_a(w_ref.at[0], m1_ref, g_ref)

    for k in range(qb):
        f1 = qi * qb + k
        x = _fft_stage_b(g_ref, f1, m2_ref[k])
        y = _cmul_packed(x, k_ref[k].reshape(x.shape).astype(F32), f1 == 0)
        y_ref[0, k] = y.reshape(y_ref.shape[2:]).astype(y_ref.dtype)


def _conv_forward(wf, kspec, mats):
    b, n, c = wf.shape
    m1, m2, _, _ = mats
    q = n // FFT_P
    h = FFT_P // 2
    tc = _tile(c, 256)
    qb = _tile(q, 8)
    return pl.pallas_call(
        _conv_fwd_kernel,
        grid=(c // tc, b, q // qb),
        in_specs=[pl.BlockSpec((1, q, FFT_P // FFT_R, FFT_R, tc), lambda j, bi, i: (bi, 0, 0, 0, j)),
                  pl.BlockSpec(m1.shape, lambda j, bi, i: (0, 0)),
                  pl.BlockSpec((qb,) + m2.shape[1:], lambda j, bi, i: (i, 0, 0)),
                  pl.BlockSpec((qb, 4, h, tc), lambda j, bi, i: (i, 0, 0, j))],
        out_specs=pl.BlockSpec((1, qb, 4, h, tc), lambda j, bi, i: (bi, i, 0, 0, j)),
        out_shape=jax.ShapeDtypeStruct((b, q, 4, h, c), BF16),
        scratch_shapes=[pltpu.VMEM((FFT_P // FFT_R, q, 2, FFT_R, tc), F32)],
        compiler_params=_params(("parallel", "parallel", "arbitrary")),
        name="conv_forward_fft",
    )(_time_view(wf, n), m1, m2, kspec)


def _conv_inv_kernel(y_ref, i1_ref, i2_ref, wf_ref, x0_ref, bias_ref, o_ref, h_ref):
    qi = pl.program_id(2)
    qb = i1_ref.shape[0]
    groups, q, _, r, tc = h_ref.shape
    for k in range(qb):
        hv = jnp.dot(i1_ref[k], y_ref[0, k].reshape(i1_ref.shape[2], tc), preferred_element_type=F32)
        half = hv.shape[0] // 2
        h_ref[:, qi * qb + k, 0] = hv[:half].reshape(groups, r, tc)
        h_ref[:, qi * qb + k, 1] = hv[half:].reshape(groups, r, tc)

    @pl.when(qi == pl.num_programs(2) - 1)
    def _():
        i2 = i2_ref[...]
        bias = bias_ref[...]
        for g2 in range(groups // 2):
            parts = []
            for g in (2 * g2, 2 * g2 + 1):
                z = jnp.dot(i2, h_ref[g].reshape(q * 2 * r, tc).astype(BF16), preferred_element_type=F32)
                z = z.reshape(q, r, tc)
                parts.append(x0_ref[0, :, g] * (z + wf_ref[0, :, g] * bias))
            o_ref[0, :, g2] = jnp.concatenate(parts, axis=1).astype(o_ref.dtype)


def _conv_inverse(yspec, wf, x0, bias, mats):
    b, n, c = wf.shape
    _, _, i1, i2 = mats
    q = n // FFT_P
    h = FFT_P // 2
    groups = FFT_P // FFT_R
    tc = _tile(c, 256)
    qb = _tile(q, 8)
    tspec = pl.BlockSpec((1, q, groups, FFT_R, tc), lambda j, bi, i: (bi, 0, 0, 0, j))
    out = pl.pallas_call(
        _conv_inv_kernel,
        grid=(c // tc, b, q // qb),
        in_specs=[pl.BlockSpec((1, qb, 4, h, tc), lambda j, bi, i: (bi, i, 0, 0, j)),
                  pl.BlockSpec((qb,) + i1.shape[1:], lambda j, bi, i: (i, 0, 0)),
                  pl.BlockSpec(i2.shape, lambda j, bi, i: (0, 0)),
                  tspec, tspec,
                  pl.BlockSpec((1, tc), lambda j, bi, i: (0, j))],
        out_specs=pl.BlockSpec((1, q, groups // 2, 2 * FFT_R, tc), lambda j, bi, i: (bi, 0, 0, 0, j)),
        out_shape=jax.ShapeDtypeStruct((b, q, groups // 2, 2 * FFT_R, c), BF16),
        scratch_shapes=[pltpu.VMEM((groups, q, 2, FFT_R, tc), F32)],
        compiler_params=_params(("parallel", "parallel", "arbitrary")),
        name="conv_inverse_fft",
    )(yspec, i1, i2, _time_view(wf, n), _time_view(x0, n), bias.reshape(1, c))
    return out.reshape(b, n, c)


SCAN_UNROLL = 4


def _tile_scan(a, b, reverse):
    rows, c = a.shape
    a3 = a.reshape(rows // SUBLANES, SUBLANES, c)
    b3 = b.reshape(rows // SUBLANES, SUBLANES, c)
    sub = lax.broadcasted_iota(jnp.int32, (1, SUBLANES, 1), 1)
    for k in (1, 2, 4):
        shift = SUBLANES - k if reverse else k
        valid = (sub < SUBLANES - k) if reverse else (sub >= k)
        b3 = b3 + a3 * jnp.where(valid, pltpu.roll(b3, shift, axis=1), 0.0)
        a3 = a3 * jnp.where(valid, pltpu.roll(a3, shift, axis=1), 1.0)
    return a3.reshape(rows, c), b3.reshape(rows, c)


def _lru_kernel(group, has_gate, *refs):
    if has_gate:
        (x_ref, gate_ref, cw_ref, cb_ref, wa_ref, ba_ref, wx_ref, bx_ref, lam_ref, h0_ref,
         y_ref, ht_ref, af_ref, bf_ref, ab_ref, bb_ref) = refs
    else:
        (x_ref, cw_ref, cb_ref, wa_ref, ba_ref, wx_ref, bx_ref, lam_ref, h0_ref,
         ht_ref, af_ref, bf_ref, ab_ref, bb_ref) = refs
    t_len = x_ref.shape[1]
    chunk = _tile(t_len, 512)
    a_refs = (af_ref, ab_ref)
    b_refs = (bf_ref, bb_ref)

    def coeffs(ci, carry):
        r0 = pl.multiple_of(ci * chunk, chunk)
        xc = _short_conv(x_ref[0, pl.ds(r0, chunk), :], cw_ref, cb_ref[...], LRU_CONV_LEFT, group)
        xcb = xc.astype(BF16)
        half_xc = 0.5 * xc
        for d in range(2):
            ta = jnp.tanh(jnp.dot(xcb, wa_ref[d, 0], preferred_element_type=F32) + ba_ref[d])
            ti = jnp.tanh(jnp.dot(xcb, wx_ref[d, 0], preferred_element_type=F32) + bx_ref[d])
            lam = lam_ref[d]
            softplus_neg = jnp.maximum(-lam, 0.0) + jnp.log(1.0 + jnp.exp(-jnp.abs(lam)))
            c1 = (-0.5 * LRU_C) * softplus_neg
            a = jnp.exp(c1 * ta + c1)
            a_tile, b_tile = _tile_scan(a, jnp.sqrt(1.0 - a * a) * (half_xc * ti + half_xc), d == 1)
            a_refs[d][pl.ds(r0, chunk), :] = a_tile
            b_refs[d][pl.ds(r0, chunk), :] = b_tile
        return carry

    lax.fori_loop(0, t_len // chunk, coeffs, 0)

    n_tiles = t_len // SUBLANES
    unroll = _tile(n_tiles, SCAN_UNROLL)

    def scan(i, carry):
        hf, hb = carry
        for u in range(unroll):
            kf = i * unroll + u
            rf = pl.multiple_of(kf * SUBLANES, SUBLANES)
            tile_f = af_ref[pl.ds(rf, SUBLANES), :] * hf + bf_ref[pl.ds(rf, SUBLANES), :]
            bf_ref[pl.ds(rf, SUBLANES), :] = tile_f
            hf = tile_f[SUBLANES - 1:SUBLANES, :]
            rb = pl.multiple_of((n_tiles - 1 - kf) * SUBLANES, SUBLANES)
            tile_b = ab_ref[pl.ds(rb, SUBLANES), :] * hb + bb_ref[pl.ds(rb, SUBLANES), :]
            bb_ref[pl.ds(rb, SUBLANES), :] = tile_b
            hb = tile_b[0:1, :]
        return hf, hb

    hf, hb = lax.fori_loop(0, n_tiles // unroll, scan, (h0_ref[0, 0:1, :], h0_ref[0, 1:2, :]))
    ht_ref[0, 0:1, :] = hf
    ht_ref[0, 1:2, :] = hb

    if has_gate:
        def emit(ci, carry):
            r0 = pl.multiple_of(ci * chunk, chunk)
            hs = bf_ref[pl.ds(r0, chunk), :] + bb_ref[pl.ds(r0, chunk), :]
            y_ref[0, pl.ds(r0, chunk), :] = (jax.nn.gelu(gate_ref[0, pl.ds(r0, chunk), :]) * hs).astype(y_ref.dtype)
            return carry

        lax.fori_loop(0, t_len // chunk, emit, 0)


def _block_diag(w, heads_per_tile):
    d2, h, hd, _ = w.shape
    w = w.reshape(d2, h // heads_per_tile, heads_per_tile, hd, hd)
    eye = jnp.eye(heads_per_tile, dtype=w.dtype)
    bd = jnp.einsum('dghij,hq->dghiqj', w, eye)
    return bd.reshape(d2, h // heads_per_tile, heads_per_tile * hd, heads_per_tile * hd).astype(BF16)


def _rglru(xsrc, x_col, gsrc, g_col, conv_w, conv_b, wa, ba, wx, bx, lam, h0, group):
    b, t, _ = xsrc.shape
    c = conv_w.shape[1]
    hd = c // LRU_HEADS
    tc = min(c, max(hd, 256))
    hp = tc // hd
    nt = c // tc
    has_gate = gsrc is not None
    wa_bd = _block_diag(0.5 * wa, hp)
    wx_bd = _block_diag(0.5 * wx, hp)
    ba, bx = 0.5 * ba, 0.5 * bx
    xo, go = x_col // tc, (g_col // tc if has_gate else 0)
    vec = lambda: pl.BlockSpec((2, 1, tc), lambda i, j: (0, 0, j))
    mat = lambda: pl.BlockSpec((2, 1, tc, tc), lambda i, j: (0, j, 0, 0))
    in_specs = [pl.BlockSpec((1, t, tc), lambda i, j: (i, 0, j + xo))]
    args = [xsrc]
    if has_gate:
        in_specs.append(pl.BlockSpec((1, t, tc), lambda i, j: (i, 0, j + go)))
        args.append(gsrc)
    in_specs += [pl.BlockSpec((conv_w.shape[0], tc), lambda i, j: (0, j)),
                 pl.BlockSpec((1, tc), lambda i, j: (0, j)),
                 mat(), vec(), mat(), vec(), vec(),
                 pl.BlockSpec((1, 2, tc), lambda i, j: (i, 0, j))]
    args += [conv_w, conv_b.reshape(1, c), wa_bd, ba.reshape(2, 1, c), wx_bd, bx.reshape(2, 1, c),
             lam.reshape(2, 1, c), h0]
    ht_spec = pl.BlockSpec((1, 2, tc), lambda i, j: (i, 0, j))
    ht_shape = jax.ShapeDtypeStruct((b, 2, c), F32)
    if has_gate:
        out_specs = [pl.BlockSpec((1, t, tc), lambda i, j: (i, 0, j)), ht_spec]
        out_shape = [jax.ShapeDtypeStruct((b, t, c), BF16), ht_shape]
    else:
        out_specs = [ht_spec]
        out_shape = [ht_shape]
    return pl.pallas_call(
        functools.partial(_lru_kernel, group, has_gate),
        grid=(b, nt),
        in_specs=in_specs,
        out_specs=out_specs,
        out_shape=out_shape,
        scratch_shapes=[pltpu.VMEM((t, tc), F32) for _ in range(4)],
        compiler_params=_params(("parallel", "parallel")),
        name="rglru" if has_gate else "rglru_context",
    )(*args)


def _out_proj_kernel(a1_ref, a2_ref, w1_ref, w2_ref, b_ref, x_ref, g_ref, o_ref):
    acc = jnp.dot(a1_ref[...], w1_ref[...], preferred_element_type=F32)
    acc = acc + jnp.dot(a2_ref[...], w2_ref[...], preferred_element_type=F32)
    o_ref[...] = x_ref[...] + g_ref[0] * (acc + b_ref[...])


def _out_projection(a1, a2, w, bias, x, gate, t):
    m, k1 = a1.shape
    n = w.shape[1]
    tm = _tile(t, 1024)
    tn = _tile(n, 512)
    per_b = t // tm
    k1_blocks = 1
    return pl.pallas_call(
        _out_proj_kernel,
        grid=(m // tm, n // tn),
        in_specs=[pl.BlockSpec((tm, k1), lambda i, j: (i, 0)),
                  pl.BlockSpec((tm, k1), lambda i, j: (i, 0)),
                  pl.BlockSpec((k1, tn), lambda i, j: (0, j)),
                  pl.BlockSpec((k1, tn), lambda i, j: (k1_blocks, j)),
                  pl.BlockSpec((1, tn), lambda i, j: (0, j)),
                  pl.BlockSpec((tm, tn), lambda i, j: (i, j)),
                  pl.BlockSpec((1, 1, tn), lambda i, j: (i // per_b, 0, j))],
        out_specs=pl.BlockSpec((tm, tn), lambda i, j: (i, j)),
        out_shape=jax.ShapeDtypeStruct((m, n), F32),
        compiler_params=_params(("parallel", "arbitrary")),
        name="out_projection",
    )(a1, a2, w, w, bias.reshape(1, n), x, gate)


def _router_kernel(x_ref, g_ref, sh_ref, sc_ref, wr_ref, h_ref, aff_ref):
    h = _rms_mod(x_ref[0], g_ref[...], sh_ref[0], sc_ref[0])
    half = h.shape[1] // 2
    h_ref[...] = pltpu.pack_elementwise([h[:, :half], h[:, half:]], packed_dtype=BF16)
    h_hi, h_lo = _split_bf16(h)
    w_hi, w_lo = _split_bf16(wr_ref[...])
    nt = (((1,), (1,)), ((), ()))
    logits = (lax.dot_general(w_hi, h_hi, nt, preferred_element_type=F32)
              + lax.dot_general(w_lo, h_hi, nt, preferred_element_type=F32)
              + lax.dot_general(w_hi, h_lo, nt, preferred_element_type=F32))
    z = jnp.exp(logits - jnp.max(logits, axis=0, keepdims=True))
    aff_ref[0] = z / jnp.sum(z, axis=0, keepdims=True)


def _router(x1, g, shift, scale, w_router):
    b, t, d = x1.shape
    e = w_router.shape[1]
    tt = _tile(t, 256)
    per_b = t // tt
    mod_map = lambda i, j: (i, 0, 0)
    return pl.pallas_call(
        _router_kernel,
        grid=(b, per_b),
        in_specs=[pl.BlockSpec((1, tt, d), lambda i, j: (i, j, 0)),
                  pl.BlockSpec((1, d), lambda i, j: (0, 0)),
                  pl.BlockSpec((1, 1, d), mod_map),
                  pl.BlockSpec((1, 1, d), mod_map),
                  pl.BlockSpec((e, d), lambda i, j: (0, 0))],
        out_specs=[pl.BlockSpec((tt, d // 2), lambda i, j: (i * per_b + j, 0)),
                   pl.BlockSpec((1, e, tt), lambda i, j: (i, 0, j))],
        out_shape=[jax.ShapeDtypeStruct((b * t, d // 2), jnp.uint32),
                   jax.ShapeDtypeStruct((b, e, t), F32)],
        compiler_params=_params(("parallel", "parallel")),
        name="router",
    )(x1, g.reshape(1, d), shift, scale, w_router.T)


def _prefix_count(mask_ref, out_ref):
    e, t = mask_ref.shape
    blk = min(t, LANES)
    tri = (lax.broadcasted_iota(jnp.int32, (blk, blk), 0)
           < lax.broadcasted_iota(jnp.int32, (blk, blk), 1)).astype(BF16)
    carry = jnp.zeros((e, 1), F32)
    for k in range(t // blk):
        m = mask_ref[:, k * blk:(k + 1) * blk]
        out_ref[:, k * blk:(k + 1) * blk] = jnp.dot(m.astype(BF16), tri, preferred_element_type=F32) + carry
        carry = carry + jnp.sum(m, axis=1, keepdims=True)
    return carry


def _topk_kernel(cap, aff_ref, idx_ref, gate_ref, dest_ref, span_ref, mask_ref, pos_ref, cnt_ref, q_ref):
    a = aff_ref[0]
    e, t = a.shape
    min_normal = 0x00800000

    def refine(i, thr):
        cand = thr | jnp.left_shift(jnp.int32(1), 30 - i)
        cnt = jnp.sum(jnp.where(a >= pltpu.bitcast(cand, F32), 1.0, 0.0), axis=1, keepdims=True)
        return jnp.where((cnt >= cap) & (cand >= min_normal), cand, thr)

    thr = lax.fori_loop(0, 31, refine, jnp.zeros((e, 1), jnp.int32))
    above = a >= pltpu.bitcast(jnp.maximum(thr + 1, min_normal), F32)
    tied = (a >= pltpu.bitcast(thr, F32)) & jnp.logical_not(above)
    need = cap - jnp.sum(jnp.where(above, 1.0, 0.0), axis=1, keepdims=True)
    mask_ref[...] = jnp.where(tied, 1.0, 0.0)
    _prefix_count(mask_ref, pos_ref)
    sel = above | (tied & (pos_ref[...] < need))
    mask_ref[...] = jnp.where(sel, 1.0, 0.0)
    _prefix_count(mask_ref, pos_ref)

    mask = mask_ref[...]
    cnt = jnp.sum(mask, axis=0, keepdims=True)
    cnt_ref[...] = jnp.broadcast_to(cnt, cnt_ref.shape)
    _prefix_count(cnt_ref, q_ref.at[0:SUBLANES])
    first = q_ref[0:1, :]
    srow = lax.broadcasted_iota(jnp.int32, (SUBLANES, 1), 0)
    span_ref[0] = jnp.where(srow == 0, first, jnp.where(srow == 1, first + cnt, 0.0))
    lower = (lax.broadcasted_iota(jnp.int32, (e, e), 1)
             < lax.broadcasted_iota(jnp.int32, (e, e), 0)).astype(BF16)
    q_ref[...] = first + jnp.dot(lower, mask.astype(BF16), preferred_element_type=F32)

    tok = lax.broadcasted_iota(jnp.int32, (1, t), 1)
    tok_hi = (tok >> 6).astype(F32)
    tok_lo = (tok & 63).astype(F32)
    slot = lax.broadcasted_iota(jnp.int32, (cap, 1), 0).astype(F32)
    vrow = lax.broadcasted_iota(jnp.int32, (SUBLANES, 1), 0)
    dest_base = pl.program_id(0) * (e * cap)

    def compact(ei, carry):
        g = aff_ref[0, pl.ds(ei, 1), :]
        g_hi = g.astype(BF16).astype(F32)
        g_mid = (g - g_hi).astype(BF16).astype(F32)
        g_lo = g - g_hi - g_mid
        q = q_ref[pl.ds(ei, 1), :]
        q_hi = jnp.floor(q * (1.0 / 64.0))
        q_lo = q - 64.0 * q_hi
        vals = jnp.where(vrow == 0, tok_hi, jnp.where(vrow == 1, tok_lo, jnp.where(
            vrow == 2, g_hi, jnp.where(vrow == 3, g_mid, jnp.where(vrow == 4, g_lo, jnp.where(
                vrow == 5, q_hi, jnp.where(vrow == 6, q_lo, 0.0))))))).astype(BF16)
        hit = (pos_ref[pl.ds(ei, 1), :] == slot) & (mask_ref[pl.ds(ei, 1), :] > 0.0)
        onehot = jnp.where(hit, 1.0, 0.0).astype(BF16)
        res = lax.dot_general(vals, onehot, (((1,), (1,)), ((), ())), preferred_element_type=F32)
        idx_ref[0, pl.ds(ei, 1), :] = (res[0:1] * 64.0 + res[1:2]).astype(jnp.int32)
        gate_ref[0, pl.ds(ei, 1), :] = res[2:3] + res[3:4] + res[4:5]
        dest_ref[0, pl.ds(ei, 1), :] = (res[5:6] * 64.0 + res[6:7]).astype(jnp.int32) + dest_base
        return carry

    lax.fori_loop(0, e, compact, 0)


def _select_tokens(aff, cap):
    b, e, t = aff.shape
    assert t <= 64 * 256 and e * cap <= 64 * 256 and e >= SUBLANES
    spec = pl.BlockSpec((1, e, cap), lambda i: (i, 0, 0))
    return pl.pallas_call(
        functools.partial(_topk_kernel, cap),
        grid=(b,),
        in_specs=[pl.BlockSpec((1, e, t), lambda i: (i, 0, 0))],
        out_specs=[spec, spec, spec, pl.BlockSpec((1, SUBLANES, t), lambda i: (i, 0, 0))],
        out_shape=[jax.ShapeDtypeStruct((b, e, cap), jnp.int32), jax.ShapeDtypeStruct((b, e, cap), F32),
                   jax.ShapeDtypeStruct((b, e, cap), jnp.int32), jax.ShapeDtypeStruct((b, SUBLANES, t), F32)],
        scratch_shapes=[pltpu.VMEM((e, t), F32), pltpu.VMEM((e, t), F32),
                        pltpu.VMEM((SUBLANES, t), F32), pltpu.VMEM((e, t), F32)],
        compiler_params=_params(("parallel",)),
        name="select_tokens",
    )(aff)


def _gather_kernel(rows_ref, h_hbm, o_ref, buf, sem):
    i = pl.program_id(0)
    r_blk = buf.shape[1]

    def issue_block(blk, slot):
        def issue(r, carry):
            row = rows_ref[blk * r_blk + r]
            pltpu.make_async_copy(h_hbm.at[pl.ds(row, 1), :], buf.at[slot, pl.ds(r, 1), :], sem.at[slot]).start()
            return carry

        lax.fori_loop(0, r_blk, issue, 0, unroll=8)

    @pl.when(i == 0)
    def _():
        issue_block(0, 0)

    @pl.when(i + 1 < pl.num_programs(0))
    def _():
        issue_block(i + 1, (i + 1) & 1)

    slot = i & 1
    pltpu.make_async_copy(h_hbm.at[pl.ds(0, r_blk), :], buf.at[slot], sem.at[slot]).wait()
    words = buf[slot]
    half = words.shape[1]
    for h in range(2):
        part = pltpu.unpack_elementwise(words, index=h, packed_dtype=BF16, unpacked_dtype=F32)
        o_ref[:, h * half:(h + 1) * half] = part.astype(o_ref.dtype)


def _gather_rows(rows, h):
    n_rows = rows.shape[0]
    words = h.shape[1]
    d = 2 * words
    r_blk = _tile(n_rows, 256)
    grid_spec = pltpu.PrefetchScalarGridSpec(
        num_scalar_prefetch=1,
        grid=(n_rows // r_blk,),
        in_specs=[pl.BlockSpec(memory_space=pl.ANY)],
        out_specs=pl.BlockSpec((r_blk, d), lambda i, rows_ref: (i, 0)),
        scratch_shapes=[pltpu.VMEM((2, r_blk, words), h.dtype), pltpu.SemaphoreType.DMA((2,))],
    )
    return pl.pallas_call(
        _gather_kernel,
        grid_spec=grid_spec,
        out_shape=jax.ShapeDtypeStruct((n_rows, d), BF16),
        compiler_params=_params(("arbitrary",)),
        name="gather_rows",
    )(rows, h)


def _expert_up_kernel(xs_ref, wg_ref, wu_ref, h_ref):
    xs = xs_ref[...]
    g = jnp.dot(xs, wg_ref[0].astype(BF16), preferred_element_type=F32)
    u = jnp.dot(xs, wu_ref[0].astype(BF16), preferred_element_type=F32)
    h_ref[...] = (g * jax.nn.sigmoid(g) * u).astype(h_ref.dtype)


def _expert_up(xs, w_gate, w_up, rows_per_expert):
    m, d = xs.shape
    f = w_gate.shape[2]
    tm = _tile(rows_per_expert, 1024)
    per_e = rows_per_expert // tm
    tf = _tile(f, 256)
    wspec = pl.BlockSpec((1, d, tf), lambda i, j: (i // per_e, 0, j))
    return pl.pallas_call(
        _expert_up_kernel,
        grid=(m // tm, f // tf),
        in_specs=[pl.BlockSpec((tm, d), lambda i, j: (i, 0)), wspec, wspec],
        out_specs=pl.BlockSpec((tm, tf), lambda i, j: (i, j)),
        out_shape=jax.ShapeDtypeStruct((m, f), BF16),
        compiler_params=_params(("parallel", "arbitrary")),
        name="expert_up",
    )(xs, w_gate, w_up)


DOWN_COL_TILE = 1024


def _expert_down_kernel(dest_ref, h_ref, wd_ref, g_ref, ys_hbm, ybuf, sem):
    i = pl.program_id(0)
    j = pl.program_id(1)
    n_i = pl.num_programs(0)
    n_j = pl.num_programs(1)
    _, tm, dw = ybuf.shape
    hw = wd_ref.shape[2] // 2
    slot = i & 1

    def wait_scatter(s):
        pltpu.make_async_copy(ybuf.at[s], ys_hbm.at[pl.ds(0, tm), :], sem.at[s]).wait()

    @pl.when((j == 0) & (i >= 2))
    def _():
        wait_scatter(slot)

    y = jnp.dot(h_ref[...], wd_ref[0].astype(BF16), preferred_element_type=F32) * g_ref[...]
    packed = pltpu.pack_elementwise([y[:, :hw], y[:, hw:]], packed_dtype=BF16)
    for jj in range(dw // hw):
        @pl.when(j == jj)
        def _(jj=jj):
            ybuf[slot, :, jj * hw:(jj + 1) * hw] = packed

    @pl.when(j == n_j - 1)
    def _():
        def issue(r, carry):
            dst = dest_ref[i * tm + r]
            pltpu.make_async_copy(ybuf.at[slot, pl.ds(r, 1), :], ys_hbm.at[pl.ds(dst, 1), :], sem.at[slot]).start()
            return carry

        lax.fori_loop(0, tm, issue, 0, unroll=8)

        @pl.when(i == n_i - 1)
        def _():
            @pl.when(i >= 1)
            def _():
                wait_scatter(1 - slot)
            wait_scatter(slot)


def _expert_down(dest, h, w_down, gates, rows_per_expert):
    m, f = h.shape
    d = w_down.shape[2]
    tm = _tile(rows_per_expert, 1024)
    per_e = rows_per_expert // tm
    td = _tile(d, DOWN_COL_TILE)
    grid_spec = pltpu.PrefetchScalarGridSpec(
        num_scalar_prefetch=1,
        grid=(m // tm, d // td),
        in_specs=[pl.BlockSpec((tm, f), lambda i, j, dest_ref: (i, 0)),
                  pl.BlockSpec((1, f, td), lambda i, j, dest_ref: (i // per_e, 0, j)),
                  pl.BlockSpec((tm, 1), lambda i, j, dest_ref: (i, 0))],
        out_specs=pl.BlockSpec(memory_space=pl.ANY),
        scratch_shapes=[pltpu.VMEM((2, tm, d // 2), jnp.uint32), pltpu.SemaphoreType.DMA((2,))],
    )
    return pl.pallas_call(
        _expert_down_kernel,
        grid_spec=grid_spec,
        out_shape=jax.ShapeDtypeStruct((m, d // 2), jnp.uint32),
        compiler_params=_params(("arbitrary", "arbitrary")),
        name="expert_down",
    )(dest, h, w_down, gates)


def _combine_kernel(slots_per_batch, hw, tb_ref, span_ref, x_ref, gt_ref, g_ref, ys_hbm, o_ref,
                    buf, sem, acc_ref, parts_ref):
    bi = pl.program_id(0)
    i = pl.program_id(1)
    n_t = pl.num_programs(1)
    kc = buf.shape[1]
    total = ys_hbm.shape[0]
    tt = acc_ref.shape[1]
    n_sub = tt // COMBINE_SUB
    tb0 = bi * (n_t * n_sub + 1) + i * n_sub
    c_lo = tb_ref[tb0]
    c_hi = tb_ref[tb0 + n_sub]
    k_first = (c_lo >> 3) << 3
    n_chunks = (c_hi - k_first + kc - 1) // kc

    def chunk_start(c):
        return pl.multiple_of(jnp.minimum(k_first + c * kc, total - kc), SUBLANES)

    def copy(c, slot):
        return pltpu.make_async_copy(ys_hbm.at[pl.ds(chunk_start(c), kc), :], buf.at[slot], sem.at[slot])

    @pl.when(n_chunks > 0)
    def _():
        copy(0, 0).start()

    base = (bi * slots_per_batch).astype(F32)
    acc_ref[...] = jnp.zeros_like(acc_ref)

    def body(c, carry):
        slot = c & 1
        copy(c, slot).wait()

        @pl.when(c + 1 < n_chunks)
        def _():
            copy(c + 1, 1 - slot).start()

        k_lo = k_first + c * kc
        k_hi = chunk_start(c) + kc
        k = chunk_start(c) + lax.broadcasted_iota(jnp.int32, (1, kc), 1)
        k = jnp.where(k >= k_lo, k, -1).astype(F32)
        words = buf[slot]
        for half in range(2):
            part = pltpu.unpack_elementwise(words, index=half, packed_dtype=BF16, unpacked_dtype=F32)
            parts_ref[half] = part.astype(BF16)
        for s in range(n_sub):
            rows = slice(s * COMBINE_SUB, (s + 1) * COMBINE_SUB)

            @pl.when((tb_ref[tb0 + s] < k_hi) & (tb_ref[tb0 + s + 1] > k_lo))
            def _(rows=rows):
                first = span_ref[0, rows, 0:1] + base
                last = span_ref[0, rows, 1:2] + base
                onehot = jnp.where((k >= first) & (k < last), 1.0, 0.0).astype(BF16)
                for half in range(2):
                    acc_ref[half, rows] += jnp.dot(onehot, parts_ref[half], preferred_element_type=F32)
        return carry

    lax.fori_loop(0, n_chunks, body, 0)
    gate = gt_ref[0]
    g_final = g_ref[...]

    def finish(r, carry):
        rows = pl.ds(pl.multiple_of(r * FINISH_ROWS, FINISH_ROWS), FINISH_ROWS)
        lo, hi = acc_ref[0, rows, :], acc_ref[1, rows, :]
        moe = jnp.concatenate([part[:, j * hw:(j + 1) * hw] for j in range(lo.shape[1] // hw)
                               for part in (lo, hi)], axis=1)
        x = x_ref[0, rows, :] + gate * moe
        o_ref[0, rows, :] = x * lax.rsqrt(jnp.mean(x * x, axis=-1, keepdims=True) + EPS) * g_final
        return carry

    lax.fori_loop(0, tt // FINISH_ROWS, finish, 0)


COMBINE_ROWS = 512
COMBINE_SUB = 128
COMBINE_CHUNK = 256
FINISH_ROWS = 64


def _combine(tile_bounds, span, ys, x1, gate, g_final, slots_per_batch):
    b, t, d = x1.shape
    tt = _tile(t, COMBINE_ROWS)
    assert tt % COMBINE_SUB == 0
    kc = min(COMBINE_CHUNK, ys.shape[0])
    grid_spec = pltpu.PrefetchScalarGridSpec(
        num_scalar_prefetch=1,
        grid=(b, t // tt),
        in_specs=[pl.BlockSpec((1, tt, SUBLANES), lambda bi, i, tb: (bi, i, 0)),
                  pl.BlockSpec((1, tt, d), lambda bi, i, tb: (bi, i, 0)),
                  pl.BlockSpec((1, 1, d), lambda bi, i, tb: (bi, 0, 0)),
                  pl.BlockSpec((1, d), lambda bi, i, tb: (0, 0)),
                  pl.BlockSpec(memory_space=pl.ANY)],
        out_specs=pl.BlockSpec((1, tt, d), lambda bi, i, tb: (bi, i, 0)),
        scratch_shapes=[pltpu.VMEM((2, kc, d // 2), jnp.uint32), pltpu.SemaphoreType.DMA((2,)),
                        pltpu.VMEM((2, tt, d // 2), F32), pltpu.VMEM((2, kc, d // 2), BF16)],
    )
    return pl.pallas_call(
        functools.partial(_combine_kernel, slots_per_batch, _tile(d, DOWN_COL_TILE) // 2),
        grid_spec=grid_spec,
        out_shape=jax.ShapeDtypeStruct((b, t, d), F32),
        compiler_params=_params(("arbitrary", "arbitrary")),
        name="combine",
    )(tile_bounds, span, x1, gate, g_final.reshape(1, d), ys)


def kernel(x, c, ctx, c_ctx, w_mod, b_mod, g_mix, g_ffn, w_in, b_in, hy_conv_w, hy_conv_b, hy_f_w1, hy_f_b1, hy_f_w2, hy_f_b2, hy_f_w3, hy_f_b3, hy_f_wout, hy_f_freq, hy_bias, lru_conv_w, lru_conv_b, lru_wa, lru_ba, lru_wx, lru_bx, lru_lambda, w_out, b_out, w_router, w_exp_gate, w_exp_up, w_exp_down, g_final):
    bsz, n_lat, d = x.shape
    n_ctx = ctx.shape[1]
    depth = w_mod.shape[0]
    c_hy = hy_bias.shape[1]
    c_lru = lru_conv_b.shape[1]
    in_gate = 3 * c_hy
    in_x = in_gate + c_lru
    n_exp = w_router.shape[2]
    cap = CAPACITY_FACTOR * n_lat // n_exp
    assert depth == 1, "context residual updates are only needed for depth > 1"
    assert n_lat % GRID_W == 0 and n_ctx & (n_ctx - 1) == 0

    rows = -(-(bsz + 1) // SUBLANES) * SUBLANES
    c_all = jnp.zeros((rows, d), F32).at[:bsz].set(c).at[bsz].set(c_ctx)
    mats = _fft_matrices(n_lat)

    l = 0
    mod = _modulation(c_all, w_mod[l], b_mod[l])
    mx = mod[:bsz].reshape(bsz, 1, N_MOD, d)
    sh1, sc1, gt1, sh2, sc2, gt2 = (mx[:, :, k] for k in range(N_MOD))
    mc = mod[bsz].reshape(1, 1, N_MOD, d)
    csh1, csc1 = mc[:, :, 0], mc[:, :, 1]

    w_in_b = w_in[l].astype(BF16)
    w_out_b = w_out[l].astype(BF16)

    hx = _norm_mod(x, g_mix[l], sh1, sc1, True)
    hc = _norm_mod(ctx, g_mix[l], csh1, csc1, False)
    hx = hx.reshape(bsz * n_lat, d)
    x0, wf = _projection_hyena(hx, w_in_b, b_in[l], hy_conv_w[l], hy_conv_b[l], c_hy)
    x0, wf = x0.reshape(bsz, n_lat, c_hy), wf.reshape(bsz, n_lat, c_hy)
    p_lru = _projection(hx, w_in_b, b_in[l], in_gate, 2 * c_lru, F32)
    p_lru = p_lru.reshape(bsz, n_lat, 2 * c_lru)
    pc_lx = _projection(hc.reshape(bsz * n_ctx, d), w_in_b, b_in[l], in_x, c_lru, F32)
    pc_lx = pc_lx.reshape(bsz, n_ctx, c_lru)

    lru_args = (lru_conv_w[l], lru_conv_b[l], lru_wa[l], lru_ba[l], lru_wx[l], lru_bx[l], lru_lambda[l])
    (h_ctx,) = _rglru(pc_lx, 0, None, 0, *lru_args, jnp.zeros((bsz, 2, c_lru), F32), n_ctx)
    y_lru, _ = _rglru(p_lru, c_lru, p_lru, 0, *lru_args, h_ctx, GRID_W)

    taps, norm = _hyena_filter(n_lat, hy_f_w1[l], hy_f_b1[l], hy_f_w2[l], hy_f_b2[l], hy_f_w3[l],
                               hy_f_b3[l], hy_f_wout[l], hy_f_freq[l])
    kspec = _filter_spectrum(taps, norm, mats, c_hy)
    yspec = _conv_forward(wf, kspec, mats)
    y_hy = _conv_inverse(yspec, wf, x0, hy_bias[l], mats)

    x1 = _out_projection(y_hy.reshape(bsz * n_lat, c_hy), y_lru.reshape(bsz * n_lat, c_lru), w_out_b,
                         b_out[l], x.reshape(bsz * n_lat, d), gt1, n_lat)
    x1 = x1.reshape(bsz, n_lat, d)

    h2, aff = _router(x1, g_ffn[l], sh2, sc2, w_router[l])
    idx, gates, dest, span = _select_tokens(aff, cap)
    rows_g = (idx + (jnp.arange(bsz, dtype=jnp.int32) * n_lat)[:, None, None])
    expert_major = lambda v: jnp.swapaxes(v, 0, 1).reshape(-1)
    xs = _gather_rows(expert_major(rows_g), h2)
    hmid = _expert_up(xs, w_exp_gate[l], w_exp_up[l], bsz * cap)
    ys = _expert_down(expert_major(dest), hmid, w_exp_down[l], expert_major(gates).reshape(-1, 1), bsz * cap)
    slots = n_exp * cap
    base = (jnp.arange(bsz, dtype=jnp.int32) * slots)[:, None]
    bounds = jnp.concatenate([span[:, 0, ::COMBINE_SUB].astype(jnp.int32) + base, base + slots], axis=1)
    bounds = bounds.reshape(-1)
    return _combine(bounds, jnp.swapaxes(span, 1, 2), ys, x1, gt2, g_final, slots)
```

```python
import functools
import math

import jax
import jax.numpy as jnp
from jax import lax
from jax.experimental import pallas as pl
from jax.experimental.pallas import tpu as pltpu

F32 = jnp.float32
BF16 = jnp.bfloat16

GRID_W = 64
HY_SHORT_LEFT = 1
FILT_BANDS = 16
DECAY_TARGET = 1e-2
MIN_DECAY = math.log(DECAY_TARGET) / 1.5
MAX_DECAY = math.log(DECAY_TARGET) / 0.3
LRU_HEADS = 16
LRU_CONV_LEFT = 2
LRU_C = 8.0
CAPACITY_FACTOR = 2
N_MOD = 6
EPS = 1e-6

LANES = 128
SUBLANES = 8
VMEM_LIMIT = 56 * 2 ** 20


def _params(sem):
    return pltpu.CompilerParams(dimension_semantics=sem, vmem_limit_bytes=VMEM_LIMIT)


def _tile(n, pref):
    t = min(n, pref)
    while n % t:
        t //= 2
    return t


def _split_bf16(v):
    hi = v.astype(BF16)
    lo = (v - hi.astype(F32)).astype(BF16)
    return hi, lo


def _mod_kernel(c_ref, w_ref, b_ref, o_ref):
    c = c_ref[...]
    s = c * jax.nn.sigmoid(c)
    s_hi, s_lo = _split_bf16(s)
    w_hi, w_lo = _split_bf16(w_ref[...])
    rows = c.shape[0]
    r = jnp.dot(jnp.concatenate([s_hi, s_lo], axis=0), w_hi, preferred_element_type=F32)
    acc = r[:rows] + r[rows:] + jnp.dot(s_hi, w_lo, preferred_element_type=F32)
    o_ref[...] = acc + b_ref[...]


def _modulation(c_all, w_mod, b_mod):
    rows, d = c_all.shape
    n = w_mod.shape[1]
    tn = _tile(n, 256)
    return pl.pallas_call(
        _mod_kernel,
        grid=(n // tn,),
        in_specs=[pl.BlockSpec((rows, d), lambda j: (0, 0)),
                  pl.BlockSpec((d, tn), lambda j: (0, j)),
                  pl.BlockSpec((1, tn), lambda j: (0, j))],
        out_specs=pl.BlockSpec((rows, tn), lambda j: (0, j)),
        out_shape=jax.ShapeDtypeStruct((rows, n), F32),
        compiler_params=_params(("parallel",)),
        name="modulation",
    )(c_all, w_mod, b_mod.reshape(1, n))


def _rms_mod(x, g, shift, scale):
    y = x * lax.rsqrt(jnp.mean(x * x, axis=-1, keepdims=True) + EPS) * g
    return y * (1.0 + scale) + shift


def _norm_kernel(x_ref, g_ref, sh_ref, sc_ref, o_ref):
    o_ref[0] = _rms_mod(x_ref[0], g_ref[...], sh_ref[0], sc_ref[0]).astype(o_ref.dtype)


def _norm_mod(x, g, shift, scale, per_batch):
    b, t, d = x.shape
    tt = _tile(t, 512)
    mod_map = (lambda i, j: (i, 0, 0)) if per_batch else (lambda i, j: (0, 0, 0))
    return pl.pallas_call(
        _norm_kernel,
        grid=(b, t // tt),
        in_specs=[pl.BlockSpec((1, tt, d), lambda i, j: (i, j, 0)),
                  pl.BlockSpec((1, d), lambda i, j: (0, 0)),
                  pl.BlockSpec((1, 1, d), mod_map),
                  pl.BlockSpec((1, 1, d), mod_map)],
        out_specs=pl.BlockSpec((1, tt, d), lambda i, j: (i, j, 0)),
        out_shape=jax.ShapeDtypeStruct((b, t, d), BF16),
        compiler_params=_params(("parallel", "parallel")),
        name="adaln_norm",
    )(x, g.reshape(1, d), shift, scale)


def _proj_kernel(a_ref, w_ref, b_ref, o_ref):
    acc = jnp.dot(a_ref[...], w_ref[...], preferred_element_type=F32)
    o_ref[...] = (acc + b_ref[...]).astype(o_ref.dtype)


def _projection(a, w, bias, col_start, n_cols, out_dtype):
    m, k = a.shape
    tm = _tile(m, 1024)
    tn = _tile(n_cols, 512)
    off = col_start // tn
    return pl.pallas_call(
        _proj_kernel,
        grid=(m // tm, n_cols // tn),
        in_specs=[pl.BlockSpec((tm, k), lambda i, j: (i, 0)),
                  pl.BlockSpec((k, tn), lambda i, j: (0, j + off)),
                  pl.BlockSpec((1, tn), lambda i, j: (0, j + off))],
        out_specs=pl.BlockSpec((tm, tn), lambda i, j: (i, j)),
        out_shape=jax.ShapeDtypeStruct((m, n_cols), out_dtype),
        compiler_params=_params(("parallel", "arbitrary")),
        name="projection",
    )(a, w, bias.reshape(1, -1))


def _short_conv(x, w_ref, bias, left, group):
    rows = x.shape[0]
    pos = lax.broadcasted_iota(jnp.int32, (rows, 1), 0) & (group - 1)
    y = bias + w_ref[left:left + 1, :] * x
    for k in range(w_ref.shape[0]):
        off = k - left
        if off == 0:
            continue
        shifted = pltpu.roll(x, (-off) % rows, axis=0)
        valid = (pos + off >= 0) & (pos + off < group)
        y = y + w_ref[k:k + 1, :] * jnp.where(valid, shifted, 0.0)
    return y


def _proj_hyena_kernel(a_ref, w0_ref, w1_ref, w2_ref, b0_ref, b1_ref, b2_ref,
                       cw0_ref, cw1_ref, cw2_ref, cb0_ref, cb1_ref, cb2_ref, x0_ref, wf_ref):
    a = a_ref[...]

    def branch(w_ref, b_ref, cw_ref, cb_ref):
        p = jnp.dot(a, w_ref[...], preferred_element_type=F32) + b_ref[...]
        return _short_conv(p, cw_ref, cb_ref[...], HY_SHORT_LEFT, GRID_W)

    x0_ref[...] = branch(w0_ref, b0_ref, cw0_ref, cb0_ref)
    wf_ref[...] = branch(w1_ref, b1_ref, cw1_ref, cb1_ref) * branch(w2_ref, b2_ref, cw2_ref, cb2_ref)


def _projection_hyena(a, w, bias, conv_w, conv_b, c):
    m, k = a.shape
    tm = _tile(m, 1024)
    assert tm % GRID_W == 0
    tc = _tile(c, 256)
    nc = c // tc
    kw = conv_w.shape[0]
    wspec = lambda g: pl.BlockSpec((k, tc), lambda i, j: (0, j + g * nc))
    bspec = lambda g: pl.BlockSpec((1, tc), lambda i, j: (0, j + g * nc))
    cwspec = lambda g: pl.BlockSpec((kw, tc), lambda i, j: (0, j + g * nc))
    ospec = pl.BlockSpec((tm, tc), lambda i, j: (i, j))
    b2 = bias.reshape(1, -1)
    cb = conv_b.reshape(1, -1)
    shape = jax.ShapeDtypeStruct((m, c), F32)
    return pl.pallas_call(
        _proj_hyena_kernel,
        grid=(m // tm, nc),
        in_specs=[pl.BlockSpec((tm, k), lambda i, j: (i, 0)),
                  wspec(0), wspec(1), wspec(2), bspec(0), bspec(1), bspec(2),
                  cwspec(0), cwspec(1), cwspec(2), bspec(0), bspec(1), bspec(2)],
        out_specs=[ospec, ospec],
        out_shape=[shape, shape],
        compiler_params=_params(("parallel", "arbitrary")),
        name="projection_hyena",
    )(a, w, w, w, b2, b2, b2, conv_w, conv_w, conv_w, cb, cb, cb)


def _hp_dot(a, b):
    return jnp.dot(a, b, preferred_element_type=F32, precision=lax.Precision.HIGHEST)


def _filter_kernel(n, w1t_ref, w1c_ref, w1s_ref, b1_ref, w2_ref, b2_ref, w3_ref, b3_ref,
                   wout_ref, freq_ref, delta_ref, h_ref, norm_ref):
    i = pl.program_id(0)
    tn = h_ref.shape[0]
    pos_i = i * tn + lax.broadcasted_iota(jnp.int32, (tn, 1), 0)
    pos = pos_i.astype(F32)
    t = pos * (1.0 / (n - 1))
    band_step = (FILT_BANDS - 1 - 1e-4) / (FILT_BANDS - 1)
    bands = 1e-4 + band_step * lax.broadcasted_iota(jnp.int32, (1, FILT_BANDS), 1).astype(F32)
    ang = (2.0 * math.pi * pos / n) * bands
    fr = freq_ref[...]
    pre = t * w1t_ref[...] + _hp_dot(jnp.cos(ang), w1c_ref[...]) - _hp_dot(jnp.sin(ang), w1s_ref[...])
    h = jnp.sin(fr * (pre + b1_ref[...]))
    h = jnp.sin(fr * (_hp_dot(h, w2_ref[...]) + b2_ref[...]))
    h = jnp.sin(fr * (_hp_dot(h, w3_ref[...]) + b3_ref[...]))
    taps = _hp_dot(h, wout_ref[...]) * jnp.exp(-t * delta_ref[...])

    @pl.when(i == 0)
    def _():
        norm_ref[...] = jnp.zeros_like(norm_ref)

    norm_ref[...] += jnp.sum(jnp.abs(taps), axis=0, keepdims=True)
    c = taps.shape[1] // 2
    col = lax.broadcasted_iota(jnp.int32, (1, taps.shape[1]), 1)
    drop = (pos_i == 0) & (col >= c)
    h_ref[...] = jnp.where(drop, 0.0, taps).astype(h_ref.dtype)


def _hyena_filter(n, w1, b1, w2, b2, w3, b3, wout, freq):
    hid = w1.shape[1]
    c2 = wout.shape[1]
    c = c2 // 2
    tn = _tile(n, 512)
    deltas = jnp.abs(jnp.linspace(MIN_DECAY, MAX_DECAY, c, dtype=F32))
    deltas = jnp.concatenate([deltas, deltas]).reshape(1, c2)
    full = lambda shape: pl.BlockSpec(shape, lambda i: (0, 0))
    return pl.pallas_call(
        functools.partial(_filter_kernel, n),
        grid=(n // tn,),
        in_specs=[full((1, hid)), full((FILT_BANDS, hid)), full((FILT_BANDS, hid)), full((1, hid)),
                  full((hid, hid)), full((1, hid)), full((hid, hid)), full((1, hid)),
                  full((hid, c2)), full((1, hid)), full((1, c2))],
        out_specs=[pl.BlockSpec((tn, c2), lambda i: (i, 0)), full((1, c2))],
        out_shape=[jax.ShapeDtypeStruct((n, c2), F32), jax.ShapeDtypeStruct((1, c2), F32)],
        compiler_params=_params(("arbitrary",)),
        name="hyena_filter",
    )(w1[0:1], w1[1:1 + FILT_BANDS], w1[1 + FILT_BANDS:], b1.reshape(1, hid), w2, b2.reshape(1, hid),
      w3, b3.reshape(1, hid), wout, freq.reshape(1, hid), deltas)


FFT_P = LANES
FFT_R = SUBLANES


def _phase(num, den):
    return (num % den).astype(F32) * (2.0 * math.pi / den)


def _fft_matrices(n):
    p, r = FFT_P, FFT_R
    q = n // p
    n2 = 2 * n
    iq = jnp.arange(q, dtype=jnp.int32)
    eye = jnp.eye(r, dtype=F32)
    alt = lambda v: (1 - 2 * (v & 1)).astype(F32)

    ang = _phase(iq[:, None] * iq[None, :], 2 * q)
    a_re = jnp.cos(ang)
    a_im = (-jnp.sin(ang)).at[0].set(alt(iq))
    core = jnp.stack([a_re, a_im], axis=1)
    m1 = jnp.einsum('fks,rt->fkrst', core, eye).reshape(q * 2 * r, q * r)

    b_c = jnp.cos(ang.T)
    b_s = (-jnp.sin(ang.T)).at[:, 0].set(alt(iq))
    core = jnp.stack([b_c, b_s], axis=2)
    i2 = jnp.einsum('tfk,rs->trfks', core, eye).reshape(q * r, q * 2 * r)

    h = p // 2
    f2 = jnp.arange(h, dtype=jnp.int32)
    s2 = jnp.arange(p, dtype=jnp.int32)
    f_lo = iq[:, None] + 2 * q * f2[None, :]
    f_hi = jnp.where(iq[:, None] == 0, q, 2 * q - iq[:, None]) + 2 * q * f2[None, :]
    freq = jnp.stack([f_lo, f_hi], axis=1)
    phi = _phase(freq[..., None] * s2, n2)
    c, s = jnp.cos(phi), jnp.sin(phi)
    zero = jnp.zeros_like(c[:, 0])
    first = (iq == 0)[:, None, None]
    dcrow = (first & (f2 == 0)[None, :, None])
    nyq = jnp.broadcast_to(alt(s2), c[:, 0].shape)
    on_gr = jnp.stack([jnp.where(first, c[:, 0], c[:, 0]),
                       jnp.where(dcrow, nyq, -s[:, 0]),
                       jnp.where(first, zero, c[:, 1]),
                       jnp.where(first, zero, -s[:, 1])], axis=1)
    on_gi = jnp.stack([jnp.where(first, zero, s[:, 0]),
                       jnp.where(first, zero, c[:, 0]),
                       jnp.where(first, c[:, 1], -s[:, 1]),
                       jnp.where(first, -s[:, 1], -c[:, 1])], axis=1)
    m2 = jnp.concatenate([on_gr, on_gi], axis=-1).reshape(q, 4 * h, 2 * p)
    ct, st = jnp.swapaxes(c, 2, 3), jnp.swapaxes(s, 2, 3)
    zt = jnp.zeros_like(ct[:, 0])
    dccol = (first & (f2 == 0)[None, None, :])
    nyq_t = jnp.broadcast_to(alt(s2)[:, None], ct[:, 0].shape)
    hc = jnp.stack([ct[:, 0], jnp.where(dccol, nyq_t, -st[:, 0]),
                    jnp.where(first, zt, ct[:, 1]), jnp.where(first, zt, -st[:, 1])], axis=2)
    hs = jnp.stack([jnp.where(first, zt, st[:, 0]), jnp.where(first, zt, ct[:, 0]),
                    jnp.where(first, ct[:, 1], -st[:, 1]), jnp.where(first, -st[:, 1], -ct[:, 1])], axis=2)
    i1 = jnp.stack([hc, hs], axis=1).reshape(q, 2 * p, 4 * h)
    return tuple(m.astype(BF16) for m in (m1, m2, i1, i2))


def _fft_stage_a(u_ref, m1_ref, g_ref):
    q, groups, r, tc = u_ref.shape
    m1 = m1_ref[...]
    for g in range(groups):
        blk = u_ref[:, g].reshape(q * r, tc).astype(BF16)
        g_ref[g] = jnp.dot(m1, blk, preferred_element_type=F32).reshape(q, 2, r, tc)


def _fft_stage_b(g_ref, f1, m2):
    groups, _, _, r, tc = g_ref.shape
    z = jnp.concatenate([g_ref[:, f1, 0].reshape(groups * r, tc),
                         g_ref[:, f1, 1].reshape(groups * r, tc)], axis=0).astype(BF16)
    return jnp.dot(m2, z, preferred_element_type=F32)


def _cmul_packed(x, k, first):
    h = x.shape[0] // 4
    xr = (x[0:h], x[2 * h:3 * h])
    xi = (x[h:2 * h], x[3 * h:4 * h])
    kr = (k[0:h], k[2 * h:3 * h])
    ki = (k[h:2 * h], k[3 * h:4 * h])
    real_pair = first & (lax.broadcasted_iota(jnp.int32, (h, 1), 0) == 0)
    lo_re = jnp.where(real_pair, xr[0] * kr[0], xr[0] * kr[0] - xi[0] * ki[0])
    lo_im = jnp.where(real_pair, xi[0] * ki[0], xr[0] * ki[0] + xi[0] * kr[0])
    return jnp.concatenate([lo_re, lo_im, xr[1] * kr[1] - xi[1] * ki[1], xr[1] * ki[1] + xi[1] * kr[1]], axis=0)


def _filter_spec_kernel(n, hf_ref, hb_ref, nf_ref, nb_ref, m1_ref, m2_ref, k_ref, gf_ref, gb_ref):
    qi = pl.program_id(1)
    qb = m2_ref.shape[0]

    @pl.when(qi == 0)
    def _():
        _fft_stage_a(hf_ref, m1_ref, gf_ref)
        _fft_stage_a(hb_ref, m1_ref, gb_ref)

    inv_norm = 1.0 / (nf_ref[...] + nb_ref[...])
    h = m2_ref.shape[1] // 4
    row = lax.broadcasted_iota(jnp.int32, (4 * h, 1), 0)
    for k in range(qb):
        f1 = qi * qb + k
        xf = _fft_stage_b(gf_ref, f1, m2_ref[k])
        xb = _fft_stage_b(gb_ref, f1, m2_ref[k])
        real_pair = (f1 == 0) & ((row == 0) | (row == h))
        imag_row = ((row >= h) & (row < 2 * h)) | (row >= 3 * h)
        spec = jnp.where(imag_row & jnp.logical_not(real_pair), xf - xb, xf + xb)
        spec = spec * jnp.where(real_pair, 0.5 / n, 1.0 / n) * inv_norm
        k_ref[k] = spec.reshape(k_ref.shape[1:]).astype(k_ref.dtype)


def _time_view(v, n):
    return v.reshape(v.shape[:-2] + (n // FFT_P, FFT_P // FFT_R, FFT_R, v.shape[-1]))


def _filter_spectrum(taps, norm, mats, c):
    n = taps.shape[0]
    m1, m2, _, _ = mats
    q = n // FFT_P
    h = FFT_P // 2
    tc = _tile(c, 256)
    nc = c // tc
    qb = _tile(q, 8)
    tv = _time_view(taps, n)
    blk = (q, FFT_P // FFT_R, FFT_R, tc)
    return pl.pallas_call(
        functools.partial(_filter_spec_kernel, n),
        grid=(nc, q // qb),
        in_specs=[pl.BlockSpec(blk, lambda j, i: (0, 0, 0, j)),
                  pl.BlockSpec(blk, lambda j, i: (0, 0, 0, j + nc)),
                  pl.BlockSpec((1, tc), lambda j, i: (0, j)),
                  pl.BlockSpec((1, tc), lambda j, i: (0, j + nc)),
                  pl.BlockSpec(m1.shape, lambda j, i: (0, 0)),
                  pl.BlockSpec((qb,) + m2.shape[1:], lambda j, i: (i, 0, 0))],
        out_specs=pl.BlockSpec((qb, 4, h, tc), lambda j, i: (i, 0, 0, j)),
        out_shape=jax.ShapeDtypeStruct((q, 4, h, c), BF16),
        scratch_shapes=[pltpu.VMEM((FFT_P // FFT_R, q, 2, FFT_R, tc), F32) for _ in range(2)],
        compiler_params=_params(("parallel", "arbitrary")),
        name="filter_spectrum",
    )(tv, tv, norm, norm, m1, m2)


def _conv_fwd_kernel(w_ref, m1_ref, m2_ref, k_ref, y_ref, g_ref):
    qi = pl.program_id(2)
    qb = m2_ref.shape[0]

    @pl.when(qi == 0)
    def _():
        _fft_stage_a(w_ref.at[0], m1_ref, g_ref)

    for k in range(qb):
        f1 = qi * qb + k
        x = _fft_stage_b(g_ref, f1, m2_ref[k])
        y = _cmul_packed(x, k_ref[k].reshape(x.shape).astype(F32), f1 == 0)
        y_ref[0, k] = y.reshape(y_ref.shape[2:]).astype(y_ref.dtype)


def _conv_forward(wf, kspec, mats):
    b, n, c = wf.shape
    m1, m2, _, _ = mats
    q = n // FFT_P
    h = FFT_P // 2
    tc = _tile(c, 512)
    qb = _tile(q, 8)
    return pl.pallas_call(
        _conv_fwd_kernel,
        grid=(c // tc, b, q // qb),
        in_specs=[pl.BlockSpec((1, q, FFT_P // FFT_R, FFT_R, tc), lambda j, bi, i: (bi, 0, 0, 0, j)),
                  pl.BlockSpec(m1.shape, lambda j, bi, i: (0, 0)),
                  pl.BlockSpec((qb,) + m2.shape[1:], lambda j, bi, i: (i, 0, 0)),
                  pl.BlockSpec((qb, 4, h, tc), lambda j, bi, i: (i, 0, 0, j))],
        out_specs=pl.BlockSpec((1, qb, 4, h, tc), lambda j, bi, i: (bi, i, 0, 0, j)),
        out_shape=jax.ShapeDtypeStruct((b, q, 4, h, c), BF16),
        scratch_shapes=[pltpu.VMEM((FFT_P // FFT_R, q, 2, FFT_R, tc), F32)],
        compiler_params=_params(("parallel", "parallel", "arbitrary")),
        name="conv_forward_fft",
    )(_time_view(wf, n), m1, m2, kspec)


def _conv_inv_kernel(y_ref, i1_ref, i2_ref, wf_ref, x0_ref, bias_ref, o_ref, h_ref):
    qi = pl.program_id(2)
    qb = i1_ref.shape[0]
    groups, q, _, r, tc = h_ref.shape
    for k in range(qb):
        hv = jnp.dot(i1_ref[k], y_ref[0, k].reshape(i1_ref.shape[2], tc), preferred_element_type=F32)
        half = hv.shape[0] // 2
        h_ref[:, qi * qb + k, 0] = hv[:half].reshape(groups, r, tc)
        h_ref[:, qi * qb + k, 1] = hv[half:].reshape(groups, r, tc)

    @pl.when(qi == pl.num_programs(2) - 1)
    def _():
        i2 = i2_ref[...]
        bias = bias_ref[...]
        for g2 in range(groups // 2):
            parts = []
            for g in (2 * g2, 2 * g2 + 1):
                z = jnp.dot(i2, h_ref[g].reshape(q * 2 * r, tc).astype(BF16), preferred_element_type=F32)
                z = z.reshape(q, r, tc)
                parts.append(x0_ref[0, :, g] * (z + wf_ref[0, :, g] * bias))
            o_ref[0, :, g2] = jnp.concatenate(parts, axis=1).astype(o_ref.dtype)


def _conv_inverse(yspec, wf, x0, bias, mats):
    b, n, c = wf.shape
    _, _, i1, i2 = mats
    q = n // FFT_P
    h = FFT_P // 2
    groups = FFT_P // FFT_R
    tc = _tile(c, 256)
    qb = _tile(q, 8)
    tspec = pl.BlockSpec((1, q, groups, FFT_R, tc), lambda j, bi, i: (bi, 0, 0, 0, j))
    out = pl.pallas_call(
        _conv_inv_kernel,
        grid=(c // tc, b, q // qb),
        in_specs=[pl.BlockSpec((1, qb, 4, h, tc), lambda j, bi, i: (bi, i, 0, 0, j)),
                  pl.BlockSpec((qb,) + i1.shape[1:], lambda j, bi, i: (i, 0, 0)),
                  pl.BlockSpec(i2.shape, lambda j, bi, i: (0, 0)),
                  tspec, tspec,
                  pl.BlockSpec((1, tc), lambda j, bi, i: (0, j))],
        out_specs=pl.BlockSpec((1, q, groups // 2, 2 * FFT_R, tc), lambda j, bi, i: (bi, 0, 0, 0, j)),
        out_shape=jax.ShapeDtypeStruct((b, q, groups // 2, 2 * FFT_R, c), BF16),
        scratch_shapes=[pltpu.VMEM((groups, q, 2, FFT_R, tc), F32)],
        compiler_params=_params(("parallel", "parallel", "arbitrary")),
        name="conv_inverse_fft",
    )(yspec, i1, i2, _time_view(wf, n), _time_view(x0, n), bias.reshape(1, c))
    return out.reshape(b, n, c)


SCAN_UNROLL = 4


def _tile_scan(a, b, reverse):
    rows, c = a.shape
    a3 = a.reshape(rows // SUBLANES, SUBLANES, c)
    b3 = b.reshape(rows // SUBLANES, SUBLANES, c)
    sub = lax.broadcasted_iota(jnp.int32, (1, SUBLANES, 1), 1)
    for k in (1, 2, 4):
        shift = SUBLANES - k if reverse else k
        valid = (sub < SUBLANES - k) if reverse else (sub >= k)
        b3 = b3 + a3 * jnp.where(valid, pltpu.roll(b3, shift, axis=1), 0.0)
        a3 = a3 * jnp.where(valid, pltpu.roll(a3, shift, axis=1), 1.0)
    return a3.reshape(rows, c), b3.reshape(rows, c)


def _lru_kernel(group, has_gate, *refs):
    if has_gate:
        (x_ref, gate_ref, cw_ref, cb_ref, wa_ref, ba_ref, wx_ref, bx_ref, lam_ref, h0_ref,
         y_ref, ht_ref, af_ref, bf_ref, ab_ref, bb_ref) = refs
    else:
        (x_ref, cw_ref, cb_ref, wa_ref, ba_ref, wx_ref, bx_ref, lam_ref, h0_ref,
         ht_ref, af_ref, bf_ref, ab_ref, bb_ref) = refs
    t_len = x_ref.shape[1]
    chunk = _tile(t_len, 512)
    a_refs = (af_ref, ab_ref)
    b_refs = (bf_ref, bb_ref)

    def coeffs(ci, carry):
        r0 = pl.multiple_of(ci * chunk, chunk)
        xc = _short_conv(x_ref[0, pl.ds(r0, chunk), :], cw_ref, cb_ref[...], LRU_CONV_LEFT, group)
        xcb = xc.astype(BF16)
        half_xc = 0.5 * xc
        for d in range(2):
            ta = jnp.tanh(jnp.dot(xcb, wa_ref[d, 0], preferred_element_type=F32) + ba_ref[d])
            ti = jnp.tanh(jnp.dot(xcb, wx_ref[d, 0], preferred_element_type=F32) + bx_ref[d])
            lam = lam_ref[d]
            softplus_neg = jnp.maximum(-lam, 0.0) + jnp.log(1.0 + jnp.exp(-jnp.abs(lam)))
            c1 = (-0.5 * LRU_C) * softplus_neg
            a = jnp.exp(c1 * ta + c1)
            a_tile, b_tile = _tile_scan(a, jnp.sqrt(1.0 - a * a) * (half_xc * ti + half_xc), d == 1)
            a_refs[d][pl.ds(r0, chunk), :] = a_tile
            b_refs[d][pl.ds(r0, chunk), :] = b_tile
        return carry

    lax.fori_loop(0, t_len // chunk, coeffs, 0)

    n_tiles = t_len // SUBLANES
    unroll = _tile(n_tiles, SCAN_UNROLL)

    def scan(i, carry):
        hf, hb = carry
        for u in range(unroll):
            kf = i * unroll + u
            rf = pl.multiple_of(kf * SUBLANES, SUBLANES)
            tile_f = af_ref[pl.ds(rf, SUBLANES), :] * hf + bf_ref[pl.ds(rf, SUBLANES), :]
            bf_ref[pl.ds(rf, SUBLANES), :] = tile_f
            hf = tile_f[SUBLANES - 1:SUBLANES, :]
            rb = pl.multiple_of((n_tiles - 1 - kf) * SUBLANES, SUBLANES)
            tile_b = ab_ref[pl.ds(rb, SUBLANES), :] * hb + bb_ref[pl.ds(rb, SUBLANES), :]
            bb_ref[pl.ds(rb, SUBLANES), :] = tile_b
            hb = tile_b[0:1, :]
        return hf, hb

    hf, hb = lax.fori_loop(0, n_tiles // unroll, scan, (h0_ref[0, 0:1, :], h0_ref[0, 1:2, :]))
    ht_ref[0, 0:1, :] = hf
    ht_ref[0, 1:2, :] = hb

    if has_gate:
        def emit(ci, carry):
            r0 = pl.multiple_of(ci * chunk, chunk)
            hs = bf_ref[pl.ds(r0, chunk), :] + bb_ref[pl.ds(r0, chunk), :]
            y_ref[0, pl.ds(r0, chunk), :] = (jax.nn.gelu(gate_ref[0, pl.ds(r0, chunk), :]) * hs).astype(y_ref.dtype)
            return carry

        lax.fori_loop(0, t_len // chunk, emit, 0)


def _block_diag(w, heads_per_tile):
    d2, h, hd, _ = w.shape
    w = w.reshape(d2, h // heads_per_tile, heads_per_tile, hd, hd)
    eye = jnp.eye(heads_per_tile, dtype=w.dtype)
    bd = jnp.einsum('dghij,hq->dghiqj', w, eye)
    return bd.reshape(d2, h // heads_per_tile, heads_per_tile * hd, heads_per_tile * hd).astype(BF16)


def _rglru(xsrc, x_col, gsrc, g_col, conv_w, conv_b, wa, ba, wx, bx, lam, h0, group):
    b, t, _ = xsrc.shape
    c = conv_w.shape[1]
    hd = c // LRU_HEADS
    tc = min(c, max(hd, 256))
    hp = tc // hd
    nt = c // tc
    has_gate = gsrc is not None
    wa_bd = _block_diag(0.5 * wa, hp)
    wx_bd = _block_diag(0.5 * wx, hp)
    ba, bx = 0.5 * ba, 0.5 * bx
    xo, go = x_col // tc, (g_col // tc if has_gate else 0)
    vec = lambda: pl.BlockSpec((2, 1, tc), lambda i, j: (0, 0, j))
    mat = lambda: pl.BlockSpec((2, 1, tc, tc), lambda i, j: (0, j, 0, 0))
    in_specs = [pl.BlockSpec((1, t, tc), lambda i, j: (i, 0, j + xo))]
    args = [xsrc]
    if has_gate:
        in_specs.append(pl.BlockSpec((1, t, tc), lambda i, j: (i, 0, j + go)))
        args.append(gsrc)
    in_specs += [pl.BlockSpec((conv_w.shape[0], tc), lambda i, j: (0, j)),
                 pl.BlockSpec((1, tc), lambda i, j: (0, j)),
                 mat(), vec(), mat(), vec(), vec(),
                 pl.BlockSpec((1, 2, tc), lambda i, j: (i, 0, j))]
    args += [conv_w, conv_b.reshape(1, c), wa_bd, ba.reshape(2, 1, c), wx_bd, bx.reshape(2, 1, c),
             lam.reshape(2, 1, c), h0]
    ht_spec = pl.BlockSpec((1, 2, tc), lambda i, j: (i, 0, j))
    ht_shape = jax.ShapeDtypeStruct((b, 2, c), F32)
    if has_gate:
        out_specs = [pl.BlockSpec((1, t, tc), lambda i, j: (i, 0, j)), ht_spec]
        out_shape = [jax.ShapeDtypeStruct((b, t, c), BF16), ht_shape]
    else:
        out_specs = [ht_spec]
        out_shape = [ht_shape]
    return pl.pallas_call(
        functools.partial(_lru_kernel, group, has_gate),
        grid=(b, nt),
        in_specs=in_specs,
        out_specs=out_specs,
        out_shape=out_shape,
        scratch_shapes=[pltpu.VMEM((t, tc), F32) for _ in range(4)],
        compiler_params=_params(("parallel", "parallel")),
        name="rglru" if has_gate else "rglru_context",
    )(*args)


def _out_proj_kernel(a1_ref, a2_ref, w1_ref, w2_ref, b_ref, x_ref, g_ref, o_ref):
    acc = jnp.dot(a1_ref[...], w1_ref[...], preferred_element_type=F32)
    acc = acc + jnp.dot(a2_ref[...], w2_ref[...], preferred_element_type=F32)
    o_ref[...] = x_ref[...] + g_ref[0] * (acc + b_ref[...])


def _out_projection(a1, a2, w, bias, x, gate, t):
    m, k1 = a1.shape
    n = w.shape[1]
    tm = _tile(t, 1024)
    tn = _tile(n, 512)
    per_b = t // tm
    k1_blocks = 1
    return pl.pallas_call(
        _out_proj_kernel,
        grid=(m // tm, n // tn),
        in_specs=[pl.BlockSpec((tm, k1), lambda i, j: (i, 0)),
                  pl.BlockSpec((tm, k1), lambda i, j: (i, 0)),
                  pl.BlockSpec((k1, tn), lambda i, j: (0, j)),
                  pl.BlockSpec((k1, tn), lambda i, j: (k1_blocks, j)),
                  pl.BlockSpec((1, tn), lambda i, j: (0, j)),
                  pl.BlockSpec((tm, tn), lambda i, j: (i, j)),
                  pl.BlockSpec((1, 1, tn), lambda i, j: (i // per_b, 0, j))],
        out_specs=pl.BlockSpec((tm, tn), lambda i, j: (i, j)),
        out_shape=jax.ShapeDtypeStruct((m, n), F32),
        compiler_params=_params(("parallel", "arbitrary")),
        name="out_projection",
    )(a1, a2, w, w, bias.reshape(1, n), x, gate)


def _router_kernel(x_ref, g_ref, sh_ref, sc_ref, wr_ref, h_ref, aff_ref):
    h = _rms_mod(x_ref[0], g_ref[...], sh_ref[0], sc_ref[0])
    half = h.shape[1] // 2
    h_ref[...] = pltpu.pack_elementwise([h[:, :half], h[:, half:]], packed_dtype=BF16)
    h_hi, h_lo = _split_bf16(h)
    w_hi, w_lo = _split_bf16(wr_ref[...])
    nt = (((1,), (1,)), ((), ()))
    logits = (lax.dot_general(w_hi, h_hi, nt, preferred_element_type=F32)
              + lax.dot_general(w_lo, h_hi, nt, preferred_element_type=F32)
              + lax.dot_general(w_hi, h_lo, nt, preferred_element_type=F32))
    z = jnp.exp(logits - jnp.max(logits, axis=0, keepdims=True))
    aff_ref[0] = z / jnp.sum(z, axis=0, keepdims=True)


def _router(x1, g, shift, scale, w_router):
    b, t, d = x1.shape
    e = w_router.shape[1]
    tt = _tile(t, 256)
    per_b = t // tt
    mod_map = lambda i, j: (i, 0, 0)
    return pl.pallas_call(
        _router_kernel,
        grid=(b, per_b),
        in_specs=[pl.BlockSpec((1, tt, d), lambda i, j: (i, j, 0)),
                  pl.BlockSpec((1, d), lambda i, j: (0, 0)),
                  pl.BlockSpec((1, 1, d), mod_map),
                  pl.BlockSpec((1, 1, d), mod_map),
                  pl.BlockSpec((e, d), lambda i, j: (0, 0))],
        out_specs=[pl.BlockSpec((tt, d // 2), lambda i, j: (i * per_b + j, 0)),
                   pl.BlockSpec((1, e, tt), lambda i, j: (i, 0, j))],
        out_shape=[jax.ShapeDtypeStruct((b * t, d // 2), jnp.uint32),
                   jax.ShapeDtypeStruct((b, e, t), F32)],
        compiler_params=_params(("parallel", "parallel")),
        name="router",
    )(x1, g.reshape(1, d), shift, scale, w_router.T)


def _prefix_count(mask_ref, out_ref):
    e, t = mask_ref.shape
    blk = min(t, LANES)
    tri = (lax.broadcasted_iota(jnp.int32, (blk, blk), 0)
           < lax.broadcasted_iota(jnp.int32, (blk, blk), 1)).astype(BF16)
    carry = jnp.zeros((e, 1), F32)
    for k in range(t // blk):
        m = mask_ref[:, k * blk:(k + 1) * blk]
        out_ref[:, k * blk:(k + 1) * blk] = jnp.dot(m.astype(BF16), tri, preferred_element_type=F32) + carry
        carry = carry + jnp.sum(m, axis=1, keepdims=True)
    return carry


def _topk_kernel(cap, aff_ref, idx_ref, gate_ref, dest_ref, span_ref, mask_ref, pos_ref, cnt_ref, q_ref):
    a = aff_ref[0]
    e, t = a.shape
    min_normal = 0x00800000

    def refine(i, thr):
        cand = thr | jnp.left_shift(jnp.int32(1), 30 - i)
        cnt = jnp.sum(jnp.where(a >= pltpu.bitcast(cand, F32), 1.0, 0.0), axis=1, keepdims=True)
        return jnp.where((cnt >= cap) & (cand >= min_normal), cand, thr)

    thr = lax.fori_loop(0, 31, refine, jnp.zeros((e, 1), jnp.int32))
    above = a >= pltpu.bitcast(jnp.maximum(thr + 1, min_normal), F32)
    tied = (a >= pltpu.bitcast(thr, F32)) & jnp.logical_not(above)
    need = cap - jnp.sum(jnp.where(above, 1.0, 0.0), axis=1, keepdims=True)
    mask_ref[...] = jnp.where(tied, 1.0, 0.0)
    _prefix_count(mask_ref, pos_ref)
    sel = above | (tied & (pos_ref[...] < need))
    mask_ref[...] = jnp.where(sel, 1.0, 0.0)
    _prefix_count(mask_ref, pos_ref)

    mask = mask_ref[...]
    cnt = jnp.sum(mask, axis=0, keepdims=True)
    cnt_ref[...] = jnp.broadcast_to(cnt, cnt_ref.shape)
    _prefix_count(cnt_ref, q_ref.at[0:SUBLANES])
    first = q_ref[0:1, :]
    srow = lax.broadcasted_iota(jnp.int32, (SUBLANES, 1), 0)
    span_ref[0] = jnp.where(srow == 0, first, jnp.where(srow == 1, first + cnt, 0.0))
    lower = (lax.broadcasted_iota(jnp.int32, (e, e), 1)
             < lax.broadcasted_iota(jnp.int32, (e, e), 0)).astype(BF16)
    q_ref[...] = first + jnp.dot(lower, mask.astype(BF16), preferred_element_type=F32)

    tok = lax.broadcasted_iota(jnp.int32, (1, t), 1)
    tok_hi = (tok >> 6).astype(F32)
    tok_lo = (tok & 63).astype(F32)
    slot = lax.broadcasted_iota(jnp.int32, (cap, 1), 0).astype(F32)
    vrow = lax.broadcasted_iota(jnp.int32, (SUBLANES, 1), 0)
    dest_base = pl.program_id(0) * (e * cap)

    def compact(ei, carry):
        g = aff_ref[0, pl.ds(ei, 1), :]
        g_hi = g.astype(BF16).astype(F32)
        g_mid = (g - g_hi).astype(BF16).astype(F32)
        g_lo = g - g_hi - g_mid
        q = q_ref[pl.ds(ei, 1), :]
        q_hi = jnp.floor(q * (1.0 / 64.0))
        q_lo = q - 64.0 * q_hi
        vals = jnp.where(vrow == 0, tok_hi, jnp.where(vrow == 1, tok_lo, jnp.where(
            vrow == 2, g_hi, jnp.where(vrow == 3, g_mid, jnp.where(vrow == 4, g_lo, jnp.where(
                vrow == 5, q_hi, jnp.where(vrow == 6, q_lo, 0.0))))))).astype(BF16)
        hit = (pos_ref[pl.ds(ei, 1), :] == slot) & (mask_ref[pl.ds(ei, 1), :] > 0.0)
        onehot = jnp.where(hit, 1.0, 0.0).astype(BF16)
        res = lax.dot_general(vals, onehot, (((1,), (1,)), ((), ())), preferred_element_type=F32)
        idx_ref[0, pl.ds(ei, 1), :] = (res[0:1] * 64.0 + res[1:2]).astype(jnp.int32)
        gate_ref[0, pl.ds(ei, 1), :] = res[2:3] + res[3:4] + res[4:5]
        dest_ref[0, pl.ds(ei, 1), :] = (res[5:6] * 64.0 + res[6:7]).astype(jnp.int32) + dest_base
        return carry

    lax.fori_loop(0, e, compact, 0)


def _select_tokens(aff, cap):
    b, e, t = aff.shape
    assert t <= 64 * 256 and e * cap <= 64 * 256 and e >= SUBLANES
    spec = pl.BlockSpec((1, e, cap), lambda i: (i, 0, 0))
    return pl.pallas_call(
        functools.partial(_topk_kernel, cap),
        grid=(b,),
        in_specs=[pl.BlockSpec((1, e, t), lambda i: (i, 0, 0))],
        out_specs=[spec, spec, spec, pl.BlockSpec((1, SUBLANES, t), lambda i: (i, 0, 0))],
        out_shape=[jax.ShapeDtypeStruct((b, e, cap), jnp.int32), jax.ShapeDtypeStruct((b, e, cap), F32),
                   jax.ShapeDtypeStruct((b, e, cap), jnp.int32), jax.ShapeDtypeStruct((b, SUBLANES, t), F32)],
        scratch_shapes=[pltpu.VMEM((e, t), F32), pltpu.VMEM((e, t), F32),
                        pltpu.VMEM((SUBLANES, t), F32), pltpu.VMEM((e, t), F32)],
        compiler_params=_params(("parallel",)),
        name="select_tokens",
    )(aff)


def _gather_kernel(rows_ref, h_hbm, o_ref, buf, sem):
    i = pl.program_id(0)
    r_blk = buf.shape[1]

    def issue_block(blk, slot):
        def issue(r, carry):
            row = rows_ref[blk * r_blk + r]
            pltpu.make_async_copy(h_hbm.at[pl.ds(row, 1), :], buf.at[slot, pl.ds(r, 1), :], sem.at[slot]).start()
            return carry

        lax.fori_loop(0, r_blk, issue, 0, unroll=8)

    @pl.when(i == 0)
    def _():
        issue_block(0, 0)

    @pl.when(i + 1 < pl.num_programs(0))
    def _():
        issue_block(i + 1, (i + 1) & 1)

    slot = i & 1
    pltpu.make_async_copy(h_hbm.at[pl.ds(0, r_blk), :], buf.at[slot], sem.at[slot]).wait()
    words = buf[slot]
    half = words.shape[1]
    for h in range(2):
        part = pltpu.unpack_elementwise(words, index=h, packed_dtype=BF16, unpacked_dtype=F32)
        o_ref[:, h * half:(h + 1) * half] = part.astype(o_ref.dtype)


def _gather_rows(rows, h):
    n_rows = rows.shape[0]
    words = h.shape[1]
    d = 2 * words
    r_blk = _tile(n_rows, 256)
    grid_spec = pltpu.PrefetchScalarGridSpec(
        num_scalar_prefetch=1,
        grid=(n_rows // r_blk,),
        in_specs=[pl.BlockSpec(memory_space=pl.ANY)],
        out_specs=pl.BlockSpec((r_blk, d), lambda i, rows_ref: (i, 0)),
        scratch_shapes=[pltpu.VMEM((2, r_blk, words), h.dtype), pltpu.SemaphoreType.DMA((2,))],
    )
    return pl.pallas_call(
        _gather_kernel,
        grid_spec=grid_spec,
        out_shape=jax.ShapeDtypeStruct((n_rows, d), BF16),
        compiler_params=_params(("arbitrary",)),
        name="gather_rows",
    )(rows, h)


def _expert_up_kernel(xs_ref, wg_ref, wu_ref, h_ref):
    xs = xs_ref[...]
    g = jnp.dot(xs, wg_ref[0].astype(BF16), preferred_element_type=F32)
    u = jnp.dot(xs, wu_ref[0].astype(BF16), preferred_element_type=F32)
    h_ref[...] = (g * jax.nn.sigmoid(g) * u).astype(h_ref.dtype)


def _expert_up(xs, w_gate, w_up, rows_per_expert):
    m, d = xs.shape
    f = w_gate.shape[2]
    tm = _tile(rows_per_expert, 1024)
    per_e = rows_per_expert // tm
    tf = _tile(f, 256)
    wspec = pl.BlockSpec((1, d, tf), lambda i, j: (i // per_e, 0, j))
    return pl.pallas_call(
        _expert_up_kernel,
        grid=(m // tm, f // tf),
        in_specs=[pl.BlockSpec((tm, d), lambda i, j: (i, 0)), wspec, wspec],
        out_specs=pl.BlockSpec((tm, tf), lambda i, j: (i, j)),
        out_shape=jax.ShapeDtypeStruct((m, f), BF16),
        compiler_params=_params(("parallel", "arbitrary")),
        name="expert_up",
    )(xs, w_gate, w_up)


DOWN_COL_TILE = 1024


def _expert_down_kernel(dest_ref, h_ref, wd_ref, g_ref, ys_hbm, ybuf, sem):
    i = pl.program_id(0)
    j = pl.program_id(1)
    n_i = pl.num_programs(0)
    n_j = pl.num_programs(1)
    _, tm, dw = ybuf.shape
    hw = wd_ref.shape[2] // 2
    slot = i & 1

    def wait_scatter(s):
        pltpu.make_async_copy(ybuf.at[s], ys_hbm.at[pl.ds(0, tm), :], sem.at[s]).wait()

    @pl.when((j == 0) & (i >= 2))
    def _():
        wait_scatter(slot)

    y = jnp.dot(h_ref[...], wd_ref[0].astype(BF16), preferred_element_type=F32) * g_ref[...]
    packed = pltpu.pack_elementwise([y[:, :hw], y[:, hw:]], packed_dtype=BF16)
    for jj in range(dw // hw):
        @pl.when(j == jj)
        def _(jj=jj):
            ybuf[slot, :, jj * hw:(jj + 1) * hw] = packed

    @pl.when(j == n_j - 1)
    def _():
        def issue(r, carry):
            dst = dest_ref[i * tm + r]
            pltpu.make_async_copy(ybuf.at[slot, pl.ds(r, 1), :], ys_hbm.at[pl.ds(dst, 1), :], sem.at[slot]).start()
            return carry

        lax.fori_loop(0, tm, issue, 0, unroll=8)

        @pl.when(i == n_i - 1)
        def _():
            @pl.when(i >= 1)
            def _():
                wait_scatter(1 - slot)
            wait_scatter(slot)


def _expert_down(dest, h, w_down, gates, rows_per_expert):
    m, f = h.shape
    d = w_down.shape[2]
    tm = _tile(rows_per_expert, 1024)
    per_e = rows_per_expert // tm
    td = _tile(d, DOWN_COL_TILE)
    grid_spec = pltpu.PrefetchScalarGridSpec(
        num_scalar_prefetch=1,
        grid=(m // tm, d // td),
        in_specs=[pl.BlockSpec((tm, f), lambda i, j, dest_ref: (i, 0)),
                  pl.BlockSpec((1, f, td), lambda i, j, dest_ref: (i // per_e, 0, j)),
                  pl.BlockSpec((tm, 1), lambda i, j, dest_ref: (i, 0))],
        out_specs=pl.BlockSpec(memory_space=pl.ANY),
        scratch_shapes=[pltpu.VMEM((2, tm, d // 2), jnp.uint32), pltpu.SemaphoreType.DMA((2,))],
    )
    return pl.pallas_call(
        _expert_down_kernel,
        grid_spec=grid_spec,
        out_shape=jax.ShapeDtypeStruct((m, d // 2), jnp.uint32),
        compiler_params=_params(("arbitrary", "arbitrary")),
        name="expert_down",
    )(dest, h, w_down, gates)


def _combine_kernel(slots_per_batch, hw, tb_ref, span_ref, x_ref, gt_ref, g_ref, ys_hbm, o_ref,
                    buf, sem, acc_ref, parts_ref):
    bi = pl.program_id(0)
    i = pl.program_id(1)
    n_t = pl.num_programs(1)
    kc = buf.shape[1]
    total = ys_hbm.shape[0]
    tt = acc_ref.shape[1]
    n_sub = tt // COMBINE_SUB
    tb0 = bi * (n_t * n_sub + 1) + i * n_sub
    c_lo = tb_ref[tb0]
    c_hi = tb_ref[tb0 + n_sub]
    k_first = (c_lo >> 3) << 3
    n_chunks = (c_hi - k_first + kc - 1) // kc

    def chunk_start(c):
        return pl.multiple_of(jnp.minimum(k_first + c * kc, total - kc), SUBLANES)

    n_buf = buf.shape[0]

    def copy(c):
        slot = c & (n_buf - 1)
        return pltpu.make_async_copy(ys_hbm.at[pl.ds(chunk_start(c), kc), :], buf.at[slot], sem.at[slot])

    for ahead in range(n_buf - 1):
        @pl.when(ahead < n_chunks)
        def _(ahead=ahead):
            copy(ahead).start()

    base = (bi * slots_per_batch).astype(F32)
    acc_ref[...] = jnp.zeros_like(acc_ref)

    def body(c, carry):
        slot = c & (n_buf - 1)
        copy(c).wait()

        @pl.when(c + n_buf - 1 < n_chunks)
        def _():
            copy(c + n_buf - 1).start()

        k_lo = k_first + c * kc
        k_hi = chunk_start(c) + kc
        k = chunk_start(c) + lax.broadcasted_iota(jnp.int32, (1, kc), 1)
        k = jnp.where(k >= k_lo, k, -1).astype(F32)
        words = buf[slot]
        for half in range(2):
            part = pltpu.unpack_elementwise(words, index=half, packed_dtype=BF16, unpacked_dtype=F32)
            parts_ref[half] = part.astype(BF16)
        for s in range(n_sub):
            rows = slice(s * COMBINE_SUB, (s + 1) * COMBINE_SUB)

            @pl.when((tb_ref[tb0 + s] < k_hi) & (tb_ref[tb0 + s + 1] > k_lo))
            def _(rows=rows):
                first = span_ref[0, rows, 0:1] + base
                last = span_ref[0, rows, 1:2] + base
                onehot = jnp.where((k >= first) & (k < last), 1.0, 0.0).astype(BF16)
                for half in range(2):
                    acc_ref[half, rows] += jnp.dot(onehot, parts_ref[half], preferred_element_type=F32)
        return carry

    lax.fori_loop(0, n_chunks, body, 0)
    gate = gt_ref[0]
    g_final = g_ref[...]

    def finish(r, carry):
        rows = pl.ds(pl.multiple_of(r * FINISH_ROWS, FINISH_ROWS), FINISH_ROWS)
        lo, hi = acc_ref[0, rows, :], acc_ref[1, rows, :]
        moe = jnp.concatenate([part[:, j * hw:(j + 1) * hw] for j in range(lo.shape[1] // hw)
                               for part in (lo, hi)], axis=1)
        x = x_ref[0, rows, :] + gate * moe
        o_ref[0, rows, :] = x * lax.rsqrt(jnp.mean(x * x, axis=-1, keepdims=True) + EPS) * g_final
        return carry

    lax.fori_loop(0, tt // FINISH_ROWS, finish, 0)


COMBINE_ROWS = 512
COMBINE_SUB = 128
COMBINE_CHUNK = 256
COMBINE_RING = 4
FINISH_ROWS = 64


def _combine(tile_bounds, span, ys, x1, gate, g_final, slots_per_batch):
    b, t, d = x1.shape
    tt = _tile(t, COMBINE_ROWS)
    assert tt % COMBINE_SUB == 0
    kc = min(COMBINE_CHUNK, ys.shape[0])
    grid_spec = pltpu.PrefetchScalarGridSpec(
        num_scalar_prefetch=1,
        grid=(b, t // tt),
        in_specs=[pl.BlockSpec((1, tt, SUBLANES), lambda bi, i, tb: (bi, i, 0)),
                  pl.BlockSpec((1, tt, d), lambda bi, i, tb: (bi, i, 0)),
                  pl.BlockSpec((1, 1, d), lambda bi, i, tb: (bi, 0, 0)),
                  pl.BlockSpec((1, d), lambda bi, i, tb: (0, 0)),
                  pl.BlockSpec(memory_space=pl.ANY)],
        out_specs=pl.BlockSpec((1, tt, d), lambda bi, i, tb: (bi, i, 0)),
        scratch_shapes=[pltpu.VMEM((COMBINE_RING, kc, d // 2), jnp.uint32),
                        pltpu.SemaphoreType.DMA((COMBINE_RING,)),
                        pltpu.VMEM((2, tt, d // 2), F32), pltpu.VMEM((2, kc, d // 2), BF16)],
    )
    return pl.pallas_call(
        functools.partial(_combine_kernel, slots_per_batch, _tile(d, DOWN_COL_TILE) // 2),
        grid_spec=grid_spec,
        out_shape=jax.ShapeDtypeStruct((b, t, d), F32),
        compiler_params=_params(("arbitrary", "arbitrary")),
        name="combine",
    )(tile_bounds, span, x1, gate, g_final.reshape(1, d), ys)


def kernel(x, c, ctx, c_ctx, w_mod, b_mod, g_mix, g_ffn, w_in, b_in, hy_conv_w, hy_conv_b, hy_f_w1, hy_f_b1, hy_f_w2, hy_f_b2, hy_f_w3, hy_f_b3, hy_f_wout, hy_f_freq, hy_bias, lru_conv_w, lru_conv_b, lru_wa, lru_ba, lru_wx, lru_bx, lru_lambda, w_out, b_out, w_router, w_exp_gate, w_exp_up, w_exp_down, g_final):
    bsz, n_lat, d = x.shape
    n_ctx = ctx.shape[1]
    depth = w_mod.shape[0]
    c_hy = hy_bias.shape[1]
    c_lru = lru_conv_b.shape[1]
    in_gate = 3 * c_hy
    in_x = in_gate + c_lru
    n_exp = w_router.shape[2]
    cap = CAPACITY_FACTOR * n_lat // n_exp
    assert depth == 1, "context residual updates are only needed for depth > 1"
    assert n_lat % GRID_W == 0 and n_ctx & (n_ctx - 1) == 0

    rows = -(-(bsz + 1) // SUBLANES) * SUBLANES
    c_all = jnp.zeros((rows, d), F32).at[:bsz].set(c).at[bsz].set(c_ctx)
    mats = _fft_matrices(n_lat)

    l = 0
    mod = _modulation(c_all, w_mod[l], b_mod[l])
    mx = mod[:bsz].reshape(bsz, 1, N_MOD, d)
    sh1, sc1, gt1, sh2, sc2, gt2 = (mx[:, :, k] for k in range(N_MOD))
    mc = mod[bsz].reshape(1, 1, N_MOD, d)
    csh1, csc1 = mc[:, :, 0], mc[:, :, 1]

    w_in_b = w_in[l].astype(BF16)
    w_out_b = w_out[l].astype(BF16)

    hx = _norm_mod(x, g_mix[l], sh1, sc1, True)
    hc = _norm_mod(ctx, g_mix[l], csh1, csc1, False)
    hx = hx.reshape(bsz * n_lat, d)
    x0, wf = _projection_hyena(hx, w_in_b, b_in[l], hy_conv_w[l], hy_conv_b[l], c_hy)
    x0, wf = x0.reshape(bsz, n_lat, c_hy), wf.reshape(bsz, n_lat, c_hy)
    p_lru = _projection(hx, w_in_b, b_in[l], in_gate, 2 * c_lru, F32)
    p_lru = p_lru.reshape(bsz, n_lat, 2 * c_lru)
    pc_lx = _projection(hc.reshape(bsz * n_ctx, d), w_in_b, b_in[l], in_x, c_lru, F32)
    pc_lx = pc_lx.reshape(bsz, n_ctx, c_lru)

    lru_args = (lru_conv_w[l], lru_conv_b[l], lru_wa[l], lru_ba[l], lru_wx[l], lru_bx[l], lru_lambda[l])
    (h_ctx,) = _rglru(pc_lx, 0, None, 0, *lru_args, jnp.zeros((bsz, 2, c_lru), F32), n_ctx)
    y_lru, _ = _rglru(p_lru, c_lru, p_lru, 0, *lru_args, h_ctx, GRID_W)

    taps, norm = _hyena_filter(n_lat, hy_f_w1[l], hy_f_b1[l], hy_f_w2[l], hy_f_b2[l], hy_f_w3[l],
                               hy_f_b3[l], hy_f_wout[l], hy_f_freq[l])
    kspec = _filter_spectrum(taps, norm, mats, c_hy)
    yspec = _conv_forward(wf, kspec, mats)
    y_hy = _conv_inverse(yspec, wf, x0, hy_bias[l], mats)

    x1 = _out_projection(y_hy.reshape(bsz * n_lat, c_hy), y_lru.reshape(bsz * n_lat, c_lru), w_out_b,
                         b_out[l], x.reshape(bsz * n_lat, d), gt1, n_lat)
    x1 = x1.reshape(bsz, n_lat, d)

    h2, aff = _router(x1, g_ffn[l], sh2, sc2, w_router[l])
    idx, gates, dest, span = _select_tokens(aff, cap)
    rows_g = (idx + (jnp.arange(bsz, dtype=jnp.int32) * n_lat)[:, None, None])
    expert_major = lambda v: jnp.swapaxes(v, 0, 1).reshape(-1)
    xs = _gather_rows(expert_major(rows_g), h2)
    hmid = _expert_up(xs, w_exp_gate[l], w_exp_up[l], bsz * cap)
    ys = _expert_down(expert_major(dest), hmid, w_exp_down[l], expert_major(gates).reshape(-1, 1), bsz * cap)
    slots = n_exp * cap
    base = (jnp.arange(bsz, dtype=jnp.int32) * slots)[:, None]
    bounds = jnp.concatenate([span[:, 0, ::COMBINE_SUB].astype(jnp.int32) + base, base + slots], axis=1)
    bounds = bounds.reshape(-1)
    return _combine(bounds, jnp.swapaxes(span, 1, 2), ys, x1, gt2, g_final, slots)
```

```python
import functools
import math

import jax
import jax.numpy as jnp
from jax import lax
from jax.experimental import pallas as pl
from jax.experimental.pallas import tpu as pltpu

F32 = jnp.float32
BF16 = jnp.bfloat16

GRID_W = 64
HY_SHORT_LEFT = 1
FILT_BANDS = 16
DECAY_TARGET = 1e-2
MIN_DECAY = math.log(DECAY_TARGET) / 1.5
MAX_DECAY = math.log(DECAY_TARGET) / 0.3
LRU_HEADS = 16
LRU_CONV_LEFT = 2
LRU_C = 8.0
CAPACITY_FACTOR = 2
N_MOD = 6
EPS = 1e-6

LANES = 128
SUBLANES = 8
VMEM_LIMIT = 56 * 2 ** 20


def _params(sem):
    return pltpu.CompilerParams(dimension_semantics=sem, vmem_limit_bytes=VMEM_LIMIT)


def _tile(n, pref):
    t = min(n, pref)
    while n % t:
        t //= 2
    return t


def _split_bf16(v):
    hi = v.astype(BF16)
    lo = (v - hi.astype(F32)).astype(BF16)
    return hi, lo


def _mod_kernel(c_ref, w_ref, b_ref, o_ref):
    c = c_ref[...]
    s = c * jax.nn.sigmoid(c)
    s_hi, s_lo = _split_bf16(s)
    w_hi, w_lo = _split_bf16(w_ref[...])
    rows = c.shape[0]
    r = jnp.dot(jnp.concatenate([s_hi, s_lo], axis=0), w_hi, preferred_element_type=F32)
    acc = r[:rows] + r[rows:] + jnp.dot(s_hi, w_lo, preferred_element_type=F32)
    o_ref[...] = acc + b_ref[...]


def _modulation(c_all, w_mod, b_mod):
    rows, d = c_all.shape
    n = w_mod.shape[1]
    tn = _tile(n, 512)
    return pl.pallas_call(
        _mod_kernel,
        grid=(n // tn,),
        in_specs=[pl.BlockSpec((rows, d), lambda j: (0, 0)),
                  pl.BlockSpec((d, tn), lambda j: (0, j)),
                  pl.BlockSpec((1, tn), lambda j: (0, j))],
        out_specs=pl.BlockSpec((rows, tn), lambda j: (0, j)),
        out_shape=jax.ShapeDtypeStruct((rows, n), F32),
        compiler_params=_params(("parallel",)),
        name="modulation",
    )(c_all, w_mod, b_mod.reshape(1, n))


def _rms_mod(x, g, shift, scale):
    y = x * lax.rsqrt(jnp.mean(x * x, axis=-1, keepdims=True) + EPS) * g
    return y * (1.0 + scale) + shift


def _norm_kernel(x_ref, g_ref, sh_ref, sc_ref, o_ref):
    o_ref[0] = _rms_mod(x_ref[0], g_ref[...], sh_ref[0], sc_ref[0]).astype(o_ref.dtype)


def _norm_mod(x, g, shift, scale, per_batch):
    b, t, d = x.shape
    tt = _tile(t, 512)
    mod_map = (lambda i, j: (i, 0, 0)) if per_batch else (lambda i, j: (0, 0, 0))
    return pl.pallas_call(
        _norm_kernel,
        grid=(b, t // tt),
        in_specs=[pl.BlockSpec((1, tt, d), lambda i, j: (i, j, 0)),
                  pl.BlockSpec((1, d), lambda i, j: (0, 0)),
                  pl.BlockSpec((1, 1, d), mod_map),
                  pl.BlockSpec((1, 1, d), mod_map)],
        out_specs=pl.BlockSpec((1, tt, d), lambda i, j: (i, j, 0)),
        out_shape=jax.ShapeDtypeStruct((b, t, d), BF16),
        compiler_params=_params(("parallel", "parallel")),
        name="adaln_norm",
    )(x, g.reshape(1, d), shift, scale)


def _proj_kernel(a_ref, w_ref, b_ref, o_ref):
    acc = jnp.dot(a_ref[...], w_ref[...], preferred_element_type=F32)
    o_ref[...] = (acc + b_ref[...]).astype(o_ref.dtype)


def _projection(a, w, bias, col_start, n_cols, out_dtype):
    m, k = a.shape
    tm = _tile(m, 1024)
    tn = _tile(n_cols, 512)
    off = col_start // tn
    return pl.pallas_call(
        _proj_kernel,
        grid=(m // tm, n_cols // tn),
        in_specs=[pl.BlockSpec((tm, k), lambda i, j: (i, 0)),
                  pl.BlockSpec((k, tn), lambda i, j: (0, j + off)),
                  pl.BlockSpec((1, tn), lambda i, j: (0, j + off))],
        out_specs=pl.BlockSpec((tm, tn), lambda i, j: (i, j)),
        out_shape=jax.ShapeDtypeStruct((m, n_cols), out_dtype),
        compiler_params=_params(("parallel", "arbitrary")),
        name="projection",
    )(a, w, bias.reshape(1, -1))


def _short_conv(x, w_ref, bias, left, group):
    rows = x.shape[0]
    pos = lax.broadcasted_iota(jnp.int32, (rows, 1), 0) & (group - 1)
    y = bias + w_ref[left:left + 1, :] * x
    for k in range(w_ref.shape[0]):
        off = k - left
        if off == 0:
            continue
        shifted = pltpu.roll(x, (-off) % rows, axis=0)
        valid = (pos + off >= 0) & (pos + off < group)
        y = y + w_ref[k:k + 1, :] * jnp.where(valid, shifted, 0.0)
    return y


def _proj_hyena_kernel(a_ref, w0_ref, w1_ref, w2_ref, b0_ref, b1_ref, b2_ref,
                       cw0_ref, cw1_ref, cw2_ref, cb0_ref, cb1_ref, cb2_ref, x0_ref, wf_ref):
    a = a_ref[...]

    def branch(w_ref, b_ref, cw_ref, cb_ref):
        p = jnp.dot(a, w_ref[...], preferred_element_type=F32) + b_ref[...]
        return _short_conv(p, cw_ref, cb_ref[...], HY_SHORT_LEFT, GRID_W)

    x0_ref[...] = branch(w0_ref, b0_ref, cw0_ref, cb0_ref)
    wf_ref[...] = branch(w1_ref, b1_ref, cw1_ref, cb1_ref) * branch(w2_ref, b2_ref, cw2_ref, cb2_ref)


def _projection_hyena(a, w, bias, conv_w, conv_b, c):
    m, k = a.shape
    tm = _tile(m, 1024)
    assert tm % GRID_W == 0
    tc = _tile(c, 256)
    nc = c // tc
    kw = conv_w.shape[0]
    wspec = lambda g: pl.BlockSpec((k, tc), lambda i, j: (0, j + g * nc))
    bspec = lambda g: pl.BlockSpec((1, tc), lambda i, j: (0, j + g * nc))
    cwspec = lambda g: pl.BlockSpec((kw, tc), lambda i, j: (0, j + g * nc))
    ospec = pl.BlockSpec((tm, tc), lambda i, j: (i, j))
    b2 = bias.reshape(1, -1)
    cb = conv_b.reshape(1, -1)
    shape = jax.ShapeDtypeStruct((m, c), F32)
    return pl.pallas_call(
        _proj_hyena_kernel,
        grid=(m // tm, nc),
        in_specs=[pl.BlockSpec((tm, k), lambda i, j: (i, 0)),
                  wspec(0), wspec(1), wspec(2), bspec(0), bspec(1), bspec(2),
                  cwspec(0), cwspec(1), cwspec(2), bspec(0), bspec(1), bspec(2)],
        out_specs=[ospec, ospec],
        out_shape=[shape, shape],
        compiler_params=_params(("parallel", "arbitrary")),
        name="projection_hyena",
    )(a, w, w, w, b2, b2, b2, conv_w, conv_w, conv_w, cb, cb, cb)


def _hp_dot(a, b):
    return jnp.dot(a, b, preferred_element_type=F32, precision=lax.Precision.HIGHEST)


def _filter_kernel(n, w1t_ref, w1c_ref, w1s_ref, b1_ref, w2_ref, b2_ref, w3_ref, b3_ref,
                   wout_ref, freq_ref, delta_ref, h_ref, norm_ref):
    i = pl.program_id(0)
    tn = h_ref.shape[0]
    pos_i = i * tn + lax.broadcasted_iota(jnp.int32, (tn, 1), 0)
    pos = pos_i.astype(F32)
    t = pos * (1.0 / (n - 1))
    band_step = (FILT_BANDS - 1 - 1e-4) / (FILT_BANDS - 1)
    bands = 1e-4 + band_step * lax.broadcasted_iota(jnp.int32, (1, FILT_BANDS), 1).astype(F32)
    ang = (2.0 * math.pi * pos / n) * bands
    fr = freq_ref[...]
    pre = t * w1t_ref[...] + _hp_dot(jnp.cos(ang), w1c_ref[...]) - _hp_dot(jnp.sin(ang), w1s_ref[...])
    h = jnp.sin(fr * (pre + b1_ref[...]))
    h = jnp.sin(fr * (_hp_dot(h, w2_ref[...]) + b2_ref[...]))
    h = jnp.sin(fr * (_hp_dot(h, w3_ref[...]) + b3_ref[...]))
    taps = _hp_dot(h, wout_ref[...]) * jnp.exp(-t * delta_ref[...])

    @pl.when(i == 0)
    def _():
        norm_ref[...] = jnp.zeros_like(norm_ref)

    norm_ref[...] += jnp.sum(jnp.abs(taps), axis=0, keepdims=True)
    c = taps.shape[1] // 2
    col = lax.broadcasted_iota(jnp.int32, (1, taps.shape[1]), 1)
    drop = (pos_i == 0) & (col >= c)
    h_ref[...] = jnp.where(drop, 0.0, taps).astype(h_ref.dtype)


def _hyena_filter(n, w1, b1, w2, b2, w3, b3, wout, freq):
    hid = w1.shape[1]
    c2 = wout.shape[1]
    c = c2 // 2
    tn = _tile(n, 512)
    deltas = jnp.abs(jnp.linspace(MIN_DECAY, MAX_DECAY, c, dtype=F32))
    deltas = jnp.concatenate([deltas, deltas]).reshape(1, c2)
    full = lambda shape: pl.BlockSpec(shape, lambda i: (0, 0))
    return pl.pallas_call(
        functools.partial(_filter_kernel, n),
        grid=(n // tn,),
        in_specs=[full((1, hid)), full((FILT_BANDS, hid)), full((FILT_BANDS, hid)), full((1, hid)),
                  full((hid, hid)), full((1, hid)), full((hid, hid)), full((1, hid)),
                  full((hid, c2)), full((1, hid)), full((1, c2))],
        out_specs=[pl.BlockSpec((tn, c2), lambda i: (i, 0)), full((1, c2))],
        out_shape=[jax.ShapeDtypeStruct((n, c2), F32), jax.ShapeDtypeStruct((1, c2), F32)],
        compiler_params=_params(("arbitrary",)),
        name="hyena_filter",
    )(w1[0:1], w1[1:1 + FILT_BANDS], w1[1 + FILT_BANDS:], b1.reshape(1, hid), w2, b2.reshape(1, hid),
      w3, b3.reshape(1, hid), wout, freq.reshape(1, hid), deltas)


FFT_P = LANES
FFT_R = SUBLANES


def _phase(num, den):
    return (num % den).astype(F32) * (2.0 * math.pi / den)


def _fft_matrices(n):
    p, r = FFT_P, FFT_R
    q = n // p
    n2 = 2 * n
    iq = jnp.arange(q, dtype=jnp.int32)
    eye = jnp.eye(r, dtype=F32)
    alt = lambda v: (1 - 2 * (v & 1)).astype(F32)

    ang = _phase(iq[:, None] * iq[None, :], 2 * q)
    a_re = jnp.cos(ang)
    a_im = (-jnp.sin(ang)).at[0].set(alt(iq))
    core = jnp.stack([a_re, a_im], axis=1)
    m1 = jnp.einsum('fks,rt->fkrst', core, eye).reshape(q * 2 * r, q * r)

    b_c = jnp.cos(ang.T)
    b_s = (-jnp.sin(ang.T)).at[:, 0].set(alt(iq))
    core = jnp.stack([b_c, b_s], axis=2)
    i2 = jnp.einsum('tfk,rs->trfks', core, eye).reshape(q * r, q * 2 * r)

    h = p // 2
    f2 = jnp.arange(h, dtype=jnp.int32)
    s2 = jnp.arange(p, dtype=jnp.int32)
    f_lo = iq[:, None] + 2 * q * f2[None, :]
    f_hi = jnp.where(iq[:, None] == 0, q, 2 * q - iq[:, None]) + 2 * q * f2[None, :]
    freq = jnp.stack([f_lo, f_hi], axis=1)
    phi = _phase(freq[..., None] * s2, n2)
    c, s = jnp.cos(phi), jnp.sin(phi)
    zero = jnp.zeros_like(c[:, 0])
    first = (iq == 0)[:, None, None]
    dcrow = (first & (f2 == 0)[None, :, None])
    nyq = jnp.broadcast_to(alt(s2), c[:, 0].shape)
    on_gr = jnp.stack([jnp.where(first, c[:, 0], c[:, 0]),
                       jnp.where(dcrow, nyq, -s[:, 0]),
                       jnp.where(first, zero, c[:, 1]),
                       jnp.where(first, zero, -s[:, 1])], axis=1)
    on_gi = jnp.stack([jnp.where(first, zero, s[:, 0]),
                       jnp.where(first, zero, c[:, 0]),
                       jnp.where(first, c[:, 1], -s[:, 1]),
                       jnp.where(first, -s[:, 1], -c[:, 1])], axis=1)
    m2 = jnp.concatenate([on_gr, on_gi], axis=-1).reshape(q, 4 * h, 2 * p)
    ct, st = jnp.swapaxes(c, 2, 3), jnp.swapaxes(s, 2, 3)
    zt = jnp.zeros_like(ct[:, 0])
    dccol = (first & (f2 == 0)[None, None, :])
    nyq_t = jnp.broadcast_to(alt(s2)[:, None], ct[:, 0].shape)
    hc = jnp.stack([ct[:, 0], jnp.where(dccol, nyq_t, -st[:, 0]),
                    jnp.where(first, zt, ct[:, 1]), jnp.where(first, zt, -st[:, 1])], axis=2)
    hs = jnp.stack([jnp.where(first, zt, st[:, 0]), jnp.where(first, zt, ct[:, 0]),
                    jnp.where(first, ct[:, 1], -st[:, 1]), jnp.where(first, -st[:, 1], -ct[:, 1])], axis=2)
    i1 = jnp.stack([hc, hs], axis=1).reshape(q, 2 * p, 4 * h)
    return tuple(m.astype(BF16) for m in (m1, m2, i1, i2))


def _fft_stage_a(u_ref, m1_ref, g_ref):
    q, groups, r, tc = u_ref.shape
    m1 = m1_ref[...]
    for g in range(groups):
        blk = u_ref[:, g].reshape(q * r, tc).astype(BF16)
        g_ref[g] = jnp.dot(m1, blk, preferred_element_type=F32).reshape(q, 2, r, tc)


def _fft_stage_b(g_ref, f1, m2):
    groups, _, _, r, tc = g_ref.shape
    z = jnp.concatenate([g_ref[:, f1, 0].reshape(groups * r, tc),
                         g_ref[:, f1, 1].reshape(groups * r, tc)], axis=0).astype(BF16)
    return jnp.dot(m2, z, preferred_element_type=F32)


def _cmul_packed(x, k, first):
    h = x.shape[0] // 4
    xr = (x[0:h], x[2 * h:3 * h])
    xi = (x[h:2 * h], x[3 * h:4 * h])
    kr = (k[0:h], k[2 * h:3 * h])
    ki = (k[h:2 * h], k[3 * h:4 * h])
    real_pair = first & (lax.broadcasted_iota(jnp.int32, (h, 1), 0) == 0)
    lo_re = jnp.where(real_pair, xr[0] * kr[0], xr[0] * kr[0] - xi[0] * ki[0])
    lo_im = jnp.where(real_pair, xi[0] * ki[0], xr[0] * ki[0] + xi[0] * kr[0])
    return jnp.concatenate([lo_re, lo_im, xr[1] * kr[1] - xi[1] * ki[1], xr[1] * ki[1] + xi[1] * kr[1]], axis=0)


def _filter_spec_kernel(n, hf_ref, hb_ref, nf_ref, nb_ref, m1_ref, m2_ref, k_ref, gf_ref, gb_ref):
    qi = pl.program_id(1)
    qb = m2_ref.shape[0]

    @pl.when(qi == 0)
    def _():
        _fft_stage_a(hf_ref, m1_ref, gf_ref)
        _fft_stage_a(hb_ref, m1_ref, gb_ref)

    inv_norm = 1.0 / (nf_ref[...] + nb_ref[...])
    h = m2_ref.shape[1] // 4
    row = lax.broadcasted_iota(jnp.int32, (4 * h, 1), 0)
    for k in range(qb):
        f1 = qi * qb + k
        xf = _fft_stage_b(gf_ref, f1, m2_ref[k])
        xb = _fft_stage_b(gb_ref, f1, m2_ref[k])
        real_pair = (f1 == 0) & ((row == 0) | (row == h))
        imag_row = ((row >= h) & (row < 2 * h)) | (row >= 3 * h)
        spec = jnp.where(imag_row & jnp.logical_not(real_pair), xf - xb, xf + xb)
        spec = spec * jnp.where(real_pair, 0.5 / n, 1.0 / n) * inv_norm
        k_ref[k] = spec.reshape(k_ref.shape[1:]).astype(k_ref.dtype)


def _time_view(v, n):
    return v.reshape(v.shape[:-2] + (n // FFT_P, FFT_P // FFT_R, FFT_R, v.shape[-1]))


def _filter_spectrum(taps, norm, mats, c):
    n = taps.shape[0]
    m1, m2, _, _ = mats
    q = n // FFT_P
    h = FFT_P // 2
    tc = _tile(c, 256)
    nc = c // tc
    qb = _tile(q, 8)
    tv = _time_view(taps, n)
    blk = (q, FFT_P // FFT_R, FFT_R, tc)
    return pl.pallas_call(
        functools.partial(_filter_spec_kernel, n),
        grid=(nc, q // qb),
        in_specs=[pl.BlockSpec(blk, lambda j, i: (0, 0, 0, j)),
                  pl.BlockSpec(blk, lambda j, i: (0, 0, 0, j + nc)),
                  pl.BlockSpec((1, tc), lambda j, i: (0, j)),
                  pl.BlockSpec((1, tc), lambda j, i: (0, j + nc)),
                  pl.BlockSpec(m1.shape, lambda j, i: (0, 0)),
                  pl.BlockSpec((qb,) + m2.shape[1:], lambda j, i: (i, 0, 0))],
        out_specs=pl.BlockSpec((qb, 4, h, tc), lambda j, i: (i, 0, 0, j)),
        out_shape=jax.ShapeDtypeStruct((q, 4, h, c), BF16),
        scratch_shapes=[pltpu.VMEM((FFT_P // FFT_R, q, 2, FFT_R, tc), F32) for _ in range(2)],
        compiler_params=_params(("parallel", "arbitrary")),
        name="filter_spectrum",
    )(tv, tv, norm, norm, m1, m2)


def _conv_fwd_kernel(w_ref, m1_ref, m2_ref, k_ref, y_ref, g_ref):
    qi = pl.program_id(2)
    qb = m2_ref.shape[0]

    @pl.when(qi == 0)
    def _():
        _fft_stage_a(w_ref.at[0], m1_ref, g_ref)

    for k in range(qb):
        f1 = qi * qb + k
        x = _fft_stage_b(g_ref, f1, m2_ref[k])
        y = _cmul_packed(x, k_ref[k].reshape(x.shape).astype(F32), f1 == 0)
        y_ref[0, k] = y.reshape(y_ref.shape[2:]).astype(y_ref.dtype)


def _conv_forward(wf, kspec, mats):
    b, n, c = wf.shape
    m1, m2, _, _ = mats
    q = n // FFT_P
    h = FFT_P // 2
    tc = _tile(c, 512)
    qb = _tile(q, 8)
    return pl.pallas_call(
        _conv_fwd_kernel,
        grid=(c // tc, b, q // qb),
        in_specs=[pl.BlockSpec((1, q, FFT_P // FFT_R, FFT_R, tc), lambda j, bi, i: (bi, 0, 0, 0, j)),
                  pl.BlockSpec(m1.shape, lambda j, bi, i: (0, 0)),
                  pl.BlockSpec((qb,) + m2.shape[1:], lambda j, bi, i: (i, 0, 0)),
                  pl.BlockSpec((qb, 4, h, tc), lambda j, bi, i: (i, 0, 0, j))],
        out_specs=pl.BlockSpec((1, qb, 4, h, tc), lambda j, bi, i: (bi, i, 0, 0, j)),
        out_shape=jax.ShapeDtypeStruct((b, q, 4, h, c), BF16),
        scratch_shapes=[pltpu.VMEM((FFT_P // FFT_R, q, 2, FFT_R, tc), F32)],
        compiler_params=_params(("parallel", "parallel", "arbitrary")),
        name="conv_forward_fft",
    )(_time_view(wf, n), m1, m2, kspec)


def _conv_inv_kernel(y_ref, i1_ref, i2_ref, wf_ref, x0_ref, bias_ref, o_ref, h_ref):
    qi = pl.program_id(2)
    qb = i1_ref.shape[0]
    groups, q, _, r, tc = h_ref.shape
    for k in range(qb):
        hv = jnp.dot(i1_ref[k], y_ref[0, k].reshape(i1_ref.shape[2], tc), preferred_element_type=F32)
        half = hv.shape[0] // 2
        h_ref[:, qi * qb + k, 0] = hv[:half].reshape(groups, r, tc)
        h_ref[:, qi * qb + k, 1] = hv[half:].reshape(groups, r, tc)

    @pl.when(qi == pl.num_programs(2) - 1)
    def _():
        i2 = i2_ref[...]
        bias = bias_ref[...]
        for g2 in range(groups // 2):
            parts = []
            for g in (2 * g2, 2 * g2 + 1):
                z = jnp.dot(i2, h_ref[g].reshape(q * 2 * r, tc).astype(BF16), preferred_element_type=F32)
                z = z.reshape(q, r, tc)
                parts.append(x0_ref[0, :, g] * (z + wf_ref[0, :, g] * bias))
            o_ref[0, :, g2] = jnp.concatenate(parts, axis=1).astype(o_ref.dtype)


def _conv_inverse(yspec, wf, x0, bias, mats):
    b, n, c = wf.shape
    _, _, i1, i2 = mats
    q = n // FFT_P
    h = FFT_P // 2
    groups = FFT_P // FFT_R
    tc = _tile(c, 256)
    qb = _tile(q, 8)
    tspec = pl.BlockSpec((1, q, groups, FFT_R, tc), lambda j, bi, i: (bi, 0, 0, 0, j))
    out = pl.pallas_call(
        _conv_inv_kernel,
        grid=(c // tc, b, q // qb),
        in_specs=[pl.BlockSpec((1, qb, 4, h, tc), lambda j, bi, i: (bi, i, 0, 0, j)),
                  pl.BlockSpec((qb,) + i1.shape[1:], lambda j, bi, i: (i, 0, 0)),
                  pl.BlockSpec(i2.shape, lambda j, bi, i: (0, 0)),
                  tspec, tspec,
                  pl.BlockSpec((1, tc), lambda j, bi, i: (0, j))],
        out_specs=pl.BlockSpec((1, q, groups // 2, 2 * FFT_R, tc), lambda j, bi, i: (bi, 0, 0, 0, j)),
        out_shape=jax.ShapeDtypeStruct((b, q, groups // 2, 2 * FFT_R, c), BF16),
        scratch_shapes=[pltpu.VMEM((groups, q, 2, FFT_R, tc), F32)],
        compiler_params=_params(("parallel", "parallel", "arbitrary")),
        name="conv_inverse_fft",
    )(yspec, i1, i2, _time_view(wf, n), _time_view(x0, n), bias.reshape(1, c))
    return out.reshape(b, n, c)


SCAN_UNROLL = 4


def _tile_scan(a, b, reverse):
    rows, c = a.shape
    a3 = a.reshape(rows // SUBLANES, SUBLANES, c)
    b3 = b.reshape(rows // SUBLANES, SUBLANES, c)
    sub = lax.broadcasted_iota(jnp.int32, (1, SUBLANES, 1), 1)
    for k in (1, 2, 4):
        shift = SUBLANES - k if reverse else k
        valid = (sub < SUBLANES - k) if reverse else (sub >= k)
        b3 = b3 + a3 * jnp.where(valid, pltpu.roll(b3, shift, axis=1), 0.0)
        a3 = a3 * jnp.where(valid, pltpu.roll(a3, shift, axis=1), 1.0)
    return a3.reshape(rows, c), b3.reshape(rows, c)


def _lru_kernel(group, has_gate, *refs):
    if has_gate:
        (x_ref, gate_ref, cw_ref, cb_ref, wa_ref, ba_ref, wx_ref, bx_ref, lam_ref, h0_ref,
         y_ref, ht_ref, af_ref, bf_ref, ab_ref, bb_ref) = refs
    else:
        (x_ref, cw_ref, cb_ref, wa_ref, ba_ref, wx_ref, bx_ref, lam_ref, h0_ref,
         ht_ref, af_ref, bf_ref, ab_ref, bb_ref) = refs
    t_len = x_ref.shape[1]
    chunk = _tile(t_len, 512)
    a_refs = (af_ref, ab_ref)
    b_refs = (bf_ref, bb_ref)

    def coeffs(ci, carry):
        r0 = pl.multiple_of(ci * chunk, chunk)
        xc = _short_conv(x_ref[0, pl.ds(r0, chunk), :], cw_ref, cb_ref[...], LRU_CONV_LEFT, group)
        xcb = xc.astype(BF16)
        half_xc = 0.5 * xc
        for d in range(2):
            ta = jnp.tanh(jnp.dot(xcb, wa_ref[d, 0], preferred_element_type=F32) + ba_ref[d])
            ti = jnp.tanh(jnp.dot(xcb, wx_ref[d, 0], preferred_element_type=F32) + bx_ref[d])
            lam = lam_ref[d]
            softplus_neg = jnp.maximum(-lam, 0.0) + jnp.log(1.0 + jnp.exp(-jnp.abs(lam)))
            c1 = (-0.5 * LRU_C) * softplus_neg
            a = jnp.exp(c1 * ta + c1)
            a_tile, b_tile = _tile_scan(a, jnp.sqrt(1.0 - a * a) * (half_xc * ti + half_xc), d == 1)
            a_refs[d][pl.ds(r0, chunk), :] = a_tile
            b_refs[d][pl.ds(r0, chunk), :] = b_tile
        return carry

    lax.fori_loop(0, t_len // chunk, coeffs, 0)

    n_tiles = t_len // SUBLANES
    unroll = _tile(n_tiles, SCAN_UNROLL)

    def scan(i, carry):
        hf, hb = carry
        for u in range(unroll):
            kf = i * unroll + u
            rf = pl.multiple_of(kf * SUBLANES, SUBLANES)
            tile_f = af_ref[pl.ds(rf, SUBLANES), :] * hf + bf_ref[pl.ds(rf, SUBLANES), :]
            bf_ref[pl.ds(rf, SUBLANES), :] = tile_f
            hf = tile_f[SUBLANES - 1:SUBLANES, :]
            rb = pl.multiple_of((n_tiles - 1 - kf) * SUBLANES, SUBLANES)
            tile_b = ab_ref[pl.ds(rb, SUBLANES), :] * hb + bb_ref[pl.ds(rb, SUBLANES), :]
            bb_ref[pl.ds(rb, SUBLANES), :] = tile_b
            hb = tile_b[0:1, :]
        return hf, hb

    hf, hb = lax.fori_loop(0, n_tiles // unroll, scan, (h0_ref[0, 0:1, :], h0_ref[0, 1:2, :]))
    ht_ref[0, 0:1, :] = hf
    ht_ref[0, 1:2, :] = hb

    if has_gate:
        def emit(ci, carry):
            r0 = pl.multiple_of(ci * chunk, chunk)
            hs = bf_ref[pl.ds(r0, chunk), :] + bb_ref[pl.ds(r0, chunk), :]
            y_ref[0, pl.ds(r0, chunk), :] = (jax.nn.gelu(gate_ref[0, pl.ds(r0, chunk), :]) * hs).astype(y_ref.dtype)
            return carry

        lax.fori_loop(0, t_len // chunk, emit, 0)


def _block_diag(w, heads_per_tile):
    d2, h, hd, _ = w.shape
    w = w.reshape(d2, h // heads_per_tile, heads_per_tile, hd, hd)
    eye = jnp.eye(heads_per_tile, dtype=w.dtype)
    bd = jnp.einsum('dghij,hq->dghiqj', w, eye)
    return bd.reshape(d2, h // heads_per_tile, heads_per_tile * hd, heads_per_tile * hd).astype(BF16)


def _rglru(xsrc, x_col, gsrc, g_col, conv_w, conv_b, wa, ba, wx, bx, lam, h0, group):
    b, t, _ = xsrc.shape
    c = conv_w.shape[1]
    hd = c // LRU_HEADS
    tc = min(c, max(hd, 256))
    hp = tc // hd
    nt = c // tc
    has_gate = gsrc is not None
    wa_bd = _block_diag(0.5 * wa, hp)
    wx_bd = _block_diag(0.5 * wx, hp)
    ba, bx = 0.5 * ba, 0.5 * bx
    xo, go = x_col // tc, (g_col // tc if has_gate else 0)
    vec = lambda: pl.BlockSpec((2, 1, tc), lambda i, j: (0, 0, j))
    mat = lambda: pl.BlockSpec((2, 1, tc, tc), lambda i, j: (0, j, 0, 0))
    in_specs = [pl.BlockSpec((1, t, tc), lambda i, j: (i, 0, j + xo))]
    args = [xsrc]
    if has_gate:
        in_specs.append(pl.BlockSpec((1, t, tc), lambda i, j: (i, 0, j + go)))
        args.append(gsrc)
    in_specs += [pl.BlockSpec((conv_w.shape[0], tc), lambda i, j: (0, j)),
                 pl.BlockSpec((1, tc), lambda i, j: (0, j)),
                 mat(), vec(), mat(), vec(), vec(),
                 pl.BlockSpec((1, 2, tc), lambda i, j: (i, 0, j))]
    args += [conv_w, conv_b.reshape(1, c), wa_bd, ba.reshape(2, 1, c), wx_bd, bx.reshape(2, 1, c),
             lam.reshape(2, 1, c), h0]
    ht_spec = pl.BlockSpec((1, 2, tc), lambda i, j: (i, 0, j))
    ht_shape = jax.ShapeDtypeStruct((b, 2, c), F32)
    if has_gate:
        out_specs = [pl.BlockSpec((1, t, tc), lambda i, j: (i, 0, j)), ht_spec]
        out_shape = [jax.ShapeDtypeStruct((b, t, c), BF16), ht_shape]
    else:
        out_specs = [ht_spec]
        out_shape = [ht_shape]
    return pl.pallas_call(
        functools.partial(_lru_kernel, group, has_gate),
        grid=(b, nt),
        in_specs=in_specs,
        out_specs=out_specs,
        out_shape=out_shape,
        scratch_shapes=[pltpu.VMEM((t, tc), F32) for _ in range(4)],
        compiler_params=_params(("parallel", "parallel")),
        name="rglru" if has_gate else "rglru_context",
    )(*args)


def _out_proj_kernel(a1_ref, a2_ref, w1_ref, w2_ref, b_ref, x_ref, g_ref, o_ref):
    acc = jnp.dot(a1_ref[...], w1_ref[...], preferred_element_type=F32)
    acc = acc + jnp.dot(a2_ref[...], w2_ref[...], preferred_element_type=F32)
    o_ref[...] = x_ref[...] + g_ref[0] * (acc + b_ref[...])


def _out_projection(a1, a2, w, bias, x, gate, t):
    m, k1 = a1.shape
    n = w.shape[1]
    tm = _tile(t, 1024)
    tn = _tile(n, 512)
    per_b = t // tm
    k1_blocks = 1
    return pl.pallas_call(
        _out_proj_kernel,
        grid=(m // tm, n // tn),
        in_specs=[pl.BlockSpec((tm, k1), lambda i, j: (i, 0)),
                  pl.BlockSpec((tm, k1), lambda i, j: (i, 0)),
                  pl.BlockSpec((k1, tn), lambda i, j: (0, j)),
                  pl.BlockSpec((k1, tn), lambda i, j: (k1_blocks, j)),
                  pl.BlockSpec((1, tn), lambda i, j: (0, j)),
                  pl.BlockSpec((tm, tn), lambda i, j: (i, j)),
                  pl.BlockSpec((1, 1, tn), lambda i, j: (i // per_b, 0, j))],
        out_specs=pl.BlockSpec((tm, tn), lambda i, j: (i, j)),
        out_shape=jax.ShapeDtypeStruct((m, n), F32),
        compiler_params=_params(("parallel", "arbitrary")),
        name="out_projection",
    )(a1, a2, w, w, bias.reshape(1, n), x, gate)


def _router_kernel(x_ref, g_ref, sh_ref, sc_ref, wr_ref, h_ref, aff_ref):
    h = _rms_mod(x_ref[0], g_ref[...], sh_ref[0], sc_ref[0])
    half = h.shape[1] // 2
    h_ref[...] = pltpu.pack_elementwise([h[:, :half], h[:, half:]], packed_dtype=BF16)
    h_hi, h_lo = _split_bf16(h)
    w_hi, w_lo = _split_bf16(wr_ref[...])
    nt = (((1,), (1,)), ((), ()))
    logits = (lax.dot_general(w_hi, h_hi, nt, preferred_element_type=F32)
              + lax.dot_general(w_lo, h_hi, nt, preferred_element_type=F32)
              + lax.dot_general(w_hi, h_lo, nt, preferred_element_type=F32))
    z = jnp.exp(logits - jnp.max(logits, axis=0, keepdims=True))
    aff_ref[0] = z / jnp.sum(z, axis=0, keepdims=True)


def _router(x1, g, shift, scale, w_router):
    b, t, d = x1.shape
    e = w_router.shape[1]
    tt = _tile(t, 512)
    per_b = t // tt
    mod_map = lambda i, j: (i, 0, 0)
    return pl.pallas_call(
        _router_kernel,
        grid=(b, per_b),
        in_specs=[pl.BlockSpec((1, tt, d), lambda i, j: (i, j, 0)),
                  pl.BlockSpec((1, d), lambda i, j: (0, 0)),
                  pl.BlockSpec((1, 1, d), mod_map),
                  pl.BlockSpec((1, 1, d), mod_map),
                  pl.BlockSpec((e, d), lambda i, j: (0, 0))],
        out_specs=[pl.BlockSpec((tt, d // 2), lambda i, j: (i * per_b + j, 0)),
                   pl.BlockSpec((1, e, tt), lambda i, j: (i, 0, j))],
        out_shape=[jax.ShapeDtypeStruct((b * t, d // 2), jnp.uint32),
                   jax.ShapeDtypeStruct((b, e, t), F32)],
        compiler_params=_params(("parallel", "parallel")),
        name="router",
    )(x1, g.reshape(1, d), shift, scale, w_router.T)


def _prefix_count(mask_ref, out_ref):
    e, t = mask_ref.shape
    blk = min(t, LANES)
    tri = (lax.broadcasted_iota(jnp.int32, (blk, blk), 0)
           < lax.broadcasted_iota(jnp.int32, (blk, blk), 1)).astype(BF16)
    carry = jnp.zeros((e, 1), F32)
    for k in range(t // blk):
        m = mask_ref[:, k * blk:(k + 1) * blk]
        out_ref[:, k * blk:(k + 1) * blk] = jnp.dot(m.astype(BF16), tri, preferred_element_type=F32) + carry
        carry = carry + jnp.sum(m, axis=1, keepdims=True)
    return carry


def _topk_kernel(cap, aff_ref, idx_ref, gate_ref, dest_ref, span_ref, mask_ref, pos_ref, cnt_ref, q_ref):
    a = aff_ref[0]
    e, t = a.shape
    min_normal = 0x00800000

    def refine(i, thr):
        cand = thr | jnp.left_shift(jnp.int32(1), 30 - i)
        cnt = jnp.sum(jnp.where(a >= pltpu.bitcast(cand, F32), 1.0, 0.0), axis=1, keepdims=True)
        return jnp.where((cnt >= cap) & (cand >= min_normal), cand, thr)

    thr = lax.fori_loop(0, 31, refine, jnp.zeros((e, 1), jnp.int32))
    above = a >= pltpu.bitcast(jnp.maximum(thr + 1, min_normal), F32)
    tied = (a >= pltpu.bitcast(thr, F32)) & jnp.logical_not(above)
    need = cap - jnp.sum(jnp.where(above, 1.0, 0.0), axis=1, keepdims=True)
    mask_ref[...] = jnp.where(tied, 1.0, 0.0)
    _prefix_count(mask_ref, pos_ref)
    sel = above | (tied & (pos_ref[...] < need))
    mask_ref[...] = jnp.where(sel, 1.0, 0.0)
    _prefix_count(mask_ref, pos_ref)

    mask = mask_ref[...]
    cnt = jnp.sum(mask, axis=0, keepdims=True)
    cnt_ref[...] = jnp.broadcast_to(cnt, cnt_ref.shape)
    _prefix_count(cnt_ref, q_ref.at[0:SUBLANES])
    first = q_ref[0:1, :]
    srow = lax.broadcasted_iota(jnp.int32, (SUBLANES, 1), 0)
    span_ref[0] = jnp.where(srow == 0, first, jnp.where(srow == 1, first + cnt, 0.0))
    lower = (lax.broadcasted_iota(jnp.int32, (e, e), 1)
             < lax.broadcasted_iota(jnp.int32, (e, e), 0)).astype(BF16)
    q_ref[...] = first + jnp.dot(lower, mask.astype(BF16), preferred_element_type=F32)

    tok = lax.broadcasted_iota(jnp.int32, (1, t), 1)
    tok_hi = (tok >> 6).astype(F32)
    tok_lo = (tok & 63).astype(F32)
    slot = lax.broadcasted_iota(jnp.int32, (cap, 1), 0).astype(F32)
    vrow = lax.broadcasted_iota(jnp.int32, (SUBLANES, 1), 0)
    dest_base = pl.program_id(0) * (e * cap)

    def compact(ei, carry):
        g = aff_ref[0, pl.ds(ei, 1), :]
        g_hi = g.astype(BF16).astype(F32)
        g_mid = (g - g_hi).astype(BF16).astype(F32)
        g_lo = g - g_hi - g_mid
        q = q_ref[pl.ds(ei, 1), :]
        q_hi = jnp.floor(q * (1.0 / 64.0))
        q_lo = q - 64.0 * q_hi
        vals = jnp.where(vrow == 0, tok_hi, jnp.where(vrow == 1, tok_lo, jnp.where(
            vrow == 2, g_hi, jnp.where(vrow == 3, g_mid, jnp.where(vrow == 4, g_lo, jnp.where(
                vrow == 5, q_hi, jnp.where(vrow == 6, q_lo, 0.0))))))).astype(BF16)
        hit = (pos_ref[pl.ds(ei, 1), :] == slot) & (mask_ref[pl.ds(ei, 1), :] > 0.0)
        onehot = jnp.where(hit, 1.0, 0.0).astype(BF16)
        res = lax.dot_general(vals, onehot, (((1,), (1,)), ((), ())), preferred_element_type=F32)
        idx_ref[0, pl.ds(ei, 1), :] = (res[0:1] * 64.0 + res[1:2]).astype(jnp.int32)
        gate_ref[0, pl.ds(ei, 1), :] = res[2:3] + res[3:4] + res[4:5]
        dest_ref[0, pl.ds(ei, 1), :] = (res[5:6] * 64.0 + res[6:7]).astype(jnp.int32) + dest_base
        return carry

    lax.fori_loop(0, e, compact, 0)


def _select_tokens(aff, cap):
    b, e, t = aff.shape
    assert t <= 64 * 256 and e * cap <= 64 * 256 and e >= SUBLANES
    spec = pl.BlockSpec((1, e, cap), lambda i: (i, 0, 0))
    return pl.pallas_call(
        functools.partial(_topk_kernel, cap),
        grid=(b,),
        in_specs=[pl.BlockSpec((1, e, t), lambda i: (i, 0, 0))],
        out_specs=[spec, spec, spec, pl.BlockSpec((1, SUBLANES, t), lambda i: (i, 0, 0))],
        out_shape=[jax.ShapeDtypeStruct((b, e, cap), jnp.int32), jax.ShapeDtypeStruct((b, e, cap), F32),
                   jax.ShapeDtypeStruct((b, e, cap), jnp.int32), jax.ShapeDtypeStruct((b, SUBLANES, t), F32)],
        scratch_shapes=[pltpu.VMEM((e, t), F32), pltpu.VMEM((e, t), F32),
                        pltpu.VMEM((SUBLANES, t), F32), pltpu.VMEM((e, t), F32)],
        compiler_params=_params(("parallel",)),
        name="select_tokens",
    )(aff)


def _gather_kernel(rows_ref, h_hbm, o_ref, buf, sem):
    i = pl.program_id(0)
    r_blk = buf.shape[1]

    def issue_block(blk, slot):
        def issue(r, carry):
            row = rows_ref[blk * r_blk + r]
            pltpu.make_async_copy(h_hbm.at[pl.ds(row, 1), :], buf.at[slot, pl.ds(r, 1), :], sem.at[slot]).start()
            return carry

        lax.fori_loop(0, r_blk, issue, 0, unroll=8)

    @pl.when(i == 0)
    def _():
        issue_block(0, 0)

    @pl.when(i + 1 < pl.num_programs(0))
    def _():
        issue_block(i + 1, (i + 1) & 1)

    slot = i & 1
    pltpu.make_async_copy(h_hbm.at[pl.ds(0, r_blk), :], buf.at[slot], sem.at[slot]).wait()
    words = buf[slot]
    half = words.shape[1]
    for h in range(2):
        part = pltpu.unpack_elementwise(words, index=h, packed_dtype=BF16, unpacked_dtype=F32)
        o_ref[:, h * half:(h + 1) * half] = part.astype(o_ref.dtype)


def _gather_rows(rows, h):
    n_rows = rows.shape[0]
    words = h.shape[1]
    d = 2 * words
    r_blk = _tile(n_rows, 512)
    grid_spec = pltpu.PrefetchScalarGridSpec(
        num_scalar_prefetch=1,
        grid=(n_rows // r_blk,),
        in_specs=[pl.BlockSpec(memory_space=pl.ANY)],
        out_specs=pl.BlockSpec((r_blk, d), lambda i, rows_ref: (i, 0)),
        scratch_shapes=[pltpu.VMEM((2, r_blk, words), h.dtype), pltpu.SemaphoreType.DMA((2,))],
    )
    return pl.pallas_call(
        _gather_kernel,
        grid_spec=grid_spec,
        out_shape=jax.ShapeDtypeStruct((n_rows, d), BF16),
        compiler_params=_params(("arbitrary",)),
        name="gather_rows",
    )(rows, h)


def _expert_up_kernel(xs_ref, wg_ref, wu_ref, h_ref):
    xs = xs_ref[...]
    g = jnp.dot(xs, wg_ref[0].astype(BF16), preferred_element_type=F32)
    u = jnp.dot(xs, wu_ref[0].astype(BF16), preferred_element_type=F32)
    h_ref[...] = (g * jax.nn.sigmoid(g) * u).astype(h_ref.dtype)


def _expert_up(xs, w_gate, w_up, rows_per_expert):
    m, d = xs.shape
    f = w_gate.shape[2]
    tm = _tile(rows_per_expert, 1024)
    per_e = rows_per_expert // tm
    tf = _tile(f, 256)
    wspec = pl.BlockSpec((1, d, tf), lambda i, j: (i // per_e, 0, j))
    return pl.pallas_call(
        _expert_up_kernel,
        grid=(m // tm, f // tf),
        in_specs=[pl.BlockSpec((tm, d), lambda i, j: (i, 0)), wspec, wspec],
        out_specs=pl.BlockSpec((tm, tf), lambda i, j: (i, j)),
        out_shape=jax.ShapeDtypeStruct((m, f), BF16),
        compiler_params=_params(("parallel", "arbitrary")),
        name="expert_up",
    )(xs, w_gate, w_up)


DOWN_COL_TILE = 1024


def _expert_down_kernel(dest_ref, h_ref, wd_ref, g_ref, ys_hbm, ybuf, sem):
    i = pl.program_id(0)
    j = pl.program_id(1)
    n_i = pl.num_programs(0)
    n_j = pl.num_programs(1)
    _, tm, dw = ybuf.shape
    hw = wd_ref.shape[2] // 2
    slot = i & 1

    def wait_scatter(s):
        pltpu.make_async_copy(ybuf.at[s], ys_hbm.at[pl.ds(0, tm), :], sem.at[s]).wait()

    @pl.when((j == 0) & (i >= 2))
    def _():
        wait_scatter(slot)

    y = jnp.dot(h_ref[...], wd_ref[0].astype(BF16), preferred_element_type=F32) * g_ref[...]
    packed = pltpu.pack_elementwise([y[:, :hw], y[:, hw:]], packed_dtype=BF16)
    for jj in range(dw // hw):
        @pl.when(j == jj)
        def _(jj=jj):
            ybuf[slot, :, jj * hw:(jj + 1) * hw] = packed

    @pl.when(j == n_j - 1)
    def _():
        def issue(r, carry):
            dst = dest_ref[i * tm + r]
            pltpu.make_async_copy(ybuf.at[slot, pl.ds(r, 1), :], ys_hbm.at[pl.ds(dst, 1), :], sem.at[slot]).start()
            return carry

        lax.fori_loop(0, tm, issue, 0, unroll=8)

        @pl.when(i == n_i - 1)
        def _():
            @pl.when(i >= 1)
            def _():
                wait_scatter(1 - slot)
            wait_scatter(slot)


def _expert_down(dest, h, w_down, gates, rows_per_expert):
    m, f = h.shape
    d = w_down.shape[2]
    tm = _tile(rows_per_expert, 1024)
    per_e = rows_per_expert // tm
    td = _tile(d, DOWN_COL_TILE)
    grid_spec = pltpu.PrefetchScalarGridSpec(
        num_scalar_prefetch=1,
        grid=(m // tm, d // td),
        in_specs=[pl.BlockSpec((tm, f), lambda i, j, dest_ref: (i, 0)),
                  pl.BlockSpec((1, f, td), lambda i, j, dest_ref: (i // per_e, 0, j)),
                  pl.BlockSpec((tm, 1), lambda i, j, dest_ref: (i, 0))],
        out_specs=pl.BlockSpec(memory_space=pl.ANY),
        scratch_shapes=[pltpu.VMEM((2, tm, d // 2), jnp.uint32), pltpu.SemaphoreType.DMA((2,))],
    )
    return pl.pallas_call(
        _expert_down_kernel,
        grid_spec=grid_spec,
        out_shape=jax.ShapeDtypeStruct((m, d // 2), jnp.uint32),
        compiler_params=_params(("arbitrary", "arbitrary")),
        name="expert_down",
    )(dest, h, w_down, gates)


def _combine_kernel(slots_per_batch, hw, tb_ref, span_ref, x_ref, gt_ref, g_ref, ys_hbm, o_ref,
                    buf, sem, acc_ref, parts_ref):
    bi = pl.program_id(0)
    i = pl.program_id(1)
    n_t = pl.num_programs(1)
    kc = buf.shape[1]
    total = ys_hbm.shape[0]
    tt = acc_ref.shape[1]
    n_sub = tt // COMBINE_SUB
    tb0 = bi * (n_t * n_sub + 1) + i * n_sub
    c_lo = tb_ref[tb0]
    c_hi = tb_ref[tb0 + n_sub]
    k_first = (c_lo >> 3) << 3
    n_chunks = (c_hi - k_first + kc - 1) // kc

    def chunk_start(c):
        return pl.multiple_of(jnp.minimum(k_first + c * kc, total - kc), SUBLANES)

    n_buf = buf.shape[0]

    def copy(c):
        slot = c & (n_buf - 1)
        return pltpu.make_async_copy(ys_hbm.at[pl.ds(chunk_start(c), kc), :], buf.at[slot], sem.at[slot])

    for ahead in range(n_buf - 1):
        @pl.when(ahead < n_chunks)
        def _(ahead=ahead):
            copy(ahead).start()

    base = (bi * slots_per_batch).astype(F32)
    acc_ref[...] = jnp.zeros_like(acc_ref)

    def body(c, carry):
        slot = c & (n_buf - 1)
        copy(c).wait()

        @pl.when(c + n_buf - 1 < n_chunks)
        def _():
            copy(c + n_buf - 1).start()

        k_lo = k_first + c * kc
        k_hi = chunk_start(c) + kc
        k = chunk_start(c) + lax.broadcasted_iota(jnp.int32, (1, kc), 1)
        k = jnp.where(k >= k_lo, k, -1).astype(F32)
        words = buf[slot]
        for half in range(2):
            part = pltpu.unpack_elementwise(words, index=half, packed_dtype=BF16, unpacked_dtype=F32)
            parts_ref[half] = part.astype(BF16)
        for s in range(n_sub):
            rows = slice(s * COMBINE_SUB, (s + 1) * COMBINE_SUB)

            @pl.when((tb_ref[tb0 + s] < k_hi) & (tb_ref[tb0 + s + 1] > k_lo))
            def _(rows=rows):
                first = span_ref[0, rows, 0:1] + base
                last = span_ref[0, rows, 1:2] + base
                onehot = jnp.where((k >= first) & (k < last), 1.0, 0.0).astype(BF16)
                for half in range(2):
                    acc_ref[half, rows] += jnp.dot(onehot, parts_ref[half], preferred_element_type=F32)
        return carry

    lax.fori_loop(0, n_chunks, body, 0)
    gate = gt_ref[0]
    g_final = g_ref[...]

    def finish(r, carry):
        rows = pl.ds(pl.multiple_of(r * FINISH_ROWS, FINISH_ROWS), FINISH_ROWS)
        lo, hi = acc_ref[0, rows, :], acc_ref[1, rows, :]
        moe = jnp.concatenate([part[:, j * hw:(j + 1) * hw] for j in range(lo.shape[1] // hw)
                               for part in (lo, hi)], axis=1)
        x = x_ref[0, rows, :] + gate * moe
        o_ref[0, rows, :] = x * lax.rsqrt(jnp.mean(x * x, axis=-1, keepdims=True) + EPS) * g_final
        return carry

    lax.fori_loop(0, tt // FINISH_ROWS, finish, 0)


COMBINE_ROWS = 512
COMBINE_SUB = 128
COMBINE_CHUNK = 256
COMBINE_RING = 4
FINISH_ROWS = 64


def _combine(tile_bounds, span, ys, x1, gate, g_final, slots_per_batch):
    b, t, d = x1.shape
    tt = _tile(t, COMBINE_ROWS)
    assert tt % COMBINE_SUB == 0
    kc = min(COMBINE_CHUNK, ys.shape[0])
    grid_spec = pltpu.PrefetchScalarGridSpec(
        num_scalar_prefetch=1,
        grid=(b, t // tt),
        in_specs=[pl.BlockSpec((1, tt, SUBLANES), lambda bi, i, tb: (bi, i, 0)),
                  pl.BlockSpec((1, tt, d), lambda bi, i, tb: (bi, i, 0)),
                  pl.BlockSpec((1, 1, d), lambda bi, i, tb: (bi, 0, 0)),
                  pl.BlockSpec((1, d), lambda bi, i, tb: (0, 0)),
                  pl.BlockSpec(memory_space=pl.ANY)],
        out_specs=pl.BlockSpec((1, tt, d), lambda bi, i, tb: (bi, i, 0)),
        scratch_shapes=[pltpu.VMEM((COMBINE_RING, kc, d // 2), jnp.uint32),
                        pltpu.SemaphoreType.DMA((COMBINE_RING,)),
                        pltpu.VMEM((2, tt, d // 2), F32), pltpu.VMEM((2, kc, d // 2), BF16)],
    )
    return pl.pallas_call(
        functools.partial(_combine_kernel, slots_per_batch, _tile(d, DOWN_COL_TILE) // 2),
        grid_spec=grid_spec,
        out_shape=jax.ShapeDtypeStruct((b, t, d), F32),
        compiler_params=_params(("arbitrary", "arbitrary")),
        name="combine",
    )(tile_bounds, span, x1, gate, g_final.reshape(1, d), ys)


def kernel(x, c, ctx, c_ctx, w_mod, b_mod, g_mix, g_ffn, w_in, b_in, hy_conv_w, hy_conv_b, hy_f_w1, hy_f_b1, hy_f_w2, hy_f_b2, hy_f_w3, hy_f_b3, hy_f_wout, hy_f_freq, hy_bias, lru_conv_w, lru_conv_b, lru_wa, lru_ba, lru_wx, lru_bx, lru_lambda, w_out, b_out, w_router, w_exp_gate, w_exp_up, w_exp_down, g_final):
    bsz, n_lat, d = x.shape
    n_ctx = ctx.shape[1]
    depth = w_mod.shape[0]
    c_hy = hy_bias.shape[1]
    c_lru = lru_conv_b.shape[1]
    in_gate = 3 * c_hy
    in_x = in_gate + c_lru
    n_exp = w_router.shape[2]
    cap = CAPACITY_FACTOR * n_lat // n_exp
    assert depth == 1, "context residual updates are only needed for depth > 1"
    assert n_lat % GRID_W == 0 and n_ctx & (n_ctx - 1) == 0

    rows = -(-(bsz + 1) // SUBLANES) * SUBLANES
    c_all = jnp.zeros((rows, d), F32).at[:bsz].set(c).at[bsz].set(c_ctx)
    mats = _fft_matrices(n_lat)

    l = 0
    mod = _modulation(c_all, w_mod[l], b_mod[l])
    mx = mod[:bsz].reshape(bsz, 1, N_MOD, d)
    sh1, sc1, gt1, sh2, sc2, gt2 = (mx[:, :, k] for k in range(N_MOD))
    mc = mod[bsz].reshape(1, 1, N_MOD, d)
    csh1, csc1 = mc[:, :, 0], mc[:, :, 1]

    w_in_b = w_in[l].astype(BF16)
    w_out_b = w_out[l].astype(BF16)

    hx = _norm_mod(x, g_mix[l], sh1, sc1, True)
    hc = _norm_mod(ctx, g_mix[l], csh1, csc1, False)
    hx = hx.reshape(bsz * n_lat, d)
    x0, wf = _projection_hyena(hx, w_in_b, b_in[l], hy_conv_w[l], hy_conv_b[l], c_hy)
    x0, wf = x0.reshape(bsz, n_lat, c_hy), wf.reshape(bsz, n_lat, c_hy)
    p_lru = _projection(hx, w_in_b, b_in[l], in_gate, 2 * c_lru, F32)
    p_lru = p_lru.reshape(bsz, n_lat, 2 * c_lru)
    pc_lx = _projection(hc.reshape(bsz * n_ctx, d), w_in_b, b_in[l], in_x, c_lru, F32)
    pc_lx = pc_lx.reshape(bsz, n_ctx, c_lru)

    lru_args = (lru_conv_w[l], lru_conv_b[l], lru_wa[l], lru_ba[l], lru_wx[l], lru_bx[l], lru_lambda[l])
    (h_ctx,) = _rglru(pc_lx, 0, None, 0, *lru_args, jnp.zeros((bsz, 2, c_lru), F32), n_ctx)
    y_lru, _ = _rglru(p_lru, c_lru, p_lru, 0, *lru_args, h_ctx, GRID_W)

    taps, norm = _hyena_filter(n_lat, hy_f_w1[l], hy_f_b1[l], hy_f_w2[l], hy_f_b2[l], hy_f_w3[l],
                               hy_f_b3[l], hy_f_wout[l], hy_f_freq[l])
    kspec = _filter_spectrum(taps, norm, mats, c_hy)
    yspec = _conv_forward(wf, kspec, mats)
    y_hy = _conv_inverse(yspec, wf, x0, hy_bias[l], mats)

    x1 = _out_projection(y_hy.reshape(bsz * n_lat, c_hy), y_lru.reshape(bsz * n_lat, c_lru), w_out_b,
                         b_out[l], x.reshape(bsz * n_lat, d), gt1, n_lat)
    x1 = x1.reshape(bsz, n_lat, d)

    h2, aff = _router(x1, g_ffn[l], sh2, sc2, w_router[l])
    idx, gates, dest, span = _select_tokens(aff, cap)
    rows_g = (idx + (jnp.arange(bsz, dtype=jnp.int32) * n_lat)[:, None, None])
    expert_major = lambda v: jnp.swapaxes(v, 0, 1).reshape(-1)
    xs = _gather_rows(expert_major(rows_g), h2)
    hmid = _expert_up(xs, w_exp_gate[l], w_exp_up[l], bsz * cap)
    ys = _expert_down(expert_major(dest), hmid, w_exp_down[l], expert_major(gates).reshape(-1, 1), bsz * cap)
    slots = n_exp * cap
    base = (jnp.arange(bsz, dtype=jnp.int32) * slots)[:, None]
    bounds = jnp.concatenate([span[:, 0, ::COMBINE_SUB].astype(jnp.int32) + base, base + slots], axis=1)
    bounds = bounds.reshape(-1)
    return _combine(bounds, jnp.swapaxes(span, 1, 2), ys, x1, gt2, g_final, slots)
```

```python
import functools
import math

import jax
import jax.numpy as jnp
from jax import lax
from jax.experimental import pallas as pl
from jax.experimental.pallas import tpu as pltpu

F32 = jnp.float32
BF16 = jnp.bfloat16

GRID_W = 64
HY_SHORT_LEFT = 1
FILT_BANDS = 16
DECAY_TARGET = 1e-2
MIN_DECAY = math.log(DECAY_TARGET) / 1.5
MAX_DECAY = math.log(DECAY_TARGET) / 0.3
LRU_HEADS = 16
LRU_CONV_LEFT = 2
LRU_C = 8.0
CAPACITY_FACTOR = 2
N_MOD = 6
EPS = 1e-6

LANES = 128
SUBLANES = 8
VMEM_LIMIT = 56 * 2 ** 20


def _params(sem):
    return pltpu.CompilerParams(dimension_semantics=sem, vmem_limit_bytes=VMEM_LIMIT)


def _tile(n, pref):
    t = min(n, pref)
    while n % t:
        t //= 2
    return t


def _split_bf16(v):
    hi = v.astype(BF16)
    lo = (v - hi.astype(F32)).astype(BF16)
    return hi, lo


def _mod_kernel(c_ref, w_ref, b_ref, o_ref):
    c = c_ref[...]
    s = c * jax.nn.sigmoid(c)
    s_hi, s_lo = _split_bf16(s)
    w_hi, w_lo = _split_bf16(w_ref[...])
    rows = c.shape[0]
    r = jnp.dot(jnp.concatenate([s_hi, s_lo], axis=0), w_hi, preferred_element_type=F32)
    acc = r[:rows] + r[rows:] + jnp.dot(s_hi, w_lo, preferred_element_type=F32)
    o_ref[...] = acc + b_ref[...]


def _modulation(c_all, w_mod, b_mod):
    rows, d = c_all.shape
    n = w_mod.shape[1]
    tn = _tile(n, 512)
    return pl.pallas_call(
        _mod_kernel,
        grid=(n // tn,),
        in_specs=[pl.BlockSpec((rows, d), lambda j: (0, 0)),
                  pl.BlockSpec((d, tn), lambda j: (0, j)),
                  pl.BlockSpec((1, tn), lambda j: (0, j))],
        out_specs=pl.BlockSpec((rows, tn), lambda j: (0, j)),
        out_shape=jax.ShapeDtypeStruct((rows, n), F32),
        compiler_params=_params(("parallel",)),
        name="modulation",
    )(c_all, w_mod, b_mod.reshape(1, n))


def _rms_mod(x, g, shift, scale):
    y = x * lax.rsqrt(jnp.mean(x * x, axis=-1, keepdims=True) + EPS) * g
    return y * (1.0 + scale) + shift


def _norm_kernel(x_ref, g_ref, sh_ref, sc_ref, o_ref):
    o_ref[0] = _rms_mod(x_ref[0], g_ref[...], sh_ref[0], sc_ref[0]).astype(o_ref.dtype)


def _norm_mod(x, g, shift, scale, per_batch):
    b, t, d = x.shape
    tt = _tile(t, 512)
    mod_map = (lambda i, j: (i, 0, 0)) if per_batch else (lambda i, j: (0, 0, 0))
    return pl.pallas_call(
        _norm_kernel,
        grid=(b, t // tt),
        in_specs=[pl.BlockSpec((1, tt, d), lambda i, j: (i, j, 0)),
                  pl.BlockSpec((1, d), lambda i, j: (0, 0)),
                  pl.BlockSpec((1, 1, d), mod_map),
                  pl.BlockSpec((1, 1, d), mod_map)],
        out_specs=pl.BlockSpec((1, tt, d), lambda i, j: (i, j, 0)),
        out_shape=jax.ShapeDtypeStruct((b, t, d), BF16),
        compiler_params=_params(("parallel", "parallel")),
        name="adaln_norm",
    )(x, g.reshape(1, d), shift, scale)


def _proj_kernel(a_ref, w_ref, b_ref, o_ref):
    acc = jnp.dot(a_ref[...], w_ref[...], preferred_element_type=F32)
    o_ref[...] = (acc + b_ref[...]).astype(o_ref.dtype)


def _projection(a, w, bias, col_start, n_cols, out_dtype):
    m, k = a.shape
    tm = _tile(m, 1024)
    tn = _tile(n_cols, 512)
    off = col_start // tn
    return pl.pallas_call(
        _proj_kernel,
        grid=(m // tm, n_cols // tn),
        in_specs=[pl.BlockSpec((tm, k), lambda i, j: (i, 0)),
                  pl.BlockSpec((k, tn), lambda i, j: (0, j + off)),
                  pl.BlockSpec((1, tn), lambda i, j: (0, j + off))],
        out_specs=pl.BlockSpec((tm, tn), lambda i, j: (i, j)),
        out_shape=jax.ShapeDtypeStruct((m, n_cols), out_dtype),
        compiler_params=_params(("parallel", "arbitrary")),
        name="projection",
    )(a, w, bias.reshape(1, -1))


def _short_conv(x, w_ref, bias, left, group):
    rows = x.shape[0]
    pos = lax.broadcasted_iota(jnp.int32, (rows, 1), 0) & (group - 1)
    y = bias + w_ref[left:left + 1, :] * x
    for k in range(w_ref.shape[0]):
        off = k - left
        if off == 0:
            continue
        shifted = pltpu.roll(x, (-off) % rows, axis=0)
        valid = (pos + off >= 0) & (pos + off < group)
        y = y + w_ref[k:k + 1, :] * jnp.where(valid, shifted, 0.0)
    return y


def _proj_hyena_kernel(a_ref, w0_ref, w1_ref, w2_ref, b0_ref, b1_ref, b2_ref,
                       cw0_ref, cw1_ref, cw2_ref, cb0_ref, cb1_ref, cb2_ref, x0_ref, wf_ref):
    a = a_ref[...]

    def branch(w_ref, b_ref, cw_ref, cb_ref):
        p = jnp.dot(a, w_ref[...], preferred_element_type=F32) + b_ref[...]
        return _short_conv(p, cw_ref, cb_ref[...], HY_SHORT_LEFT, GRID_W)

    x0_ref[...] = branch(w0_ref, b0_ref, cw0_ref, cb0_ref)
    wf_ref[...] = branch(w1_ref, b1_ref, cw1_ref, cb1_ref) * branch(w2_ref, b2_ref, cw2_ref, cb2_ref)


def _projection_hyena(a, w, bias, conv_w, conv_b, c):
    m, k = a.shape
    tm = _tile(m, 1024)
    assert tm % GRID_W == 0
    tc = _tile(c, 256)
    nc = c // tc
    kw = conv_w.shape[0]
    wspec = lambda g: pl.BlockSpec((k, tc), lambda i, j: (0, j + g * nc))
    bspec = lambda g: pl.BlockSpec((1, tc), lambda i, j: (0, j + g * nc))
    cwspec = lambda g: pl.BlockSpec((kw, tc), lambda i, j: (0, j + g * nc))
    ospec = pl.BlockSpec((tm, tc), lambda i, j: (i, j))
    b2 = bias.reshape(1, -1)
    cb = conv_b.reshape(1, -1)
    shape = jax.ShapeDtypeStruct((m, c), F32)
    return pl.pallas_call(
        _proj_hyena_kernel,
        grid=(m // tm, nc),
        in_specs=[pl.BlockSpec((tm, k), lambda i, j: (i, 0)),
                  wspec(0), wspec(1), wspec(2), bspec(0), bspec(1), bspec(2),
                  cwspec(0), cwspec(1), cwspec(2), bspec(0), bspec(1), bspec(2)],
        out_specs=[ospec, ospec],
        out_shape=[shape, shape],
        compiler_params=_params(("parallel", "arbitrary")),
        name="projection_hyena",
    )(a, w, w, w, b2, b2, b2, conv_w, conv_w, conv_w, cb, cb, cb)


def _hp_dot(a, b):
    return jnp.dot(a, b, preferred_element_type=F32, precision=lax.Precision.HIGHEST)


def _filter_kernel(n, w1t_ref, w1c_ref, w1s_ref, b1_ref, w2_ref, b2_ref, w3_ref, b3_ref,
                   wout_ref, freq_ref, delta_ref, h_ref, norm_ref):
    i = pl.program_id(0)
    tn = h_ref.shape[0]
    pos_i = i * tn + lax.broadcasted_iota(jnp.int32, (tn, 1), 0)
    pos = pos_i.astype(F32)
    t = pos * (1.0 / (n - 1))
    band_step = (FILT_BANDS - 1 - 1e-4) / (FILT_BANDS - 1)
    bands = 1e-4 + band_step * lax.broadcasted_iota(jnp.int32, (1, FILT_BANDS), 1).astype(F32)
    ang = (2.0 * math.pi * pos / n) * bands
    fr = freq_ref[...]
    pre = t * w1t_ref[...] + _hp_dot(jnp.cos(ang), w1c_ref[...]) - _hp_dot(jnp.sin(ang), w1s_ref[...])
    h = jnp.sin(fr * (pre + b1_ref[...]))
    h = jnp.sin(fr * (_hp_dot(h, w2_ref[...]) + b2_ref[...]))
    h = jnp.sin(fr * (_hp_dot(h, w3_ref[...]) + b3_ref[...]))
    h_hi, h_lo = _split_bf16(h)
    w_hi, w_lo = _split_bf16(wout_ref[...])
    proj = (jnp.dot(h_hi, w_hi, preferred_element_type=F32) + jnp.dot(h_lo, w_hi, preferred_element_type=F32)
            + jnp.dot(h_hi, w_lo, preferred_element_type=F32))
    taps = proj * jnp.exp(-t * delta_ref[...])

    @pl.when(i == 0)
    def _():
        norm_ref[...] = jnp.zeros_like(norm_ref)

    norm_ref[...] += jnp.sum(jnp.abs(taps), axis=0, keepdims=True)
    c = taps.shape[1] // 2
    col = lax.broadcasted_iota(jnp.int32, (1, taps.shape[1]), 1)
    drop = (pos_i == 0) & (col >= c)
    h_ref[...] = jnp.where(drop, 0.0, taps).astype(h_ref.dtype)


def _hyena_filter(n, w1, b1, w2, b2, w3, b3, wout, freq):
    hid = w1.shape[1]
    c2 = wout.shape[1]
    c = c2 // 2
    tn = _tile(n, 512)
    deltas = jnp.abs(jnp.linspace(MIN_DECAY, MAX_DECAY, c, dtype=F32))
    deltas = jnp.concatenate([deltas, deltas]).reshape(1, c2)
    full = lambda shape: pl.BlockSpec(shape, lambda i: (0, 0))
    return pl.pallas_call(
        functools.partial(_filter_kernel, n),
        grid=(n // tn,),
        in_specs=[full((1, hid)), full((FILT_BANDS, hid)), full((FILT_BANDS, hid)), full((1, hid)),
                  full((hid, hid)), full((1, hid)), full((hid, hid)), full((1, hid)),
                  full((hid, c2)), full((1, hid)), full((1, c2))],
        out_specs=[pl.BlockSpec((tn, c2), lambda i: (i, 0)), full((1, c2))],
        out_shape=[jax.ShapeDtypeStruct((n, c2), F32), jax.ShapeDtypeStruct((1, c2), F32)],
        compiler_params=_params(("arbitrary",)),
        name="hyena_filter",
    )(w1[0:1], w1[1:1 + FILT_BANDS], w1[1 + FILT_BANDS:], b1.reshape(1, hid), w2, b2.reshape(1, hid),
      w3, b3.reshape(1, hid), wout, freq.reshape(1, hid), deltas)


FFT_P = LANES
FFT_R = SUBLANES


def _phase(num, den):
    return (num % den).astype(F32) * (2.0 * math.pi / den)


def _fft_matrices(n):
    p, r = FFT_P, FFT_R
    q = n // p
    n2 = 2 * n
    iq = jnp.arange(q, dtype=jnp.int32)
    eye = jnp.eye(r, dtype=F32)
    alt = lambda v: (1 - 2 * (v & 1)).astype(F32)

    ang = _phase(iq[:, None] * iq[None, :], 2 * q)
    a_re = jnp.cos(ang)
    a_im = (-jnp.sin(ang)).at[0].set(alt(iq))
    core = jnp.stack([a_re, a_im], axis=1)
    m1 = jnp.einsum('fks,rt->fkrst', core, eye).reshape(q * 2 * r, q * r)

    b_c = jnp.cos(ang.T)
    b_s = (-jnp.sin(ang.T)).at[:, 0].set(alt(iq))
    core = jnp.stack([b_c, b_s], axis=2)
    i2 = jnp.einsum('tfk,rs->trfks', core, eye).reshape(q * r, q * 2 * r)

    h = p // 2
    f2 = jnp.arange(h, dtype=jnp.int32)
    s2 = jnp.arange(p, dtype=jnp.int32)
    f_lo = iq[:, None] + 2 * q * f2[None, :]
    f_hi = jnp.where(iq[:, None] == 0, q, 2 * q - iq[:, None]) + 2 * q * f2[None, :]
    freq = jnp.stack([f_lo, f_hi], axis=1)
    phi = _phase(freq[..., None] * s2, n2)
    c, s = jnp.cos(phi), jnp.sin(phi)
    zero = jnp.zeros_like(c[:, 0])
    first = (iq == 0)[:, None, None]
    dcrow = (first & (f2 == 0)[None, :, None])
    nyq = jnp.broadcast_to(alt(s2), c[:, 0].shape)
    on_gr = jnp.stack([jnp.where(first, c[:, 0], c[:, 0]),
                       jnp.where(dcrow, nyq, -s[:, 0]),
                       jnp.where(first, zero, c[:, 1]),
                       jnp.where(first, zero, -s[:, 1])], axis=1)
    on_gi = jnp.stack([jnp.where(first, zero, s[:, 0]),
                       jnp.where(first, zero, c[:, 0]),
                       jnp.where(first, c[:, 1], -s[:, 1]),
                       jnp.where(first, -s[:, 1], -c[:, 1])], axis=1)
    m2 = jnp.concatenate([on_gr, on_gi], axis=-1).reshape(q, 4 * h, 2 * p)
    ct, st = jnp.swapaxes(c, 2, 3), jnp.swapaxes(s, 2, 3)
    zt = jnp.zeros_like(ct[:, 0])
    dccol = (first & (f2 == 0)[None, None, :])
    nyq_t = jnp.broadcast_to(alt(s2)[:, None], ct[:, 0].shape)
    hc = jnp.stack([ct[:, 0], jnp.where(dccol, nyq_t, -st[:, 0]),
                    jnp.where(first, zt, ct[:, 1]), jnp.where(first, zt, -st[:, 1])], axis=2)
    hs = jnp.stack([jnp.where(first, zt, st[:, 0]), jnp.where(first, zt, ct[:, 0]),
                    jnp.where(first, ct[:, 1], -st[:, 1]), jnp.where(first, -st[:, 1], -ct[:, 1])], axis=2)
    i1 = jnp.stack([hc, hs], axis=1).reshape(q, 2 * p, 4 * h)
    return tuple(m.astype(BF16) for m in (m1, m2, i1, i2))


def _fft_stage_a(u_ref, m1_ref, g_ref):
    q, groups, r, tc = u_ref.shape
    m1 = m1_ref[...]
    for g in range(groups):
        blk = u_ref[:, g].reshape(q * r, tc).astype(BF16)
        g_ref[g] = jnp.dot(m1, blk, preferred_element_type=F32).reshape(q, 2, r, tc)


def _fft_stage_b(g_ref, f1, m2):
    groups, _, _, r, tc = g_ref.shape
    z = jnp.concatenate([g_ref[:, f1, 0].reshape(groups * r, tc),
                         g_ref[:, f1, 1].reshape(groups * r, tc)], axis=0).astype(BF16)
    return jnp.dot(m2, z, preferred_element_type=F32)


def _cmul_packed(x, k, first):
    h = x.shape[0] // 4
    xr = (x[0:h], x[2 * h:3 * h])
    xi = (x[h:2 * h], x[3 * h:4 * h])
    kr = (k[0:h], k[2 * h:3 * h])
    ki = (k[h:2 * h], k[3 * h:4 * h])
    real_pair = first & (lax.broadcasted_iota(jnp.int32, (h, 1), 0) == 0)
    lo_re = jnp.where(real_pair, xr[0] * kr[0], xr[0] * kr[0] - xi[0] * ki[0])
    lo_im = jnp.where(real_pair, xi[0] * ki[0], xr[0] * ki[0] + xi[0] * kr[0])
    return jnp.concatenate([lo_re, lo_im, xr[1] * kr[1] - xi[1] * ki[1], xr[1] * ki[1] + xi[1] * kr[1]], axis=0)


def _filter_spec_kernel(n, hf_ref, hb_ref, nf_ref, nb_ref, m1_ref, m2_ref, k_ref, gf_ref, gb_ref):
    qi = pl.program_id(1)
    qb = m2_ref.shape[0]

    @pl.when(qi == 0)
    def _():
        _fft_stage_a(hf_ref, m1_ref, gf_ref)
        _fft_stage_a(hb_ref, m1_ref, gb_ref)

    inv_norm = 1.0 / (nf_ref[...] + nb_ref[...])
    h = m2_ref.shape[1] // 4
    row = lax.broadcasted_iota(jnp.int32, (4 * h, 1), 0)
    for k in range(qb):
        f1 = qi * qb + k
        xf = _fft_stage_b(gf_ref, f1, m2_ref[k])
        xb = _fft_stage_b(gb_ref, f1, m2_ref[k])
        real_pair = (f1 == 0) & ((row == 0) | (row == h))
        imag_row = ((row >= h) & (row < 2 * h)) | (row >= 3 * h)
        spec = jnp.where(imag_row & jnp.logical_not(real_pair), xf - xb, xf + xb)
        spec = spec * jnp.where(real_pair, 0.5 / n, 1.0 / n) * inv_norm
        k_ref[k] = spec.reshape(k_ref.shape[1:]).astype(k_ref.dtype)


def _time_view(v, n):
    return v.reshape(v.shape[:-2] + (n // FFT_P, FFT_P // FFT_R, FFT_R, v.shape[-1]))


def _filter_spectrum(taps, norm, mats, c):
    n = taps.shape[0]
    m1, m2, _, _ = mats
    q = n // FFT_P
    h = FFT_P // 2
    tc = _tile(c, 256)
    nc = c // tc
    qb = _tile(q, 8)
    tv = _time_view(taps, n)
    blk = (q, FFT_P // FFT_R, FFT_R, tc)
    return pl.pallas_call(
        functools.partial(_filter_spec_kernel, n),
        grid=(nc, q // qb),
        in_specs=[pl.BlockSpec(blk, lambda j, i: (0, 0, 0, j)),
                  pl.BlockSpec(blk, lambda j, i: (0, 0, 0, j + nc)),
                  pl.BlockSpec((1, tc), lambda j, i: (0, j)),
                  pl.BlockSpec((1, tc), lambda j, i: (0, j + nc)),
                  pl.BlockSpec(m1.shape, lambda j, i: (0, 0)),
                  pl.BlockSpec((qb,) + m2.shape[1:], lambda j, i: (i, 0, 0))],
        out_specs=pl.BlockSpec((qb, 4, h, tc), lambda j, i: (i, 0, 0, j)),
        out_shape=jax.ShapeDtypeStruct((q, 4, h, c), BF16),
        scratch_shapes=[pltpu.VMEM((FFT_P // FFT_R, q, 2, FFT_R, tc), F32) for _ in range(2)],
        compiler_params=_params(("parallel", "arbitrary")),
        name="filter_spectrum",
    )(tv, tv, norm, norm, m1, m2)


def _conv_fwd_kernel(w_ref, m1_ref, m2_ref, k_ref, y_ref, g_ref):
    qi = pl.program_id(2)
    qb = m2_ref.shape[0]

    @pl.when(qi == 0)
    def _():
        _fft_stage_a(w_ref.at[0], m1_ref, g_ref)

    for k in range(qb):
        f1 = qi * qb + k
        x = _fft_stage_b(g_ref, f1, m2_ref[k])
        y = _cmul_packed(x, k_ref[k].reshape(x.shape).astype(F32), f1 == 0)
        y_ref[0, k] = y.reshape(y_ref.shape[2:]).astype(y_ref.dtype)


def _conv_forward(wf, kspec, mats):
    b, n, c = wf.shape
    m1, m2, _, _ = mats
    q = n // FFT_P
    h = FFT_P // 2
    tc = _tile(c, 512)
    qb = _tile(q, 8)
    return pl.pallas_call(
        _conv_fwd_kernel,
        grid=(c // tc, b, q // qb),
        in_specs=[pl.BlockSpec((1, q, FFT_P // FFT_R, FFT_R, tc), lambda j, bi, i: (bi, 0, 0, 0, j)),
                  pl.BlockSpec(m1.shape, lambda j, bi, i: (0, 0)),
                  pl.BlockSpec((qb,) + m2.shape[1:], lambda j, bi, i: (i, 0, 0)),
                  pl.BlockSpec((qb, 4, h, tc), lambda j, bi, i: (i, 0, 0, j))],
        out_specs=pl.BlockSpec((1, qb, 4, h, tc), lambda j, bi, i: (bi, i, 0, 0, j)),
        out_shape=jax.ShapeDtypeStruct((b, q, 4, h, c), BF16),
        scratch_shapes=[pltpu.VMEM((FFT_P // FFT_R, q, 2, FFT_R, tc), F32)],
        compiler_params=_params(("parallel", "parallel", "arbitrary")),
        name="conv_forward_fft",
    )(_time_view(wf, n), m1, m2, kspec)


def _conv_inv_kernel(y_ref, i1_ref, i2_ref, wf_ref, x0_ref, bias_ref, o_ref, h_ref):
    qi = pl.program_id(2)
    qb = i1_ref.shape[0]
    groups, q, _, r, tc = h_ref.shape
    for k in range(qb):
        hv = jnp.dot(i1_ref[k], y_ref[0, k].reshape(i1_ref.shape[2], tc), preferred_element_type=F32)
        half = hv.shape[0] // 2
        h_ref[:, qi * qb + k, 0] = hv[:half].reshape(groups, r, tc)
        h_ref[:, qi * qb + k, 1] = hv[half:].reshape(groups, r, tc)

    @pl.when(qi == pl.num_programs(2) - 1)
    def _():
        i2 = i2_ref[...]
        bias = bias_ref[...]
        for g2 in range(groups // 2):
            parts = []
            for g in (2 * g2, 2 * g2 + 1):
                z = jnp.dot(i2, h_ref[g].reshape(q * 2 * r, tc).astype(BF16), preferred_element_type=F32)
                z = z.reshape(q, r, tc)
                parts.append(x0_ref[0, :, g] * (z + wf_ref[0, :, g] * bias))
            o_ref[0, :, g2] = jnp.concatenate(parts, axis=1).astype(o_ref.dtype)


def _conv_inverse(yspec, wf, x0, bias, mats):
    b, n, c = wf.shape
    _, _, i1, i2 = mats
    q = n // FFT_P
    h = FFT_P // 2
    groups = FFT_P // FFT_R
    tc = _tile(c, 256)
    qb = _tile(q, 8)
    tspec = pl.BlockSpec((1, q, groups, FFT_R, tc), lambda j, bi, i: (bi, 0, 0, 0, j))
    out = pl.pallas_call(
        _conv_inv_kernel,
        grid=(c // tc, b, q // qb),
        in_specs=[pl.BlockSpec((1, qb, 4, h, tc), lambda j, bi, i: (bi, i, 0, 0, j)),
                  pl.BlockSpec((qb,) + i1.shape[1:], lambda j, bi, i: (i, 0, 0)),
                  pl.BlockSpec(i2.shape, lambda j, bi, i: (0, 0)),
                  tspec, tspec,
                  pl.BlockSpec((1, tc), lambda j, bi, i: (0, j))],
        out_specs=pl.BlockSpec((1, q, groups // 2, 2 * FFT_R, tc), lambda j, bi, i: (bi, 0, 0, 0, j)),
        out_shape=jax.ShapeDtypeStruct((b, q, groups // 2, 2 * FFT_R, c), BF16),
        scratch_shapes=[pltpu.VMEM((groups, q, 2, FFT_R, tc), F32)],
        compiler_params=_params(("parallel", "parallel", "arbitrary")),
        name="conv_inverse_fft",
    )(yspec, i1, i2, _time_view(wf, n), _time_view(x0, n), bias.reshape(1, c))
    return out.reshape(b, n, c)


SCAN_UNROLL = 4


def _tile_scan(a, b, reverse):
    rows, c = a.shape
    a3 = a.reshape(rows // SUBLANES, SUBLANES, c)
    b3 = b.reshape(rows // SUBLANES, SUBLANES, c)
    sub = lax.broadcasted_iota(jnp.int32, (1, SUBLANES, 1), 1)
    for k in (1, 2, 4):
        shift = SUBLANES - k if reverse else k
        valid = (sub < SUBLANES - k) if reverse else (sub >= k)
        b3 = b3 + a3 * jnp.where(valid, pltpu.roll(b3, shift, axis=1), 0.0)
        a3 = a3 * jnp.where(valid, pltpu.roll(a3, shift, axis=1), 1.0)
    return a3.reshape(rows, c), b3.reshape(rows, c)


def _lru_kernel(group, has_gate, *refs):
    if has_gate:
        (x_ref, gate_ref, cw_ref, cb_ref, wa_ref, ba_ref, wx_ref, bx_ref, lam_ref, h0_ref,
         y_ref, ht_ref, af_ref, bf_ref, ab_ref, bb_ref) = refs
    else:
        (x_ref, cw_ref, cb_ref, wa_ref, ba_ref, wx_ref, bx_ref, lam_ref, h0_ref,
         ht_ref, af_ref, bf_ref, ab_ref, bb_ref) = refs
    t_len = x_ref.shape[1]
    chunk = _tile(t_len, 512)
    a_refs = (af_ref, ab_ref)
    b_refs = (bf_ref, bb_ref)

    def coeffs(ci, carry):
        r0 = pl.multiple_of(ci * chunk, chunk)
        xc = _short_conv(x_ref[0, pl.ds(r0, chunk), :], cw_ref, cb_ref[...], LRU_CONV_LEFT, group)
        xcb = xc.astype(BF16)
        half_xc = 0.5 * xc
        for d in range(2):
            ta = jnp.tanh(jnp.dot(xcb, wa_ref[d, 0], preferred_element_type=F32) + ba_ref[d])
            ti = jnp.tanh(jnp.dot(xcb, wx_ref[d, 0], preferred_element_type=F32) + bx_ref[d])
            lam = lam_ref[d]
            softplus_neg = jnp.maximum(-lam, 0.0) + jnp.log(1.0 + jnp.exp(-jnp.abs(lam)))
            c1 = (-0.5 * LRU_C) * softplus_neg
            a = jnp.exp(c1 * ta + c1)
            a_tile, b_tile = _tile_scan(a, jnp.sqrt(1.0 - a * a) * (half_xc * ti + half_xc), d == 1)
            a_refs[d][pl.ds(r0, chunk), :] = a_tile
            b_refs[d][pl.ds(r0, chunk), :] = b_tile
        return carry

    lax.fori_loop(0, t_len // chunk, coeffs, 0)

    n_tiles = t_len // SUBLANES
    unroll = _tile(n_tiles, SCAN_UNROLL)

    def scan(i, carry):
        hf, hb = carry
        for u in range(unroll):
            kf = i * unroll + u
            rf = pl.multiple_of(kf * SUBLANES, SUBLANES)
            tile_f = af_ref[pl.ds(rf, SUBLANES), :] * hf + bf_ref[pl.ds(rf, SUBLANES), :]
            bf_ref[pl.ds(rf, SUBLANES), :] = tile_f
            hf = tile_f[SUBLANES - 1:SUBLANES, :]
            rb = pl.multiple_of((n_tiles - 1 - kf) * SUBLANES, SUBLANES)
            tile_b = ab_ref[pl.ds(rb, SUBLANES), :] * hb + bb_ref[pl.ds(rb, SUBLANES), :]
            bb_ref[pl.ds(rb, SUBLANES), :] = tile_b
            hb = tile_b[0:1, :]
        return hf, hb

    hf, hb = lax.fori_loop(0, n_tiles // unroll, scan, (h0_ref[0, 0:1, :], h0_ref[0, 1:2, :]))
    ht_ref[0, 0:1, :] = hf
    ht_ref[0, 1:2, :] = hb

    if has_gate:
        def emit(ci, carry):
            r0 = pl.multiple_of(ci * chunk, chunk)
            hs = bf_ref[pl.ds(r0, chunk), :] + bb_ref[pl.ds(r0, chunk), :]
            y_ref[0, pl.ds(r0, chunk), :] = (jax.nn.gelu(gate_ref[0, pl.ds(r0, chunk), :]) * hs).astype(y_ref.dtype)
            return carry

        lax.fori_loop(0, t_len // chunk, emit, 0)


def _block_diag(w, heads_per_tile):
    d2, h, hd, _ = w.shape
    w = w.reshape(d2, h // heads_per_tile, heads_per_tile, hd, hd)
    eye = jnp.eye(heads_per_tile, dtype=w.dtype)
    bd = jnp.einsum('dghij,hq->dghiqj', w, eye)
    return bd.reshape(d2, h // heads_per_tile, heads_per_tile * hd, heads_per_tile * hd).astype(BF16)


def _rglru(xsrc, x_col, gsrc, g_col, conv_w, conv_b, wa, ba, wx, bx, lam, h0, group):
    b, t, _ = xsrc.shape
    c = conv_w.shape[1]
    hd = c // LRU_HEADS
    tc = min(c, max(hd, 256))
    hp = tc // hd
    nt = c // tc
    has_gate = gsrc is not None
    wa_bd = _block_diag(0.5 * wa, hp)
    wx_bd = _block_diag(0.5 * wx, hp)
    ba, bx = 0.5 * ba, 0.5 * bx
    xo, go = x_col // tc, (g_col // tc if has_gate else 0)
    vec = lambda: pl.BlockSpec((2, 1, tc), lambda i, j: (0, 0, j))
    mat = lambda: pl.BlockSpec((2, 1, tc, tc), lambda i, j: (0, j, 0, 0))
    in_specs = [pl.BlockSpec((1, t, tc), lambda i, j: (i, 0, j + xo))]
    args = [xsrc]
    if has_gate:
        in_specs.append(pl.BlockSpec((1, t, tc), lambda i, j: (i, 0, j + go)))
        args.append(gsrc)
    in_specs += [pl.BlockSpec((conv_w.shape[0], tc), lambda i, j: (0, j)),
                 pl.BlockSpec((1, tc), lambda i, j: (0, j)),
                 mat(), vec(), mat(), vec(), vec(),
                 pl.BlockSpec((1, 2, tc), lambda i, j: (i, 0, j))]
    args += [conv_w, conv_b.reshape(1, c), wa_bd, ba.reshape(2, 1, c), wx_bd, bx.reshape(2, 1, c),
             lam.reshape(2, 1, c), h0]
    ht_spec = pl.BlockSpec((1, 2, tc), lambda i, j: (i, 0, j))
    ht_shape = jax.ShapeDtypeStruct((b, 2, c), F32)
    if has_gate:
        out_specs = [pl.BlockSpec((1, t, tc), lambda i, j: (i, 0, j)), ht_spec]
        out_shape = [jax.ShapeDtypeStruct((b, t, c), BF16), ht_shape]
    else:
        out_specs = [ht_spec]
        out_shape = [ht_shape]
    return pl.pallas_call(
        functools.partial(_lru_kernel, group, has_gate),
        grid=(b, nt),
        in_specs=in_specs,
        out_specs=out_specs,
        out_shape=out_shape,
        scratch_shapes=[pltpu.VMEM((t, tc), F32) for _ in range(4)],
        compiler_params=_params(("parallel", "parallel")),
        name="rglru" if has_gate else "rglru_context",
    )(*args)


def _out_proj_kernel(a1_ref, a2_ref, w1_ref, w2_ref, b_ref, x_ref, g_ref, o_ref):
    acc = jnp.dot(a1_ref[...], w1_ref[...], preferred_element_type=F32)
    acc = acc + jnp.dot(a2_ref[...], w2_ref[...], preferred_element_type=F32)
    o_ref[...] = x_ref[...] + g_ref[0] * (acc + b_ref[...])


def _out_projection(a1, a2, w, bias, x, gate, t):
    m, k1 = a1.shape
    n = w.shape[1]
    tm = _tile(t, 1024)
    tn = _tile(n, 512)
    per_b = t // tm
    k1_blocks = 1
    return pl.pallas_call(
        _out_proj_kernel,
        grid=(m // tm, n // tn),
        in_specs=[pl.BlockSpec((tm, k1), lambda i, j: (i, 0)),
                  pl.BlockSpec((tm, k1), lambda i, j: (i, 0)),
                  pl.BlockSpec((k1, tn), lambda i, j: (0, j)),
                  pl.BlockSpec((k1, tn), lambda i, j: (k1_blocks, j)),
                  pl.BlockSpec((1, tn), lambda i, j: (0, j)),
                  pl.BlockSpec((tm, tn), lambda i, j: (i, j)),
                  pl.BlockSpec((1, 1, tn), lambda i, j: (i // per_b, 0, j))],
        out_specs=pl.BlockSpec((tm, tn), lambda i, j: (i, j)),
        out_shape=jax.ShapeDtypeStruct((m, n), F32),
        compiler_params=_params(("parallel", "arbitrary")),
        name="out_projection",
    )(a1, a2, w, w, bias.reshape(1, n), x, gate)


def _router_kernel(x_ref, g_ref, sh_ref, sc_ref, wr_ref, h_ref, aff_ref):
    h = _rms_mod(x_ref[0], g_ref[...], sh_ref[0], sc_ref[0])
    half = h.shape[1] // 2
    h_ref[...] = pltpu.pack_elementwise([h[:, :half], h[:, half:]], packed_dtype=BF16)
    h_hi, h_lo = _split_bf16(h)
    w_hi, w_lo = _split_bf16(wr_ref[...])
    nt = (((1,), (1,)), ((), ()))
    logits = (lax.dot_general(w_hi, h_hi, nt, preferred_element_type=F32)
              + lax.dot_general(w_lo, h_hi, nt, preferred_element_type=F32)
              + lax.dot_general(w_hi, h_lo, nt, preferred_element_type=F32))
    z = jnp.exp(logits - jnp.max(logits, axis=0, keepdims=True))
    aff_ref[0] = z / jnp.sum(z, axis=0, keepdims=True)


def _router(x1, g, shift, scale, w_router):
    b, t, d = x1.shape
    e = w_router.shape[1]
    tt = _tile(t, 512)
    per_b = t // tt
    mod_map = lambda i, j: (i, 0, 0)
    return pl.pallas_call(
        _router_kernel,
        grid=(b, per_b),
        in_specs=[pl.BlockSpec((1, tt, d), lambda i, j: (i, j, 0)),
                  pl.BlockSpec((1, d), lambda i, j: (0, 0)),
                  pl.BlockSpec((1, 1, d), mod_map),
                  pl.BlockSpec((1, 1, d), mod_map),
                  pl.BlockSpec((e, d), lambda i, j: (0, 0))],
        out_specs=[pl.BlockSpec((tt, d // 2), lambda i, j: (i * per_b + j, 0)),
                   pl.BlockSpec((1, e, tt), lambda i, j: (i, 0, j))],
        out_shape=[jax.ShapeDtypeStruct((b * t, d // 2), jnp.uint32),
                   jax.ShapeDtypeStruct((b, e, t), F32)],
        compiler_params=_params(("parallel", "parallel")),
        name="router",
    )(x1, g.reshape(1, d), shift, scale, w_router.T)


def _prefix_count(mask_ref, out_ref):
    e, t = mask_ref.shape
    blk = min(t, LANES)
    tri = (lax.broadcasted_iota(jnp.int32, (blk, blk), 0)
           < lax.broadcasted_iota(jnp.int32, (blk, blk), 1)).astype(BF16)
    carry = jnp.zeros((e, 1), F32)
    for k in range(t // blk):
        m = mask_ref[:, k * blk:(k + 1) * blk]
        out_ref[:, k * blk:(k + 1) * blk] = jnp.dot(m.astype(BF16), tri, preferred_element_type=F32) + carry
        carry = carry + jnp.sum(m, axis=1, keepdims=True)
    return carry


def _topk_kernel(cap, aff_ref, idx_ref, gate_ref, dest_ref, span_ref, mask_ref, pos_ref, cnt_ref, q_ref):
    a = aff_ref[0]
    e, t = a.shape
    min_normal = 0x00800000

    def refine(i, thr):
        cand = thr | jnp.left_shift(jnp.int32(1), 30 - i)
        cnt = jnp.sum(jnp.where(a >= pltpu.bitcast(cand, F32), 1.0, 0.0), axis=1, keepdims=True)
        return jnp.where((cnt >= cap) & (cand >= min_normal), cand, thr)

    thr = lax.fori_loop(0, 31, refine, jnp.zeros((e, 1), jnp.int32))
    above = a >= pltpu.bitcast(jnp.maximum(thr + 1, min_normal), F32)
    tied = (a >= pltpu.bitcast(thr, F32)) & jnp.logical_not(above)
    need = cap - jnp.sum(jnp.where(above, 1.0, 0.0), axis=1, keepdims=True)
    mask_ref[...] = jnp.where(tied, 1.0, 0.0)
    _prefix_count(mask_ref, pos_ref)
    sel = above | (tied & (pos_ref[...] < need))
    mask_ref[...] = jnp.where(sel, 1.0, 0.0)
    _prefix_count(mask_ref, pos_ref)

    mask = mask_ref[...]
    cnt = jnp.sum(mask, axis=0, keepdims=True)
    cnt_ref[...] = jnp.broadcast_to(cnt, cnt_ref.shape)
    _prefix_count(cnt_ref, q_ref.at[0:SUBLANES])
    first = q_ref[0:1, :]
    srow = lax.broadcasted_iota(jnp.int32, (SUBLANES, 1), 0)
    span_ref[0] = jnp.where(srow == 0, first, jnp.where(srow == 1, first + cnt, 0.0))
    lower = (lax.broadcasted_iota(jnp.int32, (e, e), 1)
             < lax.broadcasted_iota(jnp.int32, (e, e), 0)).astype(BF16)
    q_ref[...] = first + jnp.dot(lower, mask.astype(BF16), preferred_element_type=F32)

    tok = lax.broadcasted_iota(jnp.int32, (1, t), 1)
    tok_hi = (tok >> 6).astype(F32)
    tok_lo = (tok & 63).astype(F32)
    slot = lax.broadcasted_iota(jnp.int32, (cap, 1), 0).astype(F32)
    vrow = lax.broadcasted_iota(jnp.int32, (SUBLANES, 1), 0)
    dest_base = pl.program_id(0) * (e * cap)

    def compact(ei, carry):
        g = aff_ref[0, pl.ds(ei, 1), :]
        g_hi = g.astype(BF16).astype(F32)
        g_mid = (g - g_hi).astype(BF16).astype(F32)
        g_lo = g - g_hi - g_mid
        q = q_ref[pl.ds(ei, 1), :]
        q_hi = jnp.floor(q * (1.0 / 64.0))
        q_lo = q - 64.0 * q_hi
        vals = jnp.where(vrow == 0, tok_hi, jnp.where(vrow == 1, tok_lo, jnp.where(
            vrow == 2, g_hi, jnp.where(vrow == 3, g_mid, jnp.where(vrow == 4, g_lo, jnp.where(
                vrow == 5, q_hi, jnp.where(vrow == 6, q_lo, 0.0))))))).astype(BF16)
        hit = (pos_ref[pl.ds(ei, 1), :] == slot) & (mask_ref[pl.ds(ei, 1), :] > 0.0)
        onehot = jnp.where(hit, 1.0, 0.0).astype(BF16)
        res = lax.dot_general(vals, onehot, (((1,), (1,)), ((), ())), preferred_element_type=F32)
        idx_ref[0, pl.ds(ei, 1), :] = (res[0:1] * 64.0 + res[1:2]).astype(jnp.int32)
        gate_ref[0, pl.ds(ei, 1), :] = res[2:3] + res[3:4] + res[4:5]
        dest_ref[0, pl.ds(ei, 1), :] = (res[5:6] * 64.0 + res[6:7]).astype(jnp.int32) + dest_base
        return carry

    lax.fori_loop(0, e, compact, 0)


def _select_tokens(aff, cap):
    b, e, t = aff.shape
    assert t <= 64 * 256 and e * cap <= 64 * 256 and e >= SUBLANES
    spec = pl.BlockSpec((1, e, cap), lambda i: (i, 0, 0))
    return pl.pallas_call(
        functools.partial(_topk_kernel, cap),
        grid=(b,),
        in_specs=[pl.BlockSpec((1, e, t), lambda i: (i, 0, 0))],
        out_specs=[spec, spec, spec, pl.BlockSpec((1, SUBLANES, t), lambda i: (i, 0, 0))],
        out_shape=[jax.ShapeDtypeStruct((b, e, cap), jnp.int32), jax.ShapeDtypeStruct((b, e, cap), F32),
                   jax.ShapeDtypeStruct((b, e, cap), jnp.int32), jax.ShapeDtypeStruct((b, SUBLANES, t), F32)],
        scratch_shapes=[pltpu.VMEM((e, t), F32), pltpu.VMEM((e, t), F32),
                        pltpu.VMEM((SUBLANES, t), F32), pltpu.VMEM((e, t), F32)],
        compiler_params=_params(("parallel",)),
        name="select_tokens",
    )(aff)


def _gather_kernel(rows_ref, h_hbm, o_ref, buf, sem):
    i = pl.program_id(0)
    r_blk = buf.shape[1]

    def issue_block(blk, slot):
        def issue(r, carry):
            row = rows_ref[blk * r_blk + r]
            pltpu.make_async_copy(h_hbm.at[pl.ds(row, 1), :], buf.at[slot, pl.ds(r, 1), :], sem.at[slot]).start()
            return carry

        lax.fori_loop(0, r_blk, issue, 0, unroll=8)

    @pl.when(i == 0)
    def _():
        issue_block(0, 0)

    @pl.when(i + 1 < pl.num_programs(0))
    def _():
        issue_block(i + 1, (i + 1) & 1)

    slot = i & 1
    pltpu.make_async_copy(h_hbm.at[pl.ds(0, r_blk), :], buf.at[slot], sem.at[slot]).wait()
    words = buf[slot]
    half = words.shape[1]
    for h in range(2):
        part = pltpu.unpack_elementwise(words, index=h, packed_dtype=BF16, unpacked_dtype=F32)
        o_ref[:, h * half:(h + 1) * half] = part.astype(o_ref.dtype)


def _gather_rows(rows, h):
    n_rows = rows.shape[0]
    words = h.shape[1]
    d = 2 * words
    r_blk = _tile(n_rows, 512)
    grid_spec = pltpu.PrefetchScalarGridSpec(
        num_scalar_prefetch=1,
        grid=(n_rows // r_blk,),
        in_specs=[pl.BlockSpec(memory_space=pl.ANY)],
        out_specs=pl.BlockSpec((r_blk, d), lambda i, rows_ref: (i, 0)),
        scratch_shapes=[pltpu.VMEM((2, r_blk, words), h.dtype), pltpu.SemaphoreType.DMA((2,))],
    )
    return pl.pallas_call(
        _gather_kernel,
        grid_spec=grid_spec,
        out_shape=jax.ShapeDtypeStruct((n_rows, d), BF16),
        compiler_params=_params(("arbitrary",)),
        name="gather_rows",
    )(rows, h)


def _expert_up_kernel(xs_ref, wg_ref, wu_ref, h_ref):
    xs = xs_ref[...]
    g = jnp.dot(xs, wg_ref[0].astype(BF16), preferred_element_type=F32)
    u = jnp.dot(xs, wu_ref[0].astype(BF16), preferred_element_type=F32)
    h_ref[...] = (g * jax.nn.sigmoid(g) * u).astype(h_ref.dtype)


def _expert_up(xs, w_gate, w_up, rows_per_expert):
    m, d = xs.shape
    f = w_gate.shape[2]
    tm = _tile(rows_per_expert, 1024)
    per_e = rows_per_expert // tm
    tf = _tile(f, 256)
    wspec = pl.BlockSpec((1, d, tf), lambda i, j: (i // per_e, 0, j))
    return pl.pallas_call(
        _expert_up_kernel,
        grid=(m // tm, f // tf),
        in_specs=[pl.BlockSpec((tm, d), lambda i, j: (i, 0)), wspec, wspec],
        out_specs=pl.BlockSpec((tm, tf), lambda i, j: (i, j)),
        out_shape=jax.ShapeDtypeStruct((m, f), BF16),
        compiler_params=_params(("parallel", "arbitrary")),
        name="expert_up",
    )(xs, w_gate, w_up)


DOWN_COL_TILE = 1024


def _expert_down_kernel(dest_ref, h_ref, wd_ref, g_ref, ys_hbm, ybuf, sem):
    i = pl.program_id(0)
    j = pl.program_id(1)
    n_i = pl.num_programs(0)
    n_j = pl.num_programs(1)
    _, tm, dw = ybuf.shape
    hw = wd_ref.shape[2] // 2
    slot = i & 1

    def wait_scatter(s):
        pltpu.make_async_copy(ybuf.at[s], ys_hbm.at[pl.ds(0, tm), :], sem.at[s]).wait()

    @pl.when((j == 0) & (i >= 2))
    def _():
        wait_scatter(slot)

    y = jnp.dot(h_ref[...], wd_ref[0].astype(BF16), preferred_element_type=F32) * g_ref[...]
    packed = pltpu.pack_elementwise([y[:, :hw], y[:, hw:]], packed_dtype=BF16)
    for jj in range(dw // hw):
        @pl.when(j == jj)
        def _(jj=jj):
            ybuf[slot, :, jj * hw:(jj + 1) * hw] = packed

    @pl.when(j == n_j - 1)
    def _():
        def issue(r, carry):
            dst = dest_ref[i * tm + r]
            pltpu.make_async_copy(ybuf.at[slot, pl.ds(r, 1), :], ys_hbm.at[pl.ds(dst, 1), :], sem.at[slot]).start()
            return carry

        lax.fori_loop(0, tm, issue, 0, unroll=8)

        @pl.when(i == n_i - 1)
        def _():
            @pl.when(i >= 1)
            def _():
                wait_scatter(1 - slot)
            wait_scatter(slot)


def _expert_down(dest, h, w_down, gates, rows_per_expert):
    m, f = h.shape
    d = w_down.shape[2]
    tm = _tile(rows_per_expert, 1024)
    per_e = rows_per_expert // tm
    td = _tile(d, DOWN_COL_TILE)
    grid_spec = pltpu.PrefetchScalarGridSpec(
        num_scalar_prefetch=1,
        grid=(m // tm, d // td),
        in_specs=[pl.BlockSpec((tm, f), lambda i, j, dest_ref: (i, 0)),
                  pl.BlockSpec((1, f, td), lambda i, j, dest_ref: (i // per_e, 0, j)),
                  pl.BlockSpec((tm, 1), lambda i, j, dest_ref: (i, 0))],
        out_specs=pl.BlockSpec(memory_space=pl.ANY),
        scratch_shapes=[pltpu.VMEM((2, tm, d // 2), jnp.uint32), pltpu.SemaphoreType.DMA((2,))],
    )
    return pl.pallas_call(
        _expert_down_kernel,
        grid_spec=grid_spec,
        out_shape=jax.ShapeDtypeStruct((m, d // 2), jnp.uint32),
        compiler_params=_params(("arbitrary", "arbitrary")),
        name="expert_down",
    )(dest, h, w_down, gates)


def _combine_kernel(slots_per_batch, hw, tb_ref, span_ref, x_ref, gt_ref, g_ref, ys_hbm, o_ref,
                    buf, sem, acc_ref, parts_ref):
    bi = pl.program_id(0)
    i = pl.program_id(1)
    n_t = pl.num_programs(1)
    kc = buf.shape[1]
    total = ys_hbm.shape[0]
    tt = acc_ref.shape[1]
    n_sub = tt // COMBINE_SUB
    tb0 = bi * (n_t * n_sub + 1) + i * n_sub
    c_lo = tb_ref[tb0]
    c_hi = tb_ref[tb0 + n_sub]
    k_first = (c_lo >> 3) << 3
    n_chunks = (c_hi - k_first + kc - 1) // kc

    def chunk_start(c):
        return pl.multiple_of(jnp.minimum(k_first + c * kc, total - kc), SUBLANES)

    n_buf = buf.shape[0]

    def copy(c):
        slot = c & (n_buf - 1)
        return pltpu.make_async_copy(ys_hbm.at[pl.ds(chunk_start(c), kc), :], buf.at[slot], sem.at[slot])

    for ahead in range(n_buf - 1):
        @pl.when(ahead < n_chunks)
        def _(ahead=ahead):
            copy(ahead).start()

    base = (bi * slots_per_batch).astype(F32)
    acc_ref[...] = jnp.zeros_like(acc_ref)

    def body(c, carry):
        slot = c & (n_buf - 1)
        copy(c).wait()

        @pl.when(c + n_buf - 1 < n_chunks)
        def _():
            copy(c + n_buf - 1).start()

        k_lo = k_first + c * kc
        k_hi = chunk_start(c) + kc
        k = chunk_start(c) + lax.broadcasted_iota(jnp.int32, (1, kc), 1)
        k = jnp.where(k >= k_lo, k, -1).astype(F32)
        words = buf[slot]
        for half in range(2):
            part = pltpu.unpack_elementwise(words, index=half, packed_dtype=BF16, unpacked_dtype=F32)
            parts_ref[half] = part.astype(BF16)
        for s in range(n_sub):
            rows = slice(s * COMBINE_SUB, (s + 1) * COMBINE_SUB)

            @pl.when((tb_ref[tb0 + s] < k_hi) & (tb_ref[tb0 + s + 1] > k_lo))
            def _(rows=rows):
                first = span_ref[0, rows, 0:1] + base
                last = span_ref[0, rows, 1:2] + base
                onehot = jnp.where((k >= first) & (k < last), 1.0, 0.0).astype(BF16)
                for half in range(2):
                    acc_ref[half, rows] += jnp.dot(onehot, parts_ref[half], preferred_element_type=F32)
        return carry

    lax.fori_loop(0, n_chunks, body, 0)
    gate = gt_ref[0]
    g_final = g_ref[...]

    def finish(r, carry):
        rows = pl.ds(pl.multiple_of(r * FINISH_ROWS, FINISH_ROWS), FINISH_ROWS)
        lo, hi = acc_ref[0, rows, :], acc_ref[1, rows, :]
        moe = jnp.concatenate([part[:, j * hw:(j + 1) * hw] for j in range(lo.shape[1] // hw)
                               for part in (lo, hi)], axis=1)
        x = x_ref[0, rows, :] + gate * moe
        o_ref[0, rows, :] = x * lax.rsqrt(jnp.mean(x * x, axis=-1, keepdims=True) + EPS) * g_final
        return carry

    lax.fori_loop(0, tt // FINISH_ROWS, finish, 0)


COMBINE_ROWS = 512
COMBINE_SUB = 128
COMBINE_CHUNK = 256
COMBINE_RING = 4
FINISH_ROWS = 64


def _combine(tile_bounds, span, ys, x1, gate, g_final, slots_per_batch):
    b, t, d = x1.shape
    tt = _tile(t, COMBINE_ROWS)
    assert tt % COMBINE_SUB == 0
    kc = min(COMBINE_CHUNK, ys.shape[0])
    grid_spec = pltpu.PrefetchScalarGridSpec(
        num_scalar_prefetch=1,
        grid=(b, t // tt),
        in_specs=[pl.BlockSpec((1, tt, SUBLANES), lambda bi, i, tb: (bi, i, 0)),
                  pl.BlockSpec((1, tt, d), lambda bi, i, tb: (bi, i, 0)),
                  pl.BlockSpec((1, 1, d), lambda bi, i, tb: (bi, 0, 0)),
                  pl.BlockSpec((1, d), lambda bi, i, tb: (0, 0)),
                  pl.BlockSpec(memory_space=pl.ANY)],
        out_specs=pl.BlockSpec((1, tt, d), lambda bi, i, tb: (bi, i, 0)),
        scratch_shapes=[pltpu.VMEM((COMBINE_RING, kc, d // 2), jnp.uint32),
                        pltpu.SemaphoreType.DMA((COMBINE_RING,)),
                        pltpu.VMEM((2, tt, d // 2), F32), pltpu.VMEM((2, kc, d // 2), BF16)],
    )
    return pl.pallas_call(
        functools.partial(_combine_kernel, slots_per_batch, _tile(d, DOWN_COL_TILE) // 2),
        grid_spec=grid_spec,
        out_shape=jax.ShapeDtypeStruct((b, t, d), F32),
        compiler_params=_params(("arbitrary", "arbitrary")),
        name="combine",
    )(tile_bounds, span, x1, gate, g_final.reshape(1, d), ys)


def kernel(x, c, ctx, c_ctx, w_mod, b_mod, g_mix, g_ffn, w_in, b_in, hy_conv_w, hy_conv_b, hy_f_w1, hy_f_b1, hy_f_w2, hy_f_b2, hy_f_w3, hy_f_b3, hy_f_wout, hy_f_freq, hy_bias, lru_conv_w, lru_conv_b, lru_wa, lru_ba, lru_wx, lru_bx, lru_lambda, w_out, b_out, w_router, w_exp_gate, w_exp_up, w_exp_down, g_final):
    bsz, n_lat, d = x.shape
    n_ctx = ctx.shape[1]
    depth = w_mod.shape[0]
    c_hy = hy_bias.shape[1]
    c_lru = lru_conv_b.shape[1]
    in_gate = 3 * c_hy
    in_x = in_gate + c_lru
    n_exp = w_router.shape[2]
    cap = CAPACITY_FACTOR * n_lat // n_exp
    assert depth == 1, "context residual updates are only needed for depth > 1"
    assert n_lat % GRID_W == 0 and n_ctx & (n_ctx - 1) == 0

    rows = -(-(bsz + 1) // SUBLANES) * SUBLANES
    c_all = jnp.zeros((rows, d), F32).at[:bsz].set(c).at[bsz].set(c_ctx)
    mats = _fft_matrices(n_lat)

    l = 0
    mod = _modulation(c_all, w_mod[l], b_mod[l])
    mx = mod[:bsz].reshape(bsz, 1, N_MOD, d)
    sh1, sc1, gt1, sh2, sc2, gt2 = (mx[:, :, k] for k in range(N_MOD))
    mc = mod[bsz].reshape(1, 1, N_MOD, d)
    csh1, csc1 = mc[:, :, 0], mc[:, :, 1]

    w_in_b = w_in[l].astype(BF16)
    w_out_b = w_out[l].astype(BF16)

    hx = _norm_mod(x, g_mix[l], sh1, sc1, True)
    hc = _norm_mod(ctx, g_mix[l], csh1, csc1, False)
    hx = hx.reshape(bsz * n_lat, d)
    x0, wf = _projection_hyena(hx, w_in_b, b_in[l], hy_conv_w[l], hy_conv_b[l], c_hy)
    x0, wf = x0.reshape(bsz, n_lat, c_hy), wf.reshape(bsz, n_lat, c_hy)
    p_lru = _projection(hx, w_in_b, b_in[l], in_gate, 2 * c_lru, F32)
    p_lru = p_lru.reshape(bsz, n_lat, 2 * c_lru)
    pc_lx = _projection(hc.reshape(bsz * n_ctx, d), w_in_b, b_in[l], in_x, c_lru, F32)
    pc_lx = pc_lx.reshape(bsz, n_ctx, c_lru)

    lru_args = (lru_conv_w[l], lru_conv_b[l], lru_wa[l], lru_ba[l], lru_wx[l], lru_bx[l], lru_lambda[l])
    (h_ctx,) = _rglru(pc_lx, 0, None, 0, *lru_args, jnp.zeros((bsz, 2, c_lru), F32), n_ctx)
    y_lru, _ = _rglru(p_lru, c_lru, p_lru, 0, *lru_args, h_ctx, GRID_W)

    taps, norm = _hyena_filter(n_lat, hy_f_w1[l], hy_f_b1[l], hy_f_w2[l], hy_f_b2[l], hy_f_w3[l],
                               hy_f_b3[l], hy_f_wout[l], hy_f_freq[l])
    kspec = _filter_spectrum(taps, norm, mats, c_hy)
    yspec = _conv_forward(wf, kspec, mats)
    y_hy = _conv_inverse(yspec, wf, x0, hy_bias[l], mats)

    x1 = _out_projection(y_hy.reshape(bsz * n_lat, c_hy), y_lru.reshape(bsz * n_lat, c_lru), w_out_b,
                         b_out[l], x.reshape(bsz * n_lat, d), gt1, n_lat)
    x1 = x1.reshape(bsz, n_lat, d)

    h2, aff = _router(x1, g_ffn[l], sh2, sc2, w_router[l])
    idx, gates, dest, span = _select_tokens(aff, cap)
    rows_g = (idx + (jnp.arange(bsz, dtype=jnp.int32) * n_lat)[:, None, None])
    expert_major = lambda v: jnp.swapaxes(v, 0, 1).reshape(-1)
    xs = _gather_rows(expert_major(rows_g), h2)
    hmid = _expert_up(xs, w_exp_gate[l], w_exp_up[l], bsz * cap)
    ys = _expert_down(expert_major(dest), hmid, w_exp_down[l], expert_major(gates).reshape(-1, 1), bsz * cap)
    slots = n_exp * cap
    base = (jnp.arange(bsz, dtype=jnp.int32) * slots)[:, None]
    bounds = jnp.concatenate([span[:, 0, ::COMBINE_SUB].astype(jnp.int32) + base, base + slots], axis=1)
    bounds = bounds.reshape(-1)
    return _combine(bounds, jnp.swapaxes(span, 1, 2), ys, x1, gt2, g_final, slots)
```

```python
import functools
import math

import jax
import jax.numpy as jnp
from jax import lax
from jax.experimental import pallas as pl
from jax.experimental.pallas import tpu as pltpu

F32 = jnp.float32
BF16 = jnp.bfloat16

GRID_W = 64
HY_SHORT_LEFT = 1
FILT_BANDS = 16
DECAY_TARGET = 1e-2
MIN_DECAY = math.log(DECAY_TARGET) / 1.5
MAX_DECAY = math.log(DECAY_TARGET) / 0.3
LRU_HEADS = 16
LRU_CONV_LEFT = 2
LRU_C = 8.0
CAPACITY_FACTOR = 2
N_MOD = 6
EPS = 1e-6

LANES = 128
SUBLANES = 8
VMEM_LIMIT = 56 * 2 ** 20


def _params(sem):
    return pltpu.CompilerParams(dimension_semantics=sem, vmem_limit_bytes=VMEM_LIMIT)


def _tile(n, pref):
    t = min(n, pref)
    while n % t:
        t //= 2
    return t


def _split_bf16(v):
    hi = v.astype(BF16)
    lo = (v - hi.astype(F32)).astype(BF16)
    return hi, lo


def _mod_kernel(c_ref, w_ref, b_ref, o_ref):
    c = c_ref[...]
    s = c * jax.nn.sigmoid(c)
    s_hi, s_lo = _split_bf16(s)
    w_hi, w_lo = _split_bf16(w_ref[...])
    rows = c.shape[0]
    r = jnp.dot(jnp.concatenate([s_hi, s_lo], axis=0), w_hi, preferred_element_type=F32)
    acc = r[:rows] + r[rows:] + jnp.dot(s_hi, w_lo, preferred_element_type=F32)
    o_ref[...] = acc + b_ref[...]


def _modulation(c_all, w_mod, b_mod):
    rows, d = c_all.shape
    n = w_mod.shape[1]
    tn = _tile(n, 512)
    return pl.pallas_call(
        _mod_kernel,
        grid=(n // tn,),
        in_specs=[pl.BlockSpec((rows, d), lambda j: (0, 0)),
                  pl.BlockSpec((d, tn), lambda j: (0, j)),
                  pl.BlockSpec((1, tn), lambda j: (0, j))],
        out_specs=pl.BlockSpec((rows, tn), lambda j: (0, j)),
        out_shape=jax.ShapeDtypeStruct((rows, n), F32),
        compiler_params=_params(("parallel",)),
        name="modulation",
    )(c_all, w_mod, b_mod.reshape(1, n))


def _rms_mod(x, g, shift, scale):
    y = x * lax.rsqrt(jnp.mean(x * x, axis=-1, keepdims=True) + EPS) * g
    return y * (1.0 + scale) + shift


def _norm_kernel(x_ref, g_ref, sh_ref, sc_ref, o_ref):
    o_ref[0] = _rms_mod(x_ref[0], g_ref[...], sh_ref[0], sc_ref[0]).astype(o_ref.dtype)


def _norm_mod(x, g, shift, scale, per_batch):
    b, t, d = x.shape
    tt = _tile(t, 512)
    mod_map = (lambda i, j: (i, 0, 0)) if per_batch else (lambda i, j: (0, 0, 0))
    return pl.pallas_call(
        _norm_kernel,
        grid=(b, t // tt),
        in_specs=[pl.BlockSpec((1, tt, d), lambda i, j: (i, j, 0)),
                  pl.BlockSpec((1, d), lambda i, j: (0, 0)),
                  pl.BlockSpec((1, 1, d), mod_map),
                  pl.BlockSpec((1, 1, d), mod_map)],
        out_specs=pl.BlockSpec((1, tt, d), lambda i, j: (i, j, 0)),
        out_shape=jax.ShapeDtypeStruct((b, t, d), BF16),
        compiler_params=_params(("parallel", "parallel")),
        name="adaln_norm",
    )(x, g.reshape(1, d), shift, scale)


def _proj_kernel(a_ref, w_ref, b_ref, o_ref):
    acc = jnp.dot(a_ref[...], w_ref[...], preferred_element_type=F32)
    o_ref[...] = (acc + b_ref[...]).astype(o_ref.dtype)


def _projection(a, w, bias, col_start, n_cols, out_dtype):
    m, k = a.shape
    tm = _tile(m, 1024)
    tn = _tile(n_cols, 512)
    off = col_start // tn
    return pl.pallas_call(
        _proj_kernel,
        grid=(m // tm, n_cols // tn),
        in_specs=[pl.BlockSpec((tm, k), lambda i, j: (i, 0)),
                  pl.BlockSpec((k, tn), lambda i, j: (0, j + off)),
                  pl.BlockSpec((1, tn), lambda i, j: (0, j + off))],
        out_specs=pl.BlockSpec((tm, tn), lambda i, j: (i, j)),
        out_shape=jax.ShapeDtypeStruct((m, n_cols), out_dtype),
        compiler_params=_params(("parallel", "arbitrary")),
        name="projection",
    )(a, w, bias.reshape(1, -1))


def _short_conv(x, w_ref, bias, left, group):
    rows = x.shape[0]
    pos = lax.broadcasted_iota(jnp.int32, (rows, 1), 0) & (group - 1)
    y = bias + w_ref[left:left + 1, :] * x
    for k in range(w_ref.shape[0]):
        off = k - left
        if off == 0:
            continue
        shifted = pltpu.roll(x, (-off) % rows, axis=0)
        valid = (pos + off >= 0) & (pos + off < group)
        y = y + w_ref[k:k + 1, :] * jnp.where(valid, shifted, 0.0)
    return y


def _proj_hyena_kernel(a_ref, w0_ref, w1_ref, w2_ref, b0_ref, b1_ref, b2_ref,
                       cw0_ref, cw1_ref, cw2_ref, cb0_ref, cb1_ref, cb2_ref, x0_ref, wf_ref):
    a = a_ref[...]

    def branch(w_ref, b_ref, cw_ref, cb_ref):
        p = jnp.dot(a, w_ref[...], preferred_element_type=F32) + b_ref[...]
        return _short_conv(p, cw_ref, cb_ref[...], HY_SHORT_LEFT, GRID_W)

    x0_ref[...] = branch(w0_ref, b0_ref, cw0_ref, cb0_ref)
    wf_ref[...] = branch(w1_ref, b1_ref, cw1_ref, cb1_ref) * branch(w2_ref, b2_ref, cw2_ref, cb2_ref)


def _projection_hyena(a, w, bias, conv_w, conv_b, c):
    m, k = a.shape
    tm = _tile(m, 1024)
    assert tm % GRID_W == 0
    tc = _tile(c, 256)
    nc = c // tc
    kw = conv_w.shape[0]
    wspec = lambda g: pl.BlockSpec((k, tc), lambda i, j: (0, j + g * nc))
    bspec = lambda g: pl.BlockSpec((1, tc), lambda i, j: (0, j + g * nc))
    cwspec = lambda g: pl.BlockSpec((kw, tc), lambda i, j: (0, j + g * nc))
    ospec = pl.BlockSpec((tm, tc), lambda i, j: (i, j))
    b2 = bias.reshape(1, -1)
    cb = conv_b.reshape(1, -1)
    shape = jax.ShapeDtypeStruct((m, c), F32)
    return pl.pallas_call(
        _proj_hyena_kernel,
        grid=(m // tm, nc),
        in_specs=[pl.BlockSpec((tm, k), lambda i, j: (i, 0)),
                  wspec(0), wspec(1), wspec(2), bspec(0), bspec(1), bspec(2),
                  cwspec(0), cwspec(1), cwspec(2), bspec(0), bspec(1), bspec(2)],
        out_specs=[ospec, ospec],
        out_shape=[shape, shape],
        compiler_params=_params(("parallel", "arbitrary")),
        name="projection_hyena",
    )(a, w, w, w, b2, b2, b2, conv_w, conv_w, conv_w, cb, cb, cb)


def _hp_dot(a, b):
    return jnp.dot(a, b, preferred_element_type=F32, precision=lax.Precision.HIGHEST)


def _filter_kernel(n, w1t_ref, w1c_ref, w1s_ref, b1_ref, w2_ref, b2_ref, w3_ref, b3_ref,
                   wout_ref, freq_ref, delta_ref, h_ref, norm_ref):
    i = pl.program_id(0)
    tn = h_ref.shape[0]
    pos_i = i * tn + lax.broadcasted_iota(jnp.int32, (tn, 1), 0)
    pos = pos_i.astype(F32)
    t = pos * (1.0 / (n - 1))
    band_step = (FILT_BANDS - 1 - 1e-4) / (FILT_BANDS - 1)
    bands = 1e-4 + band_step * lax.broadcasted_iota(jnp.int32, (1, FILT_BANDS), 1).astype(F32)
    ang = (2.0 * math.pi * pos / n) * bands
    fr = freq_ref[...]
    pre = t * w1t_ref[...] + _hp_dot(jnp.cos(ang), w1c_ref[...]) - _hp_dot(jnp.sin(ang), w1s_ref[...])
    h = jnp.sin(fr * (pre + b1_ref[...]))
    h = jnp.sin(fr * (_hp_dot(h, w2_ref[...]) + b2_ref[...]))
    h = jnp.sin(fr * (_hp_dot(h, w3_ref[...]) + b3_ref[...]))
    h_hi, h_lo = _split_bf16(h)
    w_hi, w_lo = _split_bf16(wout_ref[...])
    proj = (jnp.dot(h_hi, w_hi, preferred_element_type=F32) + jnp.dot(h_lo, w_hi, preferred_element_type=F32)
            + jnp.dot(h_hi, w_lo, preferred_element_type=F32))
    taps = proj * jnp.exp(-t * delta_ref[...])

    @pl.when(i == 0)
    def _():
        norm_ref[...] = jnp.zeros_like(norm_ref)

    norm_ref[...] += jnp.sum(jnp.abs(taps), axis=0, keepdims=True)
    c = taps.shape[1] // 2
    col = lax.broadcasted_iota(jnp.int32, (1, taps.shape[1]), 1)
    drop = (pos_i == 0) & (col >= c)
    h_ref[...] = jnp.where(drop, 0.0, taps).astype(h_ref.dtype)


def _hyena_filter(n, w1, b1, w2, b2, w3, b3, wout, freq):
    hid = w1.shape[1]
    c2 = wout.shape[1]
    c = c2 // 2
    tn = _tile(n, 512)
    deltas = jnp.abs(jnp.linspace(MIN_DECAY, MAX_DECAY, c, dtype=F32))
    deltas = jnp.concatenate([deltas, deltas]).reshape(1, c2)
    full = lambda shape: pl.BlockSpec(shape, lambda i: (0, 0))
    return pl.pallas_call(
        functools.partial(_filter_kernel, n),
        grid=(n // tn,),
        in_specs=[full((1, hid)), full((FILT_BANDS, hid)), full((FILT_BANDS, hid)), full((1, hid)),
                  full((hid, hid)), full((1, hid)), full((hid, hid)), full((1, hid)),
                  full((hid, c2)), full((1, hid)), full((1, c2))],
        out_specs=[pl.BlockSpec((tn, c2), lambda i: (i, 0)), full((1, c2))],
        out_shape=[jax.ShapeDtypeStruct((n, c2), F32), jax.ShapeDtypeStruct((1, c2), F32)],
        compiler_params=_params(("arbitrary",)),
        name="hyena_filter",
    )(w1[0:1], w1[1:1 + FILT_BANDS], w1[1 + FILT_BANDS:], b1.reshape(1, hid), w2, b2.reshape(1, hid),
      w3, b3.reshape(1, hid), wout, freq.reshape(1, hid), deltas)


FFT_P = LANES
FFT_R = SUBLANES


def _phase(num, den):
    return (num % den).astype(F32) * (2.0 * math.pi / den)


def _fft_matrices(n):
    p, r = FFT_P, FFT_R
    q = n // p
    n2 = 2 * n
    iq = jnp.arange(q, dtype=jnp.int32)
    eye = jnp.eye(r, dtype=F32)
    alt = lambda v: (1 - 2 * (v & 1)).astype(F32)

    ang = _phase(iq[:, None] * iq[None, :], 2 * q)
    a_re = jnp.cos(ang)
    a_im = (-jnp.sin(ang)).at[0].set(alt(iq))
    core = jnp.stack([a_re, a_im], axis=1)
    m1 = jnp.einsum('fks,rt->fkrst', core, eye).reshape(q * 2 * r, q * r)

    b_c = jnp.cos(ang.T)
    b_s = (-jnp.sin(ang.T)).at[:, 0].set(alt(iq))
    core = jnp.stack([b_c, b_s], axis=2)
    i2 = jnp.einsum('tfk,rs->trfks', core, eye).reshape(q * r, q * 2 * r)

    h = p // 2
    f2 = jnp.arange(h, dtype=jnp.int32)
    s2 = jnp.arange(p, dtype=jnp.int32)
    f_lo = iq[:, None] + 2 * q * f2[None, :]
    f_hi = jnp.where(iq[:, None] == 0, q, 2 * q - iq[:, None]) + 2 * q * f2[None, :]
    freq = jnp.stack([f_lo, f_hi], axis=1)
    phi = _phase(freq[..., None] * s2, n2)
    c, s = jnp.cos(phi), jnp.sin(phi)
    zero = jnp.zeros_like(c[:, 0])
    first = (iq == 0)[:, None, None]
    dcrow = (first & (f2 == 0)[None, :, None])
    nyq = jnp.broadcast_to(alt(s2), c[:, 0].shape)
    on_gr = jnp.stack([jnp.where(first, c[:, 0], c[:, 0]),
                       jnp.where(dcrow, nyq, -s[:, 0]),
                       jnp.where(first, zero, c[:, 1]),
                       jnp.where(first, zero, -s[:, 1])], axis=1)
    on_gi = jnp.stack([jnp.where(first, zero, s[:, 0]),
                       jnp.where(first, zero, c[:, 0]),
                       jnp.where(first, c[:, 1], -s[:, 1]),
                       jnp.where(first, -s[:, 1], -c[:, 1])], axis=1)
    m2 = jnp.concatenate([on_gr, on_gi], axis=-1).reshape(q, 4 * h, 2 * p)
    ct, st = jnp.swapaxes(c, 2, 3), jnp.swapaxes(s, 2, 3)
    zt = jnp.zeros_like(ct[:, 0])
    dccol = (first & (f2 == 0)[None, None, :])
    nyq_t = jnp.broadcast_to(alt(s2)[:, None], ct[:, 0].shape)
    hc = jnp.stack([ct[:, 0], jnp.where(dccol, nyq_t, -st[:, 0]),
                    jnp.where(first, zt, ct[:, 1]), jnp.where(first, zt, -st[:, 1])], axis=2)
    hs = jnp.stack([jnp.where(first, zt, st[:, 0]), jnp.where(first, zt, ct[:, 0]),
                    jnp.where(first, ct[:, 1], -st[:, 1]), jnp.where(first, -st[:, 1], -ct[:, 1])], axis=2)
    i1 = jnp.stack([hc, hs], axis=1).reshape(q, 2 * p, 4 * h)
    return tuple(m.astype(BF16) for m in (m1, m2, i1, i2))


def _fft_stage_a(u_ref, m1_ref, g_ref):
    q, groups, r, tc = u_ref.shape
    m1 = m1_ref[...]
    for g in range(groups):
        blk = u_ref[:, g].reshape(q * r, tc).astype(BF16)
        g_ref[g] = jnp.dot(m1, blk, preferred_element_type=F32).reshape(q, 2, r, tc)


def _fft_stage_b(g_ref, f1, m2):
    groups, _, _, r, tc = g_ref.shape
    z = jnp.concatenate([g_ref[:, f1, 0].reshape(groups * r, tc),
                         g_ref[:, f1, 1].reshape(groups * r, tc)], axis=0).astype(BF16)
    return jnp.dot(m2, z, preferred_element_type=F32)


def _cmul_packed(x, k, first):
    h = x.shape[0] // 4
    xr = (x[0:h], x[2 * h:3 * h])
    xi = (x[h:2 * h], x[3 * h:4 * h])
    kr = (k[0:h], k[2 * h:3 * h])
    ki = (k[h:2 * h], k[3 * h:4 * h])
    real_pair = first & (lax.broadcasted_iota(jnp.int32, (h, 1), 0) == 0)
    lo_re = jnp.where(real_pair, xr[0] * kr[0], xr[0] * kr[0] - xi[0] * ki[0])
    lo_im = jnp.where(real_pair, xi[0] * ki[0], xr[0] * ki[0] + xi[0] * kr[0])
    return jnp.concatenate([lo_re, lo_im, xr[1] * kr[1] - xi[1] * ki[1], xr[1] * ki[1] + xi[1] * kr[1]], axis=0)


def _filter_spec_kernel(n, hf_ref, hb_ref, nf_ref, nb_ref, m1_ref, m2_ref, k_ref, gf_ref, gb_ref):
    qi = pl.program_id(1)
    qb = m2_ref.shape[0]

    @pl.when(qi == 0)
    def _():
        _fft_stage_a(hf_ref, m1_ref, gf_ref)
        _fft_stage_a(hb_ref, m1_ref, gb_ref)

    inv_norm = 1.0 / (nf_ref[...] + nb_ref[...])
    h = m2_ref.shape[1] // 4
    row = lax.broadcasted_iota(jnp.int32, (4 * h, 1), 0)
    for k in range(qb):
        f1 = qi * qb + k
        xf = _fft_stage_b(gf_ref, f1, m2_ref[k])
        xb = _fft_stage_b(gb_ref, f1, m2_ref[k])
        real_pair = (f1 == 0) & ((row == 0) | (row == h))
        imag_row = ((row >= h) & (row < 2 * h)) | (row >= 3 * h)
        spec = jnp.where(imag_row & jnp.logical_not(real_pair), xf - xb, xf + xb)
        spec = spec * jnp.where(real_pair, 0.5 / n, 1.0 / n) * inv_norm
        k_ref[k] = spec.reshape(k_ref.shape[1:]).astype(k_ref.dtype)


def _time_view(v, n):
    return v.reshape(v.shape[:-2] + (n // FFT_P, FFT_P // FFT_R, FFT_R, v.shape[-1]))


def _filter_spectrum(taps, norm, mats, c):
    n = taps.shape[0]
    m1, m2, _, _ = mats
    q = n // FFT_P
    h = FFT_P // 2
    tc = _tile(c, 256)
    nc = c // tc
    qb = _tile(q, 8)
    tv = _time_view(taps, n)
    blk = (q, FFT_P // FFT_R, FFT_R, tc)
    return pl.pallas_call(
        functools.partial(_filter_spec_kernel, n),
        grid=(nc, q // qb),
        in_specs=[pl.BlockSpec(blk, lambda j, i: (0, 0, 0, j)),
                  pl.BlockSpec(blk, lambda j, i: (0, 0, 0, j + nc)),
                  pl.BlockSpec((1, tc), lambda j, i: (0, j)),
                  pl.BlockSpec((1, tc), lambda j, i: (0, j + nc)),
                  pl.BlockSpec(m1.shape, lambda j, i: (0, 0)),
                  pl.BlockSpec((qb,) + m2.shape[1:], lambda j, i: (i, 0, 0))],
        out_specs=pl.BlockSpec((qb, 4, h, tc), lambda j, i: (i, 0, 0, j)),
        out_shape=jax.ShapeDtypeStruct((q, 4, h, c), BF16),
        scratch_shapes=[pltpu.VMEM((FFT_P // FFT_R, q, 2, FFT_R, tc), F32) for _ in range(2)],
        compiler_params=_params(("parallel", "arbitrary")),
        name="filter_spectrum",
    )(tv, tv, norm, norm, m1, m2)


def _conv_fwd_kernel(w_ref, m1_ref, m2_ref, k_ref, y_ref, g_ref):
    qi = pl.program_id(2)
    qb = m2_ref.shape[0]

    @pl.when(qi == 0)
    def _():
        _fft_stage_a(w_ref.at[0], m1_ref, g_ref)

    for k in range(qb):
        f1 = qi * qb + k
        x = _fft_stage_b(g_ref, f1, m2_ref[k])
        y = _cmul_packed(x, k_ref[k].reshape(x.shape).astype(F32), f1 == 0)
        y_ref[0, k] = y.reshape(y_ref.shape[2:]).astype(y_ref.dtype)


def _conv_forward(wf, kspec, mats):
    b, n, c = wf.shape
    m1, m2, _, _ = mats
    q = n // FFT_P
    h = FFT_P // 2
    tc = _tile(c, 512)
    qb = _tile(q, 8)
    return pl.pallas_call(
        _conv_fwd_kernel,
        grid=(c // tc, b, q // qb),
        in_specs=[pl.BlockSpec((1, q, FFT_P // FFT_R, FFT_R, tc), lambda j, bi, i: (bi, 0, 0, 0, j)),
                  pl.BlockSpec(m1.shape, lambda j, bi, i: (0, 0)),
                  pl.BlockSpec((qb,) + m2.shape[1:], lambda j, bi, i: (i, 0, 0)),
                  pl.BlockSpec((qb, 4, h, tc), lambda j, bi, i: (i, 0, 0, j))],
        out_specs=pl.BlockSpec((1, qb, 4, h, tc), lambda j, bi, i: (bi, i, 0, 0, j)),
        out_shape=jax.ShapeDtypeStruct((b, q, 4, h, c), BF16),
        scratch_shapes=[pltpu.VMEM((FFT_P // FFT_R, q, 2, FFT_R, tc), F32)],
        compiler_params=_params(("parallel", "parallel", "arbitrary")),
        name="conv_forward_fft",
    )(_time_view(wf, n), m1, m2, kspec)


def _conv_inv_kernel(y_ref, i1_ref, i2_ref, wf_ref, x0_ref, bias_ref, o_ref, h_ref):
    qi = pl.program_id(2)
    qb = i1_ref.shape[0]
    groups, q, _, r, tc = h_ref.shape
    for k in range(qb):
        hv = jnp.dot(i1_ref[k], y_ref[0, k].reshape(i1_ref.shape[2], tc), preferred_element_type=F32)
        half = hv.shape[0] // 2
        h_ref[:, qi * qb + k, 0] = hv[:half].reshape(groups, r, tc)
        h_ref[:, qi * qb + k, 1] = hv[half:].reshape(groups, r, tc)

    @pl.when(qi == pl.num_programs(2) - 1)
    def _():
        i2 = i2_ref[...]
        bias = bias_ref[...]
        for g2 in range(groups // 2):
            parts = []
            for g in (2 * g2, 2 * g2 + 1):
                z = jnp.dot(i2, h_ref[g].reshape(q * 2 * r, tc).astype(BF16), preferred_element_type=F32)
                z = z.reshape(q, r, tc)
                parts.append(x0_ref[0, :, g] * (z + wf_ref[0, :, g] * bias))
            o_ref[0, :, g2] = jnp.concatenate(parts, axis=1).astype(o_ref.dtype)


def _conv_inverse(yspec, wf, x0, bias, mats):
    b, n, c = wf.shape
    _, _, i1, i2 = mats
    q = n // FFT_P
    h = FFT_P // 2
    groups = FFT_P // FFT_R
    tc = _tile(c, 256)
    qb = _tile(q, 8)
    tspec = pl.BlockSpec((1, q, groups, FFT_R, tc), lambda j, bi, i: (bi, 0, 0, 0, j))
    out = pl.pallas_call(
        _conv_inv_kernel,
        grid=(c // tc, b, q // qb),
        in_specs=[pl.BlockSpec((1, qb, 4, h, tc), lambda j, bi, i: (bi, i, 0, 0, j)),
                  pl.BlockSpec((qb,) + i1.shape[1:], lambda j, bi, i: (i, 0, 0)),
                  pl.BlockSpec(i2.shape, lambda j, bi, i: (0, 0)),
                  tspec, tspec,
                  pl.BlockSpec((1, tc), lambda j, bi, i: (0, j))],
        out_specs=pl.BlockSpec((1, q, groups // 2, 2 * FFT_R, tc), lambda j, bi, i: (bi, 0, 0, 0, j)),
        out_shape=jax.ShapeDtypeStruct((b, q, groups // 2, 2 * FFT_R, c), BF16),
        scratch_shapes=[pltpu.VMEM((groups, q, 2, FFT_R, tc), F32)],
        compiler_params=_params(("parallel", "parallel", "arbitrary")),
        name="conv_inverse_fft",
    )(yspec, i1, i2, _time_view(wf, n), _time_view(x0, n), bias.reshape(1, c))
    return out.reshape(b, n, c)


SCAN_UNROLL = 4


def _tile_scan(a, b, reverse):
    rows, c = a.shape
    a3 = a.reshape(rows // SUBLANES, SUBLANES, c)
    b3 = b.reshape(rows // SUBLANES, SUBLANES, c)
    sub = lax.broadcasted_iota(jnp.int32, (1, SUBLANES, 1), 1)
    for k in (1, 2, 4):
        shift = SUBLANES - k if reverse else k
        valid = (sub < SUBLANES - k) if reverse else (sub >= k)
        b3 = b3 + a3 * jnp.where(valid, pltpu.roll(b3, shift, axis=1), 0.0)
        a3 = a3 * jnp.where(valid, pltpu.roll(a3, shift, axis=1), 1.0)
    return a3.reshape(rows, c), b3.reshape(rows, c)


def _lru_kernel(group, has_gate, *refs):
    if has_gate:
        (x_ref, gate_ref, cw_ref, cb_ref, wa_ref, ba_ref, wx_ref, bx_ref, lam_ref, h0_ref,
         y_ref, ht_ref, af_ref, bf_ref, ab_ref, bb_ref) = refs
    else:
        (x_ref, cw_ref, cb_ref, wa_ref, ba_ref, wx_ref, bx_ref, lam_ref, h0_ref,
         ht_ref, af_ref, bf_ref, ab_ref, bb_ref) = refs
    t_len = x_ref.shape[1]
    chunk = _tile(t_len, 512)
    a_refs = (af_ref, ab_ref)
    b_refs = (bf_ref, bb_ref)

    def coeffs(ci, carry):
        r0 = pl.multiple_of(ci * chunk, chunk)
        xc = _short_conv(x_ref[0, pl.ds(r0, chunk), :], cw_ref, cb_ref[...], LRU_CONV_LEFT, group)
        xcb = xc.astype(BF16)
        half_xc = 0.5 * xc
        for d in range(2):
            ta = jnp.tanh(jnp.dot(xcb, wa_ref[d, 0], preferred_element_type=F32) + ba_ref[d])
            ti = jnp.tanh(jnp.dot(xcb, wx_ref[d, 0], preferred_element_type=F32) + bx_ref[d])
            lam = lam_ref[d]
            softplus_neg = jnp.maximum(-lam, 0.0) + jnp.log(1.0 + jnp.exp(-jnp.abs(lam)))
            c1 = (-0.5 * LRU_C) * softplus_neg
            a = jnp.exp(c1 * ta + c1)
            a_tile, b_tile = _tile_scan(a, jnp.sqrt(1.0 - a * a) * (half_xc * ti + half_xc), d == 1)
            a_refs[d][pl.ds(r0, chunk), :] = a_tile
            b_refs[d][pl.ds(r0, chunk), :] = b_tile
        return carry

    lax.fori_loop(0, t_len // chunk, coeffs, 0)

    n_tiles = t_len // SUBLANES
    unroll = _tile(n_tiles, SCAN_UNROLL)

    def scan(i, carry):
        hf, hb = carry
        for u in range(unroll):
            kf = i * unroll + u
            rf = pl.multiple_of(kf * SUBLANES, SUBLANES)
            tile_f = af_ref[pl.ds(rf, SUBLANES), :] * hf + bf_ref[pl.ds(rf, SUBLANES), :]
            bf_ref[pl.ds(rf, SUBLANES), :] = tile_f
            hf = tile_f[SUBLANES - 1:SUBLANES, :]
            rb = pl.multiple_of((n_tiles - 1 - kf) * SUBLANES, SUBLANES)
            tile_b = ab_ref[pl.ds(rb, SUBLANES), :] * hb + bb_ref[pl.ds(rb, SUBLANES), :]
            bb_ref[pl.ds(rb, SUBLANES), :] = tile_b
            hb = tile_b[0:1, :]
        return hf, hb

    hf, hb = lax.fori_loop(0, n_tiles // unroll, scan, (h0_ref[0, 0:1, :], h0_ref[0, 1:2, :]))
    ht_ref[0, 0:1, :] = hf
    ht_ref[0, 1:2, :] = hb

    if has_gate:
        def emit(ci, carry):
            r0 = pl.multiple_of(ci * chunk, chunk)
            hs = bf_ref[pl.ds(r0, chunk), :] + bb_ref[pl.ds(r0, chunk), :]
            y_ref[0, pl.ds(r0, chunk), :] = (jax.nn.gelu(gate_ref[0, pl.ds(r0, chunk), :]) * hs).astype(y_ref.dtype)
            return carry

        lax.fori_loop(0, t_len // chunk, emit, 0)


def _block_diag(w, heads_per_tile):
    d2, h, hd, _ = w.shape
    w = w.reshape(d2, h // heads_per_tile, heads_per_tile, hd, hd)
    eye = jnp.eye(heads_per_tile, dtype=w.dtype)
    bd = jnp.einsum('dghij,hq->dghiqj', w, eye)
    return bd.reshape(d2, h // heads_per_tile, heads_per_tile * hd, heads_per_tile * hd).astype(BF16)


def _rglru(xsrc, x_col, gsrc, g_col, conv_w, conv_b, wa, ba, wx, bx, lam, h0, group):
    b, t, _ = xsrc.shape
    c = conv_w.shape[1]
    hd = c // LRU_HEADS
    tc = min(c, max(hd, 256))
    hp = tc // hd
    nt = c // tc
    has_gate = gsrc is not None
    wa_bd = _block_diag(0.5 * wa, hp)
    wx_bd = _block_diag(0.5 * wx, hp)
    ba, bx = 0.5 * ba, 0.5 * bx
    xo, go = x_col // tc, (g_col // tc if has_gate else 0)
    vec = lambda: pl.BlockSpec((2, 1, tc), lambda i, j: (0, 0, j))
    mat = lambda: pl.BlockSpec((2, 1, tc, tc), lambda i, j: (0, j, 0, 0))
    in_specs = [pl.BlockSpec((1, t, tc), lambda i, j: (i, 0, j + xo))]
    args = [xsrc]
    if has_gate:
        in_specs.append(pl.BlockSpec((1, t, tc), lambda i, j: (i, 0, j + go)))
        args.append(gsrc)
    in_specs += [pl.BlockSpec((conv_w.shape[0], tc), lambda i, j: (0, j)),
                 pl.BlockSpec((1, tc), lambda i, j: (0, j)),
                 mat(), vec(), mat(), vec(), vec(),
                 pl.BlockSpec((1, 2, tc), lambda i, j: (i, 0, j))]
    args += [conv_w, conv_b.reshape(1, c), wa_bd, ba.reshape(2, 1, c), wx_bd, bx.reshape(2, 1, c),
             lam.reshape(2, 1, c), h0]
    ht_spec = pl.BlockSpec((1, 2, tc), lambda i, j: (i, 0, j))
    ht_shape = jax.ShapeDtypeStruct((b, 2, c), F32)
    if has_gate:
        out_specs = [pl.BlockSpec((1, t, tc), lambda i, j: (i, 0, j)), ht_spec]
        out_shape = [jax.ShapeDtypeStruct((b, t, c), BF16), ht_shape]
    else:
        out_specs = [ht_spec]
        out_shape = [ht_shape]
    return pl.pallas_call(
        functools.partial(_lru_kernel, group, has_gate),
        grid=(b, nt),
        in_specs=in_specs,
        out_specs=out_specs,
        out_shape=out_shape,
        scratch_shapes=[pltpu.VMEM((t, tc), F32) for _ in range(4)],
        compiler_params=_params(("parallel", "parallel")),
        name="rglru" if has_gate else "rglru_context",
    )(*args)


def _out_proj_kernel(a1_ref, a2_ref, w1_ref, w2_ref, b_ref, x_ref, g_ref, o_ref):
    acc = jnp.dot(a1_ref[...], w1_ref[...], preferred_element_type=F32)
    acc = acc + jnp.dot(a2_ref[...], w2_ref[...], preferred_element_type=F32)
    o_ref[...] = x_ref[...] + g_ref[0] * (acc + b_ref[...])


def _out_projection(a1, a2, w, bias, x, gate, t):
    m, k1 = a1.shape
    n = w.shape[1]
    tm = _tile(t, 1024)
    tn = _tile(n, 512)
    per_b = t // tm
    k1_blocks = 1
    return pl.pallas_call(
        _out_proj_kernel,
        grid=(m // tm, n // tn),
        in_specs=[pl.BlockSpec((tm, k1), lambda i, j: (i, 0)),
                  pl.BlockSpec((tm, k1), lambda i, j: (i, 0)),
                  pl.BlockSpec((k1, tn), lambda i, j: (0, j)),
                  pl.BlockSpec((k1, tn), lambda i, j: (k1_blocks, j)),
                  pl.BlockSpec((1, tn), lambda i, j: (0, j)),
                  pl.BlockSpec((tm, tn), lambda i, j: (i, j)),
                  pl.BlockSpec((1, 1, tn), lambda i, j: (i // per_b, 0, j))],
        out_specs=pl.BlockSpec((tm, tn), lambda i, j: (i, j)),
        out_shape=jax.ShapeDtypeStruct((m, n), F32),
        compiler_params=_params(("parallel", "arbitrary")),
        name="out_projection",
    )(a1, a2, w, w, bias.reshape(1, n), x, gate)


def _router_kernel(x_ref, g_ref, sh_ref, sc_ref, wr_ref, h_ref, aff_ref):
    h = _rms_mod(x_ref[0], g_ref[...], sh_ref[0], sc_ref[0])
    half = h.shape[1] // 2
    h_ref[...] = pltpu.pack_elementwise([h[:, :half], h[:, half:]], packed_dtype=BF16)
    h_hi, h_lo = _split_bf16(h)
    w_hi, w_lo = _split_bf16(wr_ref[...])
    nt = (((1,), (1,)), ((), ()))
    logits = (lax.dot_general(w_hi, h_hi, nt, preferred_element_type=F32)
              + lax.dot_general(w_lo, h_hi, nt, preferred_element_type=F32)
              + lax.dot_general(w_hi, h_lo, nt, preferred_element_type=F32))
    z = jnp.exp(logits - jnp.max(logits, axis=0, keepdims=True))
    aff_ref[0] = z / jnp.sum(z, axis=0, keepdims=True)


def _router(x1, g, shift, scale, w_router):
    b, t, d = x1.shape
    e = w_router.shape[1]
    tt = _tile(t, 512)
    per_b = t // tt
    mod_map = lambda i, j: (i, 0, 0)
    return pl.pallas_call(
        _router_kernel,
        grid=(b, per_b),
        in_specs=[pl.BlockSpec((1, tt, d), lambda i, j: (i, j, 0)),
                  pl.BlockSpec((1, d), lambda i, j: (0, 0)),
                  pl.BlockSpec((1, 1, d), mod_map),
                  pl.BlockSpec((1, 1, d), mod_map),
                  pl.BlockSpec((e, d), lambda i, j: (0, 0))],
        out_specs=[pl.BlockSpec((tt, d // 2), lambda i, j: (i * per_b + j, 0)),
                   pl.BlockSpec((1, e, tt), lambda i, j: (i, 0, j))],
        out_shape=[jax.ShapeDtypeStruct((b * t, d // 2), jnp.uint32),
                   jax.ShapeDtypeStruct((b, e, t), F32)],
        compiler_params=_params(("parallel", "parallel")),
        name="router",
    )(x1, g.reshape(1, d), shift, scale, w_router.T)


def _prefix_count(mask_ref, out_ref):
    e, t = mask_ref.shape
    blk = min(t, LANES)
    tri = (lax.broadcasted_iota(jnp.int32, (blk, blk), 0)
           < lax.broadcasted_iota(jnp.int32, (blk, blk), 1)).astype(BF16)
    carry = jnp.zeros((e, 1), F32)
    for k in range(t // blk):
        m = mask_ref[:, k * blk:(k + 1) * blk]
        out_ref[:, k * blk:(k + 1) * blk] = jnp.dot(m.astype(BF16), tri, preferred_element_type=F32) + carry
        carry = carry + jnp.sum(m, axis=1, keepdims=True)
    return carry


def _topk_kernel(cap, aff_ref, idx_ref, gate_ref, dest_ref, span_ref, mask_ref, pos_ref, cnt_ref, q_ref):
    a = aff_ref[0]
    e, t = a.shape
    min_normal = 0x00800000

    def refine(i, thr):
        cand = thr | jnp.left_shift(jnp.int32(1), 30 - i)
        cnt = jnp.sum(jnp.where(a >= pltpu.bitcast(cand, F32), 1.0, 0.0), axis=1, keepdims=True)
        return jnp.where((cnt >= cap) & (cand >= min_normal), cand, thr)

    thr = lax.fori_loop(0, 31, refine, jnp.zeros((e, 1), jnp.int32))
    above = a >= pltpu.bitcast(jnp.maximum(thr + 1, min_normal), F32)
    tied = (a >= pltpu.bitcast(thr, F32)) & jnp.logical_not(above)
    need = cap - jnp.sum(jnp.where(above, 1.0, 0.0), axis=1, keepdims=True)
    mask_ref[...] = jnp.where(tied, 1.0, 0.0)
    _prefix_count(mask_ref, pos_ref)
    sel = above | (tied & (pos_ref[...] < need))
    mask_ref[...] = jnp.where(sel, 1.0, 0.0)
    _prefix_count(mask_ref, pos_ref)

    mask = mask_ref[...]
    cnt = jnp.sum(mask, axis=0, keepdims=True)
    cnt_ref[...] = jnp.broadcast_to(cnt, cnt_ref.shape)
    _prefix_count(cnt_ref, q_ref.at[0:SUBLANES])
    first = q_ref[0:1, :]
    srow = lax.broadcasted_iota(jnp.int32, (SUBLANES, 1), 0)
    span_ref[0] = jnp.where(srow == 0, first, jnp.where(srow == 1, first + cnt, 0.0))
    lower = (lax.broadcasted_iota(jnp.int32, (e, e), 1)
             < lax.broadcasted_iota(jnp.int32, (e, e), 0)).astype(BF16)
    q_ref[...] = first + jnp.dot(lower, mask.astype(BF16), preferred_element_type=F32)

    tok = lax.broadcasted_iota(jnp.int32, (1, t), 1)
    tok_hi = (tok >> 6).astype(F32)
    tok_lo = (tok & 63).astype(F32)
    slot = lax.broadcasted_iota(jnp.int32, (cap, 1), 0).astype(F32)
    vrow = lax.broadcasted_iota(jnp.int32, (SUBLANES, 1), 0)
    dest_base = pl.program_id(0) * (e * cap)

    def compact(ei, carry):
        g = aff_ref[0, pl.ds(ei, 1), :]
        g_hi = g.astype(BF16).astype(F32)
        g_mid = (g - g_hi).astype(BF16).astype(F32)
        g_lo = g - g_hi - g_mid
        q = q_ref[pl.ds(ei, 1), :]
        q_hi = jnp.floor(q * (1.0 / 64.0))
        q_lo = q - 64.0 * q_hi
        vals = jnp.where(vrow == 0, tok_hi, jnp.where(vrow == 1, tok_lo, jnp.where(
            vrow == 2, g_hi, jnp.where(vrow == 3, g_mid, jnp.where(vrow == 4, g_lo, jnp.where(
                vrow == 5, q_hi, jnp.where(vrow == 6, q_lo, 0.0))))))).astype(BF16)
        hit = (pos_ref[pl.ds(ei, 1), :] == slot) & (mask_ref[pl.ds(ei, 1), :] > 0.0)
        onehot = jnp.where(hit, 1.0, 0.0).astype(BF16)
        res = lax.dot_general(vals, onehot, (((1,), (1,)), ((), ())), preferred_element_type=F32)
        idx_ref[0, pl.ds(ei, 1), :] = (res[0:1] * 64.0 + res[1:2]).astype(jnp.int32)
        gate_ref[0, pl.ds(ei, 1), :] = res[2:3] + res[3:4] + res[4:5]
        dest_ref[0, pl.ds(ei, 1), :] = (res[5:6] * 64.0 + res[6:7]).astype(jnp.int32) + dest_base
        return carry

    lax.fori_loop(0, e, compact, 0)


def _select_tokens(aff, cap):
    b, e, t = aff.shape
    assert t <= 64 * 256 and e * cap <= 64 * 256 and e >= SUBLANES
    spec = pl.BlockSpec((1, e, cap), lambda i: (i, 0, 0))
    return pl.pallas_call(
        functools.partial(_topk_kernel, cap),
        grid=(b,),
        in_specs=[pl.BlockSpec((1, e, t), lambda i: (i, 0, 0))],
        out_specs=[spec, spec, spec, pl.BlockSpec((1, SUBLANES, t), lambda i: (i, 0, 0))],
        out_shape=[jax.ShapeDtypeStruct((b, e, cap), jnp.int32), jax.ShapeDtypeStruct((b, e, cap), F32),
                   jax.ShapeDtypeStruct((b, e, cap), jnp.int32), jax.ShapeDtypeStruct((b, SUBLANES, t), F32)],
        scratch_shapes=[pltpu.VMEM((e, t), F32), pltpu.VMEM((e, t), F32),
                        pltpu.VMEM((SUBLANES, t), F32), pltpu.VMEM((e, t), F32)],
        compiler_params=_params(("parallel",)),
        name="select_tokens",
    )(aff)


def _gather_kernel(rows_ref, h_hbm, o_ref, buf, sem):
    i = pl.program_id(0)
    r_blk = buf.shape[1]

    def issue_block(blk, slot):
        def issue(p, carry):
            for prio in range(2):
                r = 2 * p + prio
                row = rows_ref[blk * r_blk + r]
                pltpu.make_async_copy(h_hbm.at[pl.ds(row, 1), :], buf.at[slot, pl.ds(r, 1), :],
                                      sem.at[slot]).start(priority=prio)
            return carry

        lax.fori_loop(0, r_blk // 2, issue, 0, unroll=4)

    @pl.when(i == 0)
    def _():
        issue_block(0, 0)

    @pl.when(i + 1 < pl.num_programs(0))
    def _():
        issue_block(i + 1, (i + 1) & 1)

    slot = i & 1
    pltpu.make_async_copy(h_hbm.at[pl.ds(0, r_blk), :], buf.at[slot], sem.at[slot]).wait()
    words = buf[slot]
    half = words.shape[1]
    for h in range(2):
        part = pltpu.unpack_elementwise(words, index=h, packed_dtype=BF16, unpacked_dtype=F32)
        o_ref[:, h * half:(h + 1) * half] = part.astype(o_ref.dtype)


def _gather_rows(rows, h):
    n_rows = rows.shape[0]
    words = h.shape[1]
    d = 2 * words
    r_blk = _tile(n_rows, 512)
    grid_spec = pltpu.PrefetchScalarGridSpec(
        num_scalar_prefetch=1,
        grid=(n_rows // r_blk,),
        in_specs=[pl.BlockSpec(memory_space=pl.ANY)],
        out_specs=pl.BlockSpec((r_blk, d), lambda i, rows_ref: (i, 0)),
        scratch_shapes=[pltpu.VMEM((2, r_blk, words), h.dtype), pltpu.SemaphoreType.DMA((2,))],
    )
    return pl.pallas_call(
        _gather_kernel,
        grid_spec=grid_spec,
        out_shape=jax.ShapeDtypeStruct((n_rows, d), BF16),
        compiler_params=_params(("arbitrary",)),
        name="gather_rows",
    )(rows, h)


def _expert_up_kernel(xs_ref, wg_ref, wu_ref, h_ref):
    xs = xs_ref[...]
    g = jnp.dot(xs, wg_ref[0].astype(BF16), preferred_element_type=F32)
    u = jnp.dot(xs, wu_ref[0].astype(BF16), preferred_element_type=F32)
    h_ref[...] = (g * jax.nn.sigmoid(g) * u).astype(h_ref.dtype)


def _expert_up(xs, w_gate, w_up, rows_per_expert):
    m, d = xs.shape
    f = w_gate.shape[2]
    tm = _tile(rows_per_expert, 1024)
    per_e = rows_per_expert // tm
    tf = _tile(f, 256)
    wspec = pl.BlockSpec((1, d, tf), lambda i, j: (i // per_e, 0, j))
    return pl.pallas_call(
        _expert_up_kernel,
        grid=(m // tm, f // tf),
        in_specs=[pl.BlockSpec((tm, d), lambda i, j: (i, 0)), wspec, wspec],
        out_specs=pl.BlockSpec((tm, tf), lambda i, j: (i, j)),
        out_shape=jax.ShapeDtypeStruct((m, f), BF16),
        compiler_params=_params(("parallel", "arbitrary")),
        name="expert_up",
    )(xs, w_gate, w_up)


DOWN_COL_TILE = 1024


def _expert_down_kernel(dest_ref, h_ref, wd_ref, g_ref, ys_hbm, ybuf, sem):
    i = pl.program_id(0)
    j = pl.program_id(1)
    n_i = pl.num_programs(0)
    n_j = pl.num_programs(1)
    _, tm, dw = ybuf.shape
    hw = wd_ref.shape[2] // 2
    slot = i & 1

    def wait_scatter(s):
        pltpu.make_async_copy(ybuf.at[s], ys_hbm.at[pl.ds(0, tm), :], sem.at[s]).wait()

    @pl.when((j == 0) & (i >= 2))
    def _():
        wait_scatter(slot)

    y = jnp.dot(h_ref[...], wd_ref[0].astype(BF16), preferred_element_type=F32) * g_ref[...]
    packed = pltpu.pack_elementwise([y[:, :hw], y[:, hw:]], packed_dtype=BF16)
    for jj in range(dw // hw):
        @pl.when(j == jj)
        def _(jj=jj):
            ybuf[slot, :, jj * hw:(jj + 1) * hw] = packed

    @pl.when(j == n_j - 1)
    def _():
        def issue(p, carry):
            for prio in range(2):
                r = 2 * p + prio
                dst = dest_ref[i * tm + r]
                pltpu.make_async_copy(ybuf.at[slot, pl.ds(r, 1), :], ys_hbm.at[pl.ds(dst, 1), :],
                                      sem.at[slot]).start(priority=prio)
            return carry

        lax.fori_loop(0, tm // 2, issue, 0, unroll=4)

        @pl.when(i == n_i - 1)
        def _():
            @pl.when(i >= 1)
            def _():
                wait_scatter(1 - slot)
            wait_scatter(slot)


def _expert_down(dest, h, w_down, gates, rows_per_expert):
    m, f = h.shape
    d = w_down.shape[2]
    tm = _tile(rows_per_expert, 1024)
    per_e = rows_per_expert // tm
    td = _tile(d, DOWN_COL_TILE)
    grid_spec = pltpu.PrefetchScalarGridSpec(
        num_scalar_prefetch=1,
        grid=(m // tm, d // td),
        in_specs=[pl.BlockSpec((tm, f), lambda i, j, dest_ref: (i, 0)),
                  pl.BlockSpec((1, f, td), lambda i, j, dest_ref: (i // per_e, 0, j)),
                  pl.BlockSpec((tm, 1), lambda i, j, dest_ref: (i, 0))],
        out_specs=pl.BlockSpec(memory_space=pl.ANY),
        scratch_shapes=[pltpu.VMEM((2, tm, d // 2), jnp.uint32), pltpu.SemaphoreType.DMA((2,))],
    )
    return pl.pallas_call(
        _expert_down_kernel,
        grid_spec=grid_spec,
        out_shape=jax.ShapeDtypeStruct((m, d // 2), jnp.uint32),
        compiler_params=_params(("arbitrary", "arbitrary")),
        name="expert_down",
    )(dest, h, w_down, gates)


def _combine_kernel(slots_per_batch, hw, tb_ref, span_ref, x_ref, gt_ref, g_ref, ys_hbm, o_ref,
                    buf, sem, acc_ref, parts_ref):
    bi = pl.program_id(0)
    i = pl.program_id(1)
    n_t = pl.num_programs(1)
    kc = buf.shape[1]
    total = ys_hbm.shape[0]
    tt = acc_ref.shape[1]
    n_sub = tt // COMBINE_SUB
    tb0 = bi * (n_t * n_sub + 1) + i * n_sub
    c_lo = tb_ref[tb0]
    c_hi = tb_ref[tb0 + n_sub]
    k_first = (c_lo >> 3) << 3
    n_chunks = (c_hi - k_first + kc - 1) // kc

    def chunk_start(c):
        return pl.multiple_of(jnp.minimum(k_first + c * kc, total - kc), SUBLANES)

    n_buf = buf.shape[0]

    def copy(c):
        slot = c & (n_buf - 1)
        return pltpu.make_async_copy(ys_hbm.at[pl.ds(chunk_start(c), kc), :], buf.at[slot], sem.at[slot])

    for ahead in range(n_buf - 1):
        @pl.when(ahead < n_chunks)
        def _(ahead=ahead):
            copy(ahead).start()

    base = (bi * slots_per_batch).astype(F32)
    acc_ref[...] = jnp.zeros_like(acc_ref)

    def body(c, carry):
        slot = c & (n_buf - 1)
        copy(c).wait()

        @pl.when(c + n_buf - 1 < n_chunks)
        def _():
            copy(c + n_buf - 1).start()

        k_lo = k_first + c * kc
        k_hi = chunk_start(c) + kc
        k = chunk_start(c) + lax.broadcasted_iota(jnp.int32, (1, kc), 1)
        k = jnp.where(k >= k_lo, k, -1).astype(F32)
        words = buf[slot]
        for half in range(2):
            part = pltpu.unpack_elementwise(words, index=half, packed_dtype=BF16, unpacked_dtype=F32)
            parts_ref[half] = part.astype(BF16)
        for s in range(n_sub):
            rows = slice(s * COMBINE_SUB, (s + 1) * COMBINE_SUB)

            @pl.when((tb_ref[tb0 + s] < k_hi) & (tb_ref[tb0 + s + 1] > k_lo))
            def _(rows=rows):
                first = span_ref[0, rows, 0:1] + base
                last = span_ref[0, rows, 1:2] + base
                onehot = jnp.where((k >= first) & (k < last), 1.0, 0.0).astype(BF16)
                for half in range(2):
                    acc_ref[half, rows] += jnp.dot(onehot, parts_ref[half], preferred_element_type=F32)
        return carry

    lax.fori_loop(0, n_chunks, body, 0)
    gate = gt_ref[0]
    g_final = g_ref[...]

    def finish(r, carry):
        rows = pl.ds(pl.multiple_of(r * FINISH_ROWS, FINISH_ROWS), FINISH_ROWS)
        lo, hi = acc_ref[0, rows, :], acc_ref[1, rows, :]
        moe = jnp.concatenate([part[:, j * hw:(j + 1) * hw] for j in range(lo.shape[1] // hw)
                               for part in (lo, hi)], axis=1)
        x = x_ref[0, rows, :] + gate * moe
        o_ref[0, rows, :] = x * lax.rsqrt(jnp.mean(x * x, axis=-1, keepdims=True) + EPS) * g_final
        return carry

    lax.fori_loop(0, tt // FINISH_ROWS, finish, 0)


COMBINE_ROWS = 512
COMBINE_SUB = 128
COMBINE_CHUNK = 256
COMBINE_RING = 4
FINISH_ROWS = 64


def _combine(tile_bounds, span, ys, x1, gate, g_final, slots_per_batch):
    b, t, d = x1.shape
    tt = _tile(t, COMBINE_ROWS)
    assert tt % COMBINE_SUB == 0
    kc = min(COMBINE_CHUNK, ys.shape[0])
    grid_spec = pltpu.PrefetchScalarGridSpec(
        num_scalar_prefetch=1,
        grid=(b, t // tt),
        in_specs=[pl.BlockSpec((1, tt, SUBLANES), lambda bi, i, tb: (bi, i, 0)),
                  pl.BlockSpec((1, tt, d), lambda bi, i, tb: (bi, i, 0)),
                  pl.BlockSpec((1, 1, d), lambda bi, i, tb: (bi, 0, 0)),
                  pl.BlockSpec((1, d), lambda bi, i, tb: (0, 0)),
                  pl.BlockSpec(memory_space=pl.ANY)],
        out_specs=pl.BlockSpec((1, tt, d), lambda bi, i, tb: (bi, i, 0)),
        scratch_shapes=[pltpu.VMEM((COMBINE_RING, kc, d // 2), jnp.uint32),
                        pltpu.SemaphoreType.DMA((COMBINE_RING,)),
                        pltpu.VMEM((2, tt, d // 2), F32), pltpu.VMEM((2, kc, d // 2), BF16)],
    )
    return pl.pallas_call(
        functools.partial(_combine_kernel, slots_per_batch, _tile(d, DOWN_COL_TILE) // 2),
        grid_spec=grid_spec,
        out_shape=jax.ShapeDtypeStruct((b, t, d), F32),
        compiler_params=_params(("arbitrary", "arbitrary")),
        name="combine",
    )(tile_bounds, span, x1, gate, g_final.reshape(1, d), ys)


def kernel(x, c, ctx, c_ctx, w_mod, b_mod, g_mix, g_ffn, w_in, b_in, hy_conv_w, hy_conv_b, hy_f_w1, hy_f_b1, hy_f_w2, hy_f_b2, hy_f_w3, hy_f_b3, hy_f_wout, hy_f_freq, hy_bias, lru_conv_w, lru_conv_b, lru_wa, lru_ba, lru_wx, lru_bx, lru_lambda, w_out, b_out, w_router, w_exp_gate, w_exp_up, w_exp_down, g_final):
    bsz, n_lat, d = x.shape
    n_ctx = ctx.shape[1]
    depth = w_mod.shape[0]
    c_hy = hy_bias.shape[1]
    c_lru = lru_conv_b.shape[1]
    in_gate = 3 * c_hy
    in_x = in_gate + c_lru
    n_exp = w_router.shape[2]
    cap = CAPACITY_FACTOR * n_lat // n_exp
    assert depth == 1, "context residual updates are only needed for depth > 1"
    assert n_lat % GRID_W == 0 and n_ctx & (n_ctx - 1) == 0

    rows = -(-(bsz + 1) // SUBLANES) * SUBLANES
    c_all = jnp.zeros((rows, d), F32).at[:bsz].set(c).at[bsz].set(c_ctx)
    mats = _fft_matrices(n_lat)

    l = 0
    mod = _modulation(c_all, w_mod[l], b_mod[l])
    mx = mod[:bsz].reshape(bsz, 1, N_MOD, d)
    sh1, sc1, gt1, sh2, sc2, gt2 = (mx[:, :, k] for k in range(N_MOD))
    mc = mod[bsz].reshape(1, 1, N_MOD, d)
    csh1, csc1 = mc[:, :, 0], mc[:, :, 1]

    w_in_b = w_in[l].astype(BF16)
    w_out_b = w_out[l].astype(BF16)

    hx = _norm_mod(x, g_mix[l], sh1, sc1, True)
    hc = _norm_mod(ctx, g_mix[l], csh1, csc1, False)
    hx = hx.reshape(bsz * n_lat, d)
    x0, wf = _projection_hyena(hx, w_in_b, b_in[l], hy_conv_w[l], hy_conv_b[l], c_hy)
    x0, wf = x0.reshape(bsz, n_lat, c_hy), wf.reshape(bsz, n_lat, c_hy)
    p_lru = _projection(hx, w_in_b, b_in[l], in_gate, 2 * c_lru, F32)
    p_lru = p_lru.reshape(bsz, n_lat, 2 * c_lru)
    pc_lx = _projection(hc.reshape(bsz * n_ctx, d), w_in_b, b_in[l], in_x, c_lru, F32)
    pc_lx = pc_lx.reshape(bsz, n_ctx, c_lru)

    lru_args = (lru_conv_w[l], lru_conv_b[l], lru_wa[l], lru_ba[l], lru_wx[l], lru_bx[l], lru_lambda[l])
    (h_ctx,) = _rglru(pc_lx, 0, None, 0, *lru_args, jnp.zeros((bsz, 2, c_lru), F32), n_ctx)
    y_lru, _ = _rglru(p_lru, c_lru, p_lru, 0, *lru_args, h_ctx, GRID_W)

    taps, norm = _hyena_filter(n_lat, hy_f_w1[l], hy_f_b1[l], hy_f_w2[l], hy_f_b2[l], hy_f_w3[l],
                               hy_f_b3[l], hy_f_wout[l], hy_f_freq[l])
    kspec = _filter_spectrum(taps, norm, mats, c_hy)
    yspec = _conv_forward(wf, kspec, mats)
    y_hy = _conv_inverse(yspec, wf, x0, hy_bias[l], mats)

    x1 = _out_projection(y_hy.reshape(bsz * n_lat, c_hy), y_lru.reshape(bsz * n_lat, c_lru), w_out_b,
                         b_out[l], x.reshape(bsz * n_lat, d), gt1, n_lat)
    x1 = x1.reshape(bsz, n_lat, d)

    h2, aff = _router(x1, g_ffn[l], sh2, sc2, w_router[l])
    idx, gates, dest, span = _select_tokens(aff, cap)
    rows_g = (idx + (jnp.arange(bsz, dtype=jnp.int32) * n_lat)[:, None, None])
    expert_major = lambda v: jnp.swapaxes(v, 0, 1).reshape(-1)
    xs = _gather_rows(expert_major(rows_g), h2)
    hmid = _expert_up(xs, w_exp_gate[l], w_exp_up[l], bsz * cap)
    ys = _expert_down(expert_major(dest), hmid, w_exp_down[l], expert_major(gates).reshape(-1, 1), bsz * cap)
    slots = n_exp * cap
    base = (jnp.arange(bsz, dtype=jnp.int32) * slots)[:, None]
    bounds = jnp.concatenate([span[:, 0, ::COMBINE_SUB].astype(jnp.int32) + base, base + slots], axis=1)
    bounds = bounds.reshape(-1)
    return _combine(bounds, jnp.swapaxes(span, 1, 2), ys, x1, gt2, g_final, slots)
```
